```python
import math
import jax, jax.numpy as jnp
from jax import lax
import numpy as np

D_MODEL = 2048
BATCH = 8
SEQ = 2048
DEPTH = 4

CTX_LEN = 256
GRID_W = 64
N_MIXERS = 2
EPS = 1e-6
DA_HEADS = 8
DA_QK_DIM = 128
DA_V_DIM = 2 * DA_QK_DIM
DA_QK_W = DA_HEADS * 2 * DA_QK_DIM
DA_V_W = DA_HEADS * DA_V_DIM
ROPE_BASE = 10000.0
Q_BLOCK = 128
GDN_QK_HEADS = 16
GDN_V_HEADS = 32
GDN_QK_DIM = 128
GDN_V_DIM = 128
GDN_QK_W = GDN_QK_HEADS * GDN_QK_DIM
GDN_V_W = GDN_V_HEADS * GDN_V_DIM
GDN_QKV_W = 2 * GDN_QK_W + GDN_V_W
GDN_IN_W = GDN_QKV_W + GDN_V_W + 4 * GDN_V_HEADS
GDN_CONV = 5
GDN_CHUNK = 64
D_FF = 5504
FFN_CONV = 3
N_DA_LAYERS = (DEPTH + 1) // 2
N_GDN_LAYERS = DEPTH // 2

kernel_name = 'hybrid_diffattn_gdeltanet_convffn_dit'


def rmsnorm(x, w):
    xf = x.astype(jnp.float32)
    y = xf * lax.rsqrt(jnp.mean(xf * xf, axis=-1, keepdims=True) + EPS)
    return (y * w.astype(jnp.float32)).astype(x.dtype)


def l2norm(x):
    xf = x.astype(jnp.float32)
    return (xf * lax.rsqrt(jnp.sum(xf * xf, axis=-1, keepdims=True) + EPS)).astype(x.dtype)


def dwconv_centred(x, w):
    k = w.shape[0]
    p = k // 2
    t = x.shape[1]
    xp = jnp.pad(x, ((0, 0), (p, p), (0, 0)))
    out = xp[:, 0:t] * w[0]
    for j in range(1, k):
        out = out + xp[:, j:j + t] * w[j]
    return out


def axial_rope_tables(t_len, rot_dim):
    t = jnp.arange(t_len, dtype=jnp.int32)
    rows = (t // GRID_W).astype(jnp.float32)
    cols = (t % GRID_W).astype(jnp.float32)
    n_freq = rot_dim // 4
    inv_freq = ROPE_BASE ** (-jnp.arange(n_freq, dtype=jnp.float32) / n_freq)
    ang = jnp.stack([rows[:, None] * inv_freq, cols[:, None] * inv_freq], axis=1)
    return jnp.cos(ang), jnp.sin(ang)


def apply_axial_rope(x, cos, sin):
    shp = x.shape
    xr = x.reshape(shp[:-1] + (2, 2, shp[-1] // 4))
    x1, x2 = xr[..., 0, :], xr[..., 1, :]
    c = cos.astype(x.dtype)
    s = sin.astype(x.dtype)
    out = jnp.stack([x1 * c - x2 * s, x2 * c + x1 * s], axis=-2)
    return out.reshape(shp)


def diff_attend(q, k, v, lam):
    s = jnp.einsum('bhcqd,bhckd->bhcqk', q, k).astype(jnp.float32) * (DA_QK_DIM ** -0.5)
    p = jax.nn.softmax(s, axis=-1)
    a = p[:, :, 0] - lam * p[:, :, 1]
    return jnp.einsum('bhqk,bhkd->bhqd', a.astype(v.dtype), v)


def diff_attention_mixer(h_ctx, h_lat, w_qkv, lam_vecs, head_gain, w_o, lambda_init, cos, sin, with_ctx_out):
    def project(h):
        b, t, _ = h.shape
        q, k, v = jnp.split(h @ w_qkv, [DA_QK_W, 2 * DA_QK_W], axis=-1)
        q = q.reshape(b, t, DA_HEADS, 2, DA_QK_DIM).transpose(0, 2, 3, 1, 4)
        k = k.reshape(b, t, DA_HEADS, 2, DA_QK_DIM).transpose(0, 2, 3, 1, 4)
        v = v.reshape(b, t, DA_HEADS, DA_V_DIM).transpose(0, 2, 1, 3)
        return q, k, v

    lv = lam_vecs.astype(jnp.float32)
    lam = jnp.exp(jnp.sum(lv[0] * lv[1])) - jnp.exp(jnp.sum(lv[2] * lv[3])) + lambda_init
    q_c, k_c, v_c = project(h_ctx)
    q_l, k_l, v_l = project(h_lat)
    q_l = apply_axial_rope(q_l, cos, sin)
    k_l = apply_axial_rope(k_l, cos, sin)
    k_all = jnp.concatenate([k_c, k_l], axis=3)
    v_all = jnp.concatenate([v_c, v_l], axis=2)
    b, h, _, t, d = q_l.shape
    nb = t // Q_BLOCK
    q_blocks = jnp.moveaxis(q_l.reshape(b, h, 2, nb, Q_BLOCK, d), 3, 0)
    o_blocks = lax.map(lambda qb: diff_attend(qb, k_all, v_all, lam), q_blocks)
    o_lat = jnp.moveaxis(o_blocks, 0, 2).reshape(b, h, t, DA_V_DIM)

    def out_proj(o):
        o = rmsnorm(o, head_gain) * (1.0 - lambda_init)
        bb, hh, tt, dv = o.shape
        return o.transpose(0, 2, 1, 3).reshape(bb, tt, hh * dv) @ w_o

    y_lat = out_proj(o_lat)
    y_ctx = out_proj(diff_attend(q_c, k_c, v_c, lam)) if with_ctx_out else None
    return y_ctx, y_lat


def gdn_features(h, w_in, conv_w, a_log, dt_bias):
    b, t, _ = h.shape
    qkv, z, ab = jnp.split(h @ w_in, [GDN_QKV_W, GDN_QKV_W + GDN_V_W], axis=-1)
    qkv = jax.nn.silu(dwconv_centred(qkv, conv_w))
    q, k, v = jnp.split(qkv, [GDN_QK_W, 2 * GDN_QK_W], axis=-1)
    rep = GDN_V_HEADS // GDN_QK_HEADS
    q = jnp.repeat(l2norm(q.reshape(b, t, GDN_QK_HEADS, GDN_QK_DIM)), rep, axis=2) * (GDN_QK_DIM ** -0.5)
    k = jnp.repeat(l2norm(k.reshape(b, t, GDN_QK_HEADS, GDN_QK_DIM)), rep, axis=2)
    v = v.reshape(b, t, GDN_V_HEADS, GDN_V_DIM)
    z = z.reshape(b, t, GDN_V_HEADS, GDN_V_DIM)
    ab = ab.reshape(b, t, 2, 2, GDN_V_HEADS).astype(jnp.float32)
    beta = jax.nn.sigmoid(ab[:, :, :, 0])
    g = -jnp.exp(a_log.astype(jnp.float32)) * jax.nn.softplus(ab[:, :, :, 1] + dt_bias.astype(jnp.float32))
    return q, k, v, z, beta, g


def gated_delta_chunked(q, k, v, beta, g, s0, want_out):
    out_dtype = v.dtype
    b, t, h, dk = k.shape
    dv = v.shape[-1]
    n = t // GDN_CHUNK

    def chunks(a):
        a = a.astype(jnp.float32).reshape((b, n, GDN_CHUNK) + a.shape[2:])
        return jnp.swapaxes(a, 2, 3)

    k, v, beta, g = chunks(k), chunks(v), chunks(beta), chunks(g)
    gam = jnp.cumsum(g, axis=-1)
    idx = jnp.arange(GDN_CHUNK)
    incl = idx[:, None] >= idx[None, :]
    strict = idx[:, None] > idx[None, :]
    decay = jnp.exp(jnp.where(incl, gam[..., :, None] - gam[..., None, :], -jnp.inf))
    kk = jnp.einsum('bnhid,bnhjd->bnhij', k, k)
    a_mat = jnp.where(strict, beta[..., :, None] * kk * decay, 0.0)
    rhs = jnp.concatenate([v * beta[..., None], k * (beta * jnp.exp(gam))[..., None]], axis=-1)
    uw = lax.linalg.triangular_solve(a_mat, rhs, left_side=True, lower=True, unit_diagonal=True)
    u, w = uw[..., :dv], uw[..., dv:]
    k_dec = k * jnp.exp(gam[..., -1:] - gam)[..., None]
    g_last = jnp.exp(gam[..., -1])
    xs = [u, w, k_dec, g_last]
    if want_out:
        q = chunks(q)
        qk = jnp.where(incl, jnp.einsum('bnhid,bnhjd->bnhij', q, k) * decay, 0.0)
        xs = xs + [q * jnp.exp(gam)[..., None], qk]
    xs = [jnp.moveaxis(a, 1, 0) for a in xs]

    def step(state, inp):
        u_c, w_c, kd_c, gl_c = inp[0], inp[1], inp[2], inp[3]
        v_new = u_c - jnp.einsum('bhck,bhkv->bhcv', w_c, state)
        new_state = state * gl_c[..., None, None] + jnp.einsum('bhck,bhcv->bhkv', kd_c, v_new)
        if want_out:
            o = jnp.einsum('bhck,bhkv->bhcv', inp[4], state) + jnp.einsum('bhij,bhjv->bhiv', inp[5], v_new)
            return new_state, o
        return new_state, None

    s_final, o = lax.scan(step, s0, xs)
    if want_out:
        o = jnp.swapaxes(jnp.moveaxis(o, 0, 1), 2, 3).reshape(b, t, h, dv).astype(out_dtype)
    return o, s_final


def gdn_output(o, z, norm_gain, w_o):
    b, t, h, dv = o.shape
    y = rmsnorm(o, norm_gain) * jax.nn.silu(z)
    return y.reshape(b, t, h * dv) @ w_o


def gated_deltanet_mixer(h_ctx, h_lat, w_in, conv_w, a_log, dt_bias, norm_gain, w_o, with_ctx_out):
    q_c, k_c, v_c, z_c, beta_c, g_c = gdn_features(h_ctx, w_in, conv_w, a_log, dt_bias)
    q_l, k_l, v_l, z_l, beta_l, g_l = gdn_features(h_lat, w_in, conv_w, a_log, dt_bias)
    s0 = jnp.zeros((h_lat.shape[0], GDN_V_HEADS, GDN_QK_DIM, GDN_V_DIM), jnp.float32)

    def flip(a):
        return jnp.flip(a, axis=1)

    o_cf, s_cf = gated_delta_chunked(q_c, k_c, v_c, beta_c[:, :, 0], g_c[:, :, 0], s0, with_ctx_out)
    o_cb, s_cb = gated_delta_chunked(flip(q_c), flip(k_c), flip(v_c), flip(beta_c[:, :, 1]), flip(g_c[:, :, 1]), s0, with_ctx_out)
    o_lf, _ = gated_delta_chunked(q_l, k_l, v_l, beta_l[:, :, 0], g_l[:, :, 0], s_cf, True)
    o_lb, _ = gated_delta_chunked(flip(q_l), flip(k_l), flip(v_l), flip(beta_l[:, :, 1]), flip(g_l[:, :, 1]), s_cb, True)
    y_lat = gdn_output(o_lf + flip(o_lb), z_l, norm_gain, w_o)
    y_ctx = gdn_output(o_cf + flip(o_cb), z_c, norm_gain, w_o) if with_ctx_out else None
    return y_ctx, y_lat


def conv_ffn(h, w_up, conv_w, w_down):
    u = dwconv_centred(h @ w_up, conv_w)
    gate, val = jnp.split(u, 2, axis=-1)
    return (jax.nn.silu(gate) * val) @ w_down


def setup_inputs(seed: int = 0) -> dict:
    key = jax.random.key(seed)
    ks = jax.random.split(key, 24)

    def nrm(k, shape, scale):
        return jax.random.normal(k, shape, jnp.float32) * scale

    dt = jnp.exp(jax.random.uniform(ks[15], (N_GDN_LAYERS, 2, GDN_V_HEADS), jnp.float32, math.log(1e-3), math.log(1e-1)))
    return {
        'x': nrm(ks[0], (BATCH, SEQ, D_MODEL), 1.0),
        'c': nrm(ks[1], (BATCH, D_MODEL), 1.0),
        'ctx': nrm(ks[2], (BATCH, CTX_LEN, D_MODEL), 1.0),
        'c_ctx': nrm(ks[3], (D_MODEL,), 1.0),
        'w_mod': nrm(ks[4], (DEPTH, D_MODEL, 6 * D_MODEL), 0.5 * D_MODEL ** -0.5),
        'b_mod': nrm(ks[5], (DEPTH, 6 * D_MODEL), 0.02),
        'norm_mix': 1.0 + nrm(ks[6], (DEPTH, D_MODEL), 0.02),
        'norm_ffn': 1.0 + nrm(ks[7], (DEPTH, D_MODEL), 0.02),
        'da_w_qkv': nrm(ks[8], (N_DA_LAYERS, D_MODEL, 2 * DA_QK_W + DA_V_W), D_MODEL ** -0.5),
        'da_lambda': nrm(ks[9], (N_DA_LAYERS, 4, DA_QK_DIM), 0.1),
        'da_head_gain': 1.0 + nrm(ks[10], (N_DA_LAYERS, DA_V_DIM), 0.02),
        'da_w_o': nrm(ks[11], (N_DA_LAYERS, DA_V_W, D_MODEL), DA_V_W ** -0.5),
        'gdn_w_in': nrm(ks[12], (N_GDN_LAYERS, D_MODEL, GDN_IN_W), D_MODEL ** -0.5),
        'gdn_conv': nrm(ks[13], (N_GDN_LAYERS, GDN_CONV, GDN_QKV_W), GDN_CONV ** -0.5),
        'gdn_a_log': jnp.log(jax.random.uniform(ks[14], (N_GDN_LAYERS, 2, GDN_V_HEADS), jnp.float32, 1.0, 16.0)),
        'gdn_dt_bias': dt + jnp.log(-jnp.expm1(-dt)),
        'gdn_norm_gain': 1.0 + nrm(ks[16], (N_GDN_LAYERS, GDN_V_DIM), 0.02),
        'gdn_w_o': nrm(ks[17], (N_GDN_LAYERS, GDN_V_W, D_MODEL), GDN_V_W ** -0.5),
        'ffn_w_up': nrm(ks[18], (DEPTH, D_MODEL, 2 * D_FF), D_MODEL ** -0.5),
        'ffn_conv': nrm(ks[19], (DEPTH, FFN_CONV, 2 * D_FF), FFN_CONV ** -0.5),
        'ffn_w_down': nrm(ks[20], (DEPTH, D_FF, D_MODEL), D_FF ** -0.5),
        'final_norm': 1.0 + nrm(ks[21], (D_MODEL,), 0.02),
    }


def reference(x, c, ctx, c_ctx, w_mod, b_mod, norm_mix, norm_ffn, da_w_qkv, da_lambda, da_head_gain, da_w_o,
              gdn_w_in, gdn_conv, gdn_a_log, gdn_dt_bias, gdn_norm_gain, gdn_w_o, ffn_w_up, ffn_conv, ffn_w_down,
              final_norm):
    seq = x.shape[1]
    cos, sin = axial_rope_tables(seq, DA_QK_DIM)
    silu_c = jax.nn.silu(c)[:, None, :]
    silu_cc = jax.nn.silu(c_ctx)[None, None, :]
    for i in range(DEPTH):
        last = i == DEPTH - 1
        m_l = jnp.split(silu_c @ w_mod[i] + b_mod[i], 6, axis=-1)
        m_c = jnp.split(silu_cc @ w_mod[i] + b_mod[i], 6, axis=-1)
        h_lat = rmsnorm(x, norm_mix[i]) * (1 + m_l[1]) + m_l[0]
        h_ctx = rmsnorm(ctx, norm_mix[i]) * (1 + m_c[1]) + m_c[0]
        j = i // N_MIXERS
        if i % N_MIXERS == 0:
            lambda_init = 0.8 - 0.6 * math.exp(-0.3 * i)
            y_ctx, y_lat = diff_attention_mixer(h_ctx, h_lat, da_w_qkv[j], da_lambda[j], da_head_gain[j], da_w_o[j],
                                                lambda_init, cos, sin, not last)
        else:
            y_ctx, y_lat = gated_deltanet_mixer(h_ctx, h_lat, gdn_w_in[j], gdn_conv[j], gdn_a_log[j], gdn_dt_bias[j],
                                                gdn_norm_gain[j], gdn_w_o[j], not last)
        x = x + m_l[2] * y_lat
        h_lat = rmsnorm(x, norm_ffn[i]) * (1 + m_l[4]) + m_l[3]
        x = x + m_l[5] * conv_ffn(h_lat, ffn_w_up[i], ffn_conv[i], ffn_w_down[i])
        if not last:
            ctx = ctx + m_c[2] * y_ctx
            h_ctx = rmsnorm(ctx, norm_ffn[i]) * (1 + m_c[4]) + m_c[3]
            ctx = ctx + m_c[5] * conv_ffn(h_ctx, ffn_w_up[i], ffn_conv[i], ffn_w_down[i])
    return rmsnorm(x, final_norm)
```

```python
import functools
import math

import jax
import jax.numpy as jnp
from jax import lax
from jax.experimental import pallas as pl
from jax.experimental.pallas import tpu as pltpu

F32 = jnp.float32
BF16 = jnp.bfloat16

D_MODEL = 2048
DEPTH = 4
CTX_LEN = 256
GRID_W = 64
EPS = 1e-6
DA_HEADS = 8
DA_QK_DIM = 128
DA_V_DIM = 256
DA_QK_W = DA_HEADS * 2 * DA_QK_DIM
DA_V_W = DA_HEADS * DA_V_DIM
ROPE_BASE = 10000.0
GDN_QK_HEADS = 16
GDN_V_HEADS = 32
GDN_DIM = 128
GDN_QK_W = GDN_QK_HEADS * GDN_DIM
GDN_V_W = GDN_V_HEADS * GDN_DIM
GDN_QKV_W = 2 * GDN_QK_W + GDN_V_W
GDN_CONV = 5
CHUNK = 64
D_FF = 5504
FFN_CONV = 3

LANES = 128
HALO = 16
D_FF_PAD = 5632
VMEM_LIMIT = 56 * 1024 * 1024

NT_DIMS = (((1,), (1,)), ((), ()))
TN_DIMS = (((0,), (0,)), ((), ()))


def _silu(x):
    return x * jax.nn.sigmoid(x)


def _params(sem):
    return pltpu.CompilerParams(dimension_semantics=sem, vmem_limit_bytes=VMEM_LIMIT)


def _mods_kernel(c_ref, w_ref, b_ref, o_ref):
    s = _silu(c_ref[...]).astype(BF16)
    w = w_ref[0].astype(BF16)
    o_ref[0] = jnp.dot(s, w, preferred_element_type=F32) + b_ref[0]


def _mods_call(cvec, w_mod, b_mod):
    depth, d, n = w_mod.shape
    tn = 1024
    return pl.pallas_call(
        _mods_kernel,
        out_shape=jax.ShapeDtypeStruct((depth, cvec.shape[0], n), F32),
        grid=(depth, n // tn),
        in_specs=[
            pl.BlockSpec((cvec.shape[0], d), lambda i, j: (0, 0)),
            pl.BlockSpec((1, d, tn), lambda i, j: (i, 0, j)),
            pl.BlockSpec((1, 1, tn), lambda i, j: (i, 0, j)),
        ],
        out_specs=pl.BlockSpec((1, cvec.shape[0], tn), lambda i, j: (i, 0, j)),
        compiler_params=_params(("parallel", "parallel")),
        name="adaln_mods",
    )(cvec, w_mod, b_mod.reshape(depth, 1, n))


def _norm_mod(xv, g, nw, modv, ctx_len):
    y = xv * lax.rsqrt(jnp.mean(xv * xv, axis=-1, keepdims=True) + EPS) * nw
    is_ctx = g < ctx_len
    shift = jnp.where(is_ctx, modv[2:3], modv[0:1])
    scale = jnp.where(is_ctx, modv[3:4], modv[1:2])
    return (y * (1.0 + scale) + shift).astype(BF16)


def _token_conv(u, cw, g, taps, tm, ctx_len, tok):
    p = taps // 2
    acc = None
    for jj in range(taps):
        dlt = jj - p
        sh = u[HALO + dlt:HALO + dlt + tm]
        if dlt != 0:
            gd = g + dlt
            same = ((gd < ctx_len) & (g < ctx_len)) | ((gd >= ctx_len) & (g >= ctx_len))
            ok = (gd >= 0) & (gd < tok) & same
            sh = jnp.where(ok, sh, 0.0)
        term = sh * cw[jj:jj + 1]
        acc = term if acc is None else acc + term
    return acc


def _proj_kernel(*refs, n_w, taps, epi, tm, tn, ctx_len, tok, q_tiles):
    halo = taps > 0
    it = iter(refs)
    x_ref = next(it)
    xp_ref = next(it) if halo else None
    xn_ref = next(it) if halo else None
    modv_ref = next(it)
    nw_ref = next(it)
    w_refs = [next(it) for _ in range(n_w)]
    cw_refs = [next(it) for _ in range(n_w)] if halo else []
    ex_refs = [next(it) for _ in range(2)] if epi == "rope" else []
    out_ref = next(it)
    h_ref = next(it)

    t = pl.program_id(1)
    j = pl.program_id(2)
    off = HALO if halo else 0
    rows = 128

    @pl.when(j == 0)
    def _():
        modv = modv_ref[0]
        nw = nw_ref[...]
        for r0 in range(0, tm, rows):
            g = t * tm + r0 + lax.broadcasted_iota(jnp.int32, (rows, 1), 0)
            h_ref[off + r0:off + r0 + rows] = _norm_mod(x_ref[0, r0:r0 + rows], g, nw, modv, ctx_len)
        if halo:
            gi = lax.broadcasted_iota(jnp.int32, (HALO, 1), 0)
            h_ref[0:HALO] = _norm_mod(xp_ref[0], t * tm - HALO + gi, nw, modv, ctx_len)
            h_ref[off + tm:off + tm + HALO] = _norm_mod(xn_ref[0], t * tm + tm + gi, nw, modv, ctx_len)

    hv = h_ref[...]
    us = [jnp.dot(hv, w[...], preferred_element_type=F32) for w in w_refs]
    g = t * tm + lax.broadcasted_iota(jnp.int32, (tm, 1), 0)
    if halo:
        us = [_token_conv(u, cw[...], g, taps, tm, ctx_len, tok) for u, cw in zip(us, cw_refs)]

    if epi == "ffn":
        out_ref[0] = (_silu(us[0]) * us[1]).astype(out_ref.dtype)
    elif epi == "plain":
        out_ref[0] = us[0].astype(out_ref.dtype)
    elif epi == "gdn_v":
        a = _silu(us[0])
        for hh in range(tn // LANES):
            out_ref[0, hh] = a[:, hh * LANES:(hh + 1) * LANES].astype(out_ref.dtype)
    elif epi == "gdn_qk":
        a = _silu(us[0])
        qs = jnp.where(j < q_tiles, GDN_DIM ** -0.5, 1.0).astype(F32)
        for hh in range(tn // LANES):
            xs = a[:, hh * LANES:(hh + 1) * LANES]
            nrm = xs * lax.rsqrt(jnp.sum(xs * xs, axis=-1, keepdims=True) + EPS)
            out_ref[0, hh] = (nrm * qs).astype(out_ref.dtype)
    elif epi == "rope":
        u = us[0]
        cos = jnp.where(j < 2 * q_tiles, ex_refs[0][...], 1.0)
        sin = jnp.where(j < 2 * q_tiles, ex_refs[1][...], 0.0)
        lane = lax.broadcasted_iota(jnp.int32, (1, tn), 1)
        swapped = jnp.where((lane % 64) < 32, pltpu.roll(u, tn - 32, 1), pltpu.roll(u, 32, 1))
        sc = jnp.where(j < q_tiles, DA_QK_DIM ** -0.5, 1.0).astype(F32)
        out_ref[0] = ((u * cos + swapped * sin) * sc).astype(out_ref.dtype)
    else:
        raise ValueError(epi)


def _proj_call(x, modv, nw, ws, cws, *, epi, tn, tm, out_dtype=BF16, extras=(), q_tiles=0,
               ctx_len=CTX_LEN):
    b, tok, d = x.shape
    n = ws[0].shape[1]
    taps = cws[0].shape[0] if cws else 0
    halo = taps > 0
    nt = tok // tm
    hb = tm // HALO
    nhb = tok // HALO

    in_specs = [pl.BlockSpec((1, tm, d), lambda bi, t, j: (bi, t, 0))]
    args = [x]
    if halo:
        in_specs += [
            pl.BlockSpec((1, HALO, d), lambda bi, t, j: (bi, jnp.maximum(t * hb - 1, 0), 0)),
            pl.BlockSpec((1, HALO, d), lambda bi, t, j: (bi, jnp.minimum((t + 1) * hb, nhb - 1), 0)),
        ]
        args += [x, x]
    in_specs += [pl.BlockSpec((1, 4, d), lambda bi, t, j: (bi, 0, 0)),
                 pl.BlockSpec((1, d), lambda bi, t, j: (0, 0))]
    args += [modv, nw.reshape(1, d)]
    for w in ws:
        in_specs.append(pl.BlockSpec((d, tn), lambda bi, t, j: (0, j)))
        args.append(w)
    for cw in cws:
        in_specs.append(pl.BlockSpec((taps, tn), lambda bi, t, j: (0, j)))
        args.append(cw)
    for e in extras:
        in_specs.append(pl.BlockSpec((tm, tn), lambda bi, t, j: (t, 0)))
        args.append(e)

    if epi in ("gdn_v", "gdn_qk"):
        out_shape = jax.ShapeDtypeStruct((b, n // LANES, tok, LANES), out_dtype)
        out_spec = pl.BlockSpec((1, tn // LANES, tm, LANES), lambda bi, t, j: (bi, j, t, 0))
    else:
        out_shape = jax.ShapeDtypeStruct((b, tok, n), out_dtype)
        out_spec = pl.BlockSpec((1, tm, tn), lambda bi, t, j: (bi, t, j))

    kern = functools.partial(_proj_kernel, n_w=len(ws), taps=taps, epi=epi, tm=tm, tn=tn,
                             ctx_len=ctx_len, tok=tok, q_tiles=q_tiles)
    return pl.pallas_call(
        kern,
        out_shape=out_shape,
        grid=(b, nt, n // tn),
        in_specs=in_specs,
        out_specs=out_spec,
        scratch_shapes=[pltpu.VMEM((tm + (2 * HALO if halo else 0), d), BF16)],
        compiler_params=_params(("parallel", "parallel", "arbitrary")),
        name="norm_proj_" + epi,
    )(*args)


def _row_gate(g_ref, t, tm, ctx_len):
    g = t * tm + lax.broadcasted_iota(jnp.int32, (tm, 1), 0)
    return jnp.where(g < ctx_len, g_ref[0, 1:2], g_ref[0, 0:1])


def _oproj_kernel(y_ref, w_ref, x_ref, g_ref, o_ref, *, tm, ctx_len):
    acc = jnp.dot(y_ref[0], w_ref[...], preferred_element_type=F32)
    o_ref[0] = x_ref[0] + _row_gate(g_ref, pl.program_id(1), tm, ctx_len) * acc


def _oproj_call(y, w, x, gate, *, tm, tn, ctx_len=CTX_LEN):
    b, tok, d = x.shape
    k = y.shape[-1]
    return pl.pallas_call(
        functools.partial(_oproj_kernel, tm=tm, ctx_len=ctx_len),
        out_shape=jax.ShapeDtypeStruct(x.shape, F32),
        grid=(b, tok // tm, d // tn),
        in_specs=[
            pl.BlockSpec((1, tm, k), lambda bi, t, j: (bi, t, 0)),
            pl.BlockSpec((k, tn), lambda bi, t, j: (0, j)),
            pl.BlockSpec((1, tm, tn), lambda bi, t, j: (bi, t, j)),
            pl.BlockSpec((1, 2, tn), lambda bi, t, j: (bi, 0, j)),
        ],
        out_specs=pl.BlockSpec((1, tm, tn), lambda bi, t, j: (bi, t, j)),
        compiler_params=_params(("parallel", "parallel", "arbitrary")),
        name="out_proj_residual",
    )(y, w, x, gate)


def _gdn_oproj_kernel(o_ref, z_ref, gain_ref, w_ref, x_ref, g_ref, out_ref, y_ref, *, tm, ctx_len):
    j = pl.program_id(2)

    @pl.when(j == 0)
    def _():
        gain = gain_ref[...]
        for h in range(GDN_V_HEADS):
            o = o_ref[0, 0, h].astype(F32) + o_ref[1, 0, h].astype(F32)
            y = o * lax.rsqrt(jnp.mean(o * o, axis=-1, keepdims=True) + EPS) * gain
            z = z_ref[0, :, h * LANES:(h + 1) * LANES].astype(F32)
            y_ref[:, h * LANES:(h + 1) * LANES] = (y * _silu(z)).astype(BF16)

    acc = jnp.dot(y_ref[...], w_ref[...], preferred_element_type=F32)
    out_ref[0] = x_ref[0] + _row_gate(g_ref, pl.program_id(1), tm, ctx_len) * acc


def _gdn_oproj_call(o, z, gain, w, x, gate, *, tm, tn, ctx_len=CTX_LEN):
    b, tok, d = x.shape
    k = w.shape[0]
    return pl.pallas_call(
        functools.partial(_gdn_oproj_kernel, tm=tm, ctx_len=ctx_len),
        out_shape=jax.ShapeDtypeStruct(x.shape, F32),
        grid=(b, tok // tm, d // tn),
        in_specs=[
            pl.BlockSpec((2, 1, GDN_V_HEADS, tm, LANES), lambda bi, t, j: (0, bi, 0, t, 0)),
            pl.BlockSpec((1, tm, k), lambda bi, t, j: (bi, t, 0)),
            pl.BlockSpec((1, LANES), lambda bi, t, j: (0, 0)),
            pl.BlockSpec((k, tn), lambda bi, t, j: (0, j)),
            pl.BlockSpec((1, tm, tn), lambda bi, t, j: (bi, t, j)),
            pl.BlockSpec((1, 2, tn), lambda bi, t, j: (bi, 0, j)),
        ],
        out_specs=pl.BlockSpec((1, tm, tn), lambda bi, t, j: (bi, t, j)),
        scratch_shapes=[pltpu.VMEM((tm, k), BF16)],
        compiler_params=_params(("parallel", "parallel", "arbitrary")),
        name="gdn_out_proj_residual",
    )(o, z, gain.reshape(1, LANES), w, x, gate)


def _diff_attn_kernel(q_ref, k_ref, v_ref, lam_ref, gain_ref, o_ref, *, lam_init, ctx_len):
    lv = lam_ref[...]
    lam = (jnp.exp(jnp.sum(lv[0:1] * lv[1:2], axis=-1, keepdims=True))
           - jnp.exp(jnp.sum(lv[2:3] * lv[3:4], axis=-1, keepdims=True)) + lam_init)

    def attend(kk, vv):
        q = q_ref[0]
        ps = []
        for c in range(2):
            s = lax.dot_general(q[:, c * DA_QK_DIM:(c + 1) * DA_QK_DIM],
                                kk[:, c * DA_QK_DIM:(c + 1) * DA_QK_DIM], NT_DIMS,
                                preferred_element_type=F32)
            p = jnp.exp(s - jnp.max(s, axis=-1, keepdims=True))
            ps.append(p * (1.0 / jnp.sum(p, axis=-1, keepdims=True)))
        a = ps[0] - lam * ps[1]
        o = jnp.dot(a.astype(BF16), vv, preferred_element_type=F32)
        o = o * lax.rsqrt(jnp.mean(o * o, axis=-1, keepdims=True) + EPS) * gain_ref[...]
        o_ref[0] = (o * (1.0 - lam_init)).astype(o_ref.dtype)

    t = pl.program_id(2)

    @pl.when(t == 0)
    def _():
        attend(k_ref[0, 0:ctx_len], v_ref[0, 0:ctx_len])

    @pl.when(t != 0)
    def _():
        attend(k_ref[0], v_ref[0])


def _diff_attn_call(qkv, lam_vecs, head_gain, lam_init, *, ctx_len=CTX_LEN):
    b, tok, _ = qkv.shape
    hw = 2 * DA_QK_DIM
    tq = ctx_len
    return pl.pallas_call(
        functools.partial(_diff_attn_kernel, lam_init=lam_init, ctx_len=ctx_len),
        out_shape=jax.ShapeDtypeStruct((b, tok, DA_V_W), BF16),
        grid=(b, DA_HEADS, tok // tq),
        in_specs=[
            pl.BlockSpec((1, tq, hw), lambda bi, h, t: (bi, t, h)),
            pl.BlockSpec((1, tok, hw), lambda bi, h, t: (bi, 0, DA_HEADS + h)),
            pl.BlockSpec((1, tok, DA_V_DIM), lambda bi, h, t: (bi, 0, 2 * DA_HEADS + h)),
            pl.BlockSpec((4, DA_QK_DIM), lambda bi, h, t: (0, 0)),
            pl.BlockSpec((1, DA_V_DIM), lambda bi, h, t: (0, 0)),
        ],
        out_specs=pl.BlockSpec((1, tq, DA_V_DIM), lambda bi, h, t: (bi, t, h)),
        compiler_params=_params(("parallel", "parallel", "arbitrary")),
        name="diff_attention",
    )(qkv, qkv, qkv, lam_vecs, head_gain.reshape(1, DA_V_DIM))


def _split3(x):
    x1 = x.astype(BF16)
    r1 = x - x1.astype(F32)
    x2 = r1.astype(BF16)
    x3 = (r1 - x2.astype(F32)).astype(BF16)
    return x1, x2, x3


def _unit_lower_inverse(a2, base):
    c = CHUNK
    nt = c // 8
    lane = lax.broadcasted_iota(jnp.int32, (8, LANES), 1)
    sub = lax.broadcasted_iota(jnp.int32, (8, LANES), 0)
    col = lane % c
    first = lane < c
    row2 = lax.broadcasted_iota(jnp.int32, (c, LANES), 0)
    col2 = lax.broadcasted_iota(jnp.int32, (c, LANES), 1) % c

    a_diag = jnp.where((row2 // base) == (col2 // base), a2, 0.0)
    a_t = [a_diag[8 * r:8 * r + 8] for r in range(nt)]
    t_t = [jnp.where(col == 8 * r + sub, 1.0, 0.0).astype(F32) for r in range(nt)]
    for jcol in range(base - 1):
        for blk in range(c // base):
            rj = blk * base + jcol
            trow = t_t[rj // 8][rj % 8:rj % 8 + 1, :]
            for r in range(blk * base // 8, (blk + 1) * base // 8):
                if 8 * r + 7 <= rj:
                    continue
                acol = jnp.where(first, a_t[r][:, rj:rj + 1], a_t[r][:, c + rj:c + rj + 1])
                t_t[r] = t_t[r] - acol * trow
    t2 = jnp.concatenate(t_t, axis=0)

    first2 = lax.broadcasted_iota(jnp.int32, (c, LANES), 1) < c

    def blockdiag(x2):
        return jnp.concatenate([jnp.where(first2, x2, 0.0), jnp.where(first2, 0.0, x2)], axis=0)

    tbd = blockdiag(t2)
    size = base
    while size < c:
        q2 = jnp.where(((row2 // (2 * size)) == (col2 // (2 * size))) & ((row2 // size) != (col2 // size)),
                       a2, 0.0)
        tb = tbd.astype(BF16)
        tq = jnp.dot(tb, blockdiag(q2).astype(BF16), preferred_element_type=F32)
        tbd = tbd - jnp.dot(tq.astype(BF16), tb, preferred_element_type=F32)
        size *= 2
    return tbd


def _gdn_scan_kernel(q_ref, k_ref, v_ref, ab_ref, alog_ref, dt_ref, o_ref,
                     s_ref, cg_ref, cb_ref, rg_ref, rb_ref, gt_ref, *, base):
    c = CHUNK
    d = pl.program_id(1)
    step = pl.program_id(2)
    rev = d == 1

    @pl.when(step == 0)
    def _():
        s_ref[...] = jnp.zeros(s_ref.shape, F32)

    ab = ab_ref[0]
    beta_all = jax.nn.sigmoid(ab)
    xs = ab + dt_ref[0]
    g_all = -jnp.exp(alog_ref[0]) * (jnp.maximum(xs, 0.0) + jnp.log1p(jnp.exp(-jnp.abs(xs))))
    ri = lax.broadcasted_iota(jnp.int32, (c, c), 0)
    ci = lax.broadcasted_iota(jnp.int32, (c, c), 1)
    sgn = 1 - 2 * d
    cum = jnp.where((ci - ri) * sgn <= 0, 1.0, 0.0).astype(BF16)
    gam_all = sum(jnp.dot(cum, part, preferred_element_type=F32) for part in _split3(g_all))
    gtot_all = jnp.sum(g_all, axis=0, keepdims=True)

    pi = lax.broadcasted_iota(jnp.int32, (GDN_QK_HEADS, LANES), 0)
    li = lax.broadcasted_iota(jnp.int32, (GDN_QK_HEADS, LANES), 1)
    zpad = jnp.zeros((c, LANES), BF16)

    def pair_rows(x, col0):
        sel_e = jnp.where(li == col0 + 2 * pi, 1.0, 0.0).astype(BF16)
        sel_o = jnp.where(li == col0 + 2 * pi + 1, 1.0, 0.0).astype(BF16)
        acc = jnp.zeros((GDN_QK_HEADS, LANES), F32)
        for part in _split3(x):
            acc = acc + lax.dot_general(sel_e, jnp.concatenate([part, zpad], axis=0), NT_DIMS,
                                        preferred_element_type=F32)
            acc = acc + lax.dot_general(sel_o, jnp.concatenate([zpad, part], axis=0), NT_DIMS,
                                        preferred_element_type=F32)
        return acc

    rg_ref[...] = pair_rows(gam_all, GDN_V_HEADS)
    rb_ref[...] = pair_rows(beta_all, 0)
    for h in range(GDN_V_HEADS):
        cg_ref[h] = jnp.broadcast_to(gam_all[:, GDN_V_HEADS + h:GDN_V_HEADS + h + 1], (c, LANES))
        cb_ref[h] = jnp.broadcast_to(beta_all[:, h:h + 1], (c, LANES))
        gt_ref[h:h + 1, :] = jnp.broadcast_to(gtot_all[:, GDN_V_HEADS + h:GDN_V_HEADS + h + 1], (1, LANES))

    row2 = lax.broadcasted_iota(jnp.int32, (c, LANES), 0)
    lane2 = lax.broadcasted_iota(jnp.int32, (c, LANES), 1)
    col2 = lane2 % c
    first2 = lane2 < c

    def pair_body(p, carry):
        q = q_ref[0, p]
        k = k_ref[0, p]
        kf = k.astype(F32)
        qf = q.astype(F32)
        qkk = lax.dot_general(jnp.concatenate([q, k], axis=0), jnp.concatenate([k, k], axis=0),
                              NT_DIMS, preferred_element_type=F32)
        qk2 = qkk[0:c]
        kk2 = qkk[c:2 * c]
        h0 = 2 * p
        gc2 = jnp.where(first2, cg_ref[h0], cg_ref[h0 + 1])
        bc2 = jnp.where(first2, cb_ref[h0], cb_ref[h0 + 1])
        gr2 = rg_ref[pl.ds(p, 1), :]
        br2 = rb_ref[pl.ds(p, 1), :]
        e2 = jnp.exp(-jnp.abs(gc2 - gr2))
        a2 = jnp.where(row2 > col2, jnp.where(rev, br2, bc2) * kk2 * e2, 0.0)
        qkm2 = jnp.where((row2 - col2) * sgn >= 0, qk2 * e2, 0.0)

        tbd = _unit_lower_inverse(a2, base)
        tuse = jnp.where(rev, tbd.T, tbd).astype(BF16)

        rhs = []
        for hh in range(2):
            cg = cg_ref[h0 + hh]
            cb = cb_ref[h0 + hh]
            vb = v_ref[0, h0 + hh].astype(F32) * cb
            kb = kf * (cb * jnp.exp(cg))
            rhs.append(jnp.concatenate([vb, kb], axis=1).astype(BF16))
        uw = jnp.dot(tuse, jnp.concatenate(rhs, axis=0), preferred_element_type=F32)

        for hh in range(2):
            h = h0 + hh
            cg = cg_ref[h]
            gt = gt_ref[pl.ds(h, 1), :]
            u = uw[hh * c:(hh + 1) * c, 0:GDN_DIM]
            w = uw[hh * c:(hh + 1) * c, GDN_DIM:2 * GDN_DIM]
            s_old = s_ref[h]
            wq = jnp.concatenate([w.astype(BF16), (qf * jnp.exp(cg)).astype(BF16)], axis=0)
            ws_qs = jnp.dot(wq, s_old.astype(BF16), preferred_element_type=F32)
            v_new = (u - ws_qs[0:c]).astype(BF16)
            qkm = qkm2[:, hh * c:(hh + 1) * c].astype(BF16)
            o = ws_qs[c:2 * c] + jnp.dot(qkm, v_new, preferred_element_type=F32)
            o_ref[0, 0, h] = o.astype(o_ref.dtype)
            k_dec = (kf * jnp.exp(gt - cg)).astype(BF16)
            s_ref[h] = s_old * jnp.exp(gt) + lax.dot_general(k_dec, v_new, TN_DIMS,
                                                            preferred_element_type=F32)
        return carry

    lax.fori_loop(0, GDN_QK_HEADS, pair_body, 0)


def _gdn_scan_call(qk, v, ab, alog, dt, *, ctx_len=CTX_LEN, base=16):
    b, _, tok, _ = qk.shape
    nc = tok // CHUNK
    ncc = ctx_len // CHUNK

    def chunk(dd, s):
        back = jnp.where(s < ncc, ncc - 1 - s, nc - 1 - (s - ncc))
        return jnp.where(dd == 0, s, back)

    return pl.pallas_call(
        functools.partial(_gdn_scan_kernel, base=base),
        out_shape=jax.ShapeDtypeStruct((2, b, GDN_V_HEADS, tok, LANES), BF16),
        grid=(b, 2, nc),
        in_specs=[
            pl.BlockSpec((1, GDN_QK_HEADS, CHUNK, LANES), lambda bi, dd, s: (bi, 0, chunk(dd, s), 0)),
            pl.BlockSpec((1, GDN_QK_HEADS, CHUNK, LANES), lambda bi, dd, s: (bi, 1, chunk(dd, s), 0)),
            pl.BlockSpec((1, GDN_V_HEADS, CHUNK, LANES), lambda bi, dd, s: (bi, 0, chunk(dd, s), 0)),
            pl.BlockSpec((1, CHUNK, LANES), lambda bi, dd, s: (bi, chunk(dd, s), dd)),
            pl.BlockSpec((1, 1, LANES), lambda bi, dd, s: (dd, 0, 0)),
            pl.BlockSpec((1, 1, LANES), lambda bi, dd, s: (dd, 0, 0)),
        ],
        out_specs=pl.BlockSpec((1, 1, GDN_V_HEADS, CHUNK, LANES),
                               lambda bi, dd, s: (dd, bi, 0, chunk(dd, s), 0)),
        scratch_shapes=[
            pltpu.VMEM((GDN_V_HEADS, GDN_DIM, GDN_DIM), F32),
            pltpu.VMEM((GDN_V_HEADS, CHUNK, LANES), F32),
            pltpu.VMEM((GDN_V_HEADS, CHUNK, LANES), F32),
            pltpu.VMEM((GDN_QK_HEADS, LANES), F32),
            pltpu.VMEM((GDN_QK_HEADS, LANES), F32),
            pltpu.VMEM((GDN_V_HEADS, LANES), F32),
        ],
        compiler_params=_params(("parallel", "parallel", "arbitrary")),
        name="gdn_chunk_scan",
    )(qk, qk, v, ab, alog, dt)


def _final_norm_kernel(x_ref, w_ref, o_ref):
    xv = x_ref[0]
    o_ref[0] = xv * lax.rsqrt(jnp.mean(xv * xv, axis=-1, keepdims=True) + EPS) * w_ref[...]


def _final_norm_call(x, w, *, ctx_len=CTX_LEN):
    b, tok, d = x.shape
    tr = ctx_len
    return pl.pallas_call(
        _final_norm_kernel,
        out_shape=jax.ShapeDtypeStruct((b, tok - ctx_len, d), F32),
        grid=(b, (tok - ctx_len) // tr),
        in_specs=[pl.BlockSpec((1, tr, d), lambda bi, t: (bi, t + 1, 0)),
                  pl.BlockSpec((1, d), lambda bi, t: (0, 0))],
        out_specs=pl.BlockSpec((1, tr, d), lambda bi, t: (bi, t, 0)),
        compiler_params=_params(("parallel", "parallel")),
        name="final_rmsnorm",
    )(x, w.reshape(1, d))


def _rope_tables(seq, ctx_len, width):
    t = jnp.arange(seq, dtype=jnp.int32)
    rows = (t // GRID_W).astype(F32)
    cols = (t % GRID_W).astype(F32)
    n_freq = DA_QK_DIM // 4
    inv_freq = ROPE_BASE ** (-jnp.arange(n_freq, dtype=F32) / n_freq)
    ar = rows[:, None] * inv_freq
    ac = cols[:, None] * inv_freq
    cos = jnp.concatenate([jnp.cos(ar), jnp.cos(ar), jnp.cos(ac), jnp.cos(ac)], axis=1)
    sin = jnp.concatenate([-jnp.sin(ar), jnp.sin(ar), -jnp.sin(ac), jnp.sin(ac)], axis=1)
    cos = jnp.concatenate([jnp.ones((ctx_len, DA_QK_DIM), F32), cos], axis=0)
    sin = jnp.concatenate([jnp.zeros((ctx_len, DA_QK_DIM), F32), sin], axis=0)
    reps = width // DA_QK_DIM
    return jnp.tile(cos, (1, reps)), jnp.tile(sin, (1, reps))


def kernel(x, c, ctx, c_ctx, w_mod, b_mod, norm_mix, norm_ffn, da_w_qkv, da_lambda, da_head_gain, da_w_o,
           gdn_w_in, gdn_conv, gdn_a_log, gdn_dt_bias, gdn_norm_gain, gdn_w_o, ffn_w_up, ffn_conv,
           ffn_w_down, final_norm):
    bsz, seq, d = x.shape
    ctx_len = ctx.shape[1]
    tm = (ctx_len + seq) // 3

    xa = jnp.concatenate([ctx, x], axis=1)
    pad_rows = 16 - (bsz + 1)
    cvec = jnp.concatenate([c, c_ctx[None, :], jnp.zeros((pad_rows, d), F32)], axis=0)
    mods_all = _mods_call(cvec, w_mod, b_mod)
    cos_t, sin_t = _rope_tables(seq, ctx_len, 512)

    def sel(mods, idx):
        lat = mods[:bsz, idx * d:(idx + 1) * d]
        cx = jnp.broadcast_to(mods[bsz:bsz + 1, idx * d:(idx + 1) * d], (bsz, d))
        return lat, cx

    def modv(mods, i_shift, i_scale):
        sl, sc = sel(mods, i_shift)
        cl, cc = sel(mods, i_scale)
        return jnp.stack([sl, cl, sc, cc], axis=1)

    def gatev(mods, idx):
        gl, gc = sel(mods, idx)
        return jnp.stack([gl, gc], axis=1)

    for i in range(DEPTH):
        mods = mods_all[i]
        jm = i // 2
        mv = modv(mods, 0, 1)
        if i % 2 == 0:
            lam_init = 0.8 - 0.6 * math.exp(-0.3 * i)
            qkv = _proj_call(xa, mv, norm_mix[i], [da_w_qkv[jm].astype(BF16)], [], epi="rope", tn=512,
                             tm=tm, extras=(cos_t, sin_t), q_tiles=DA_QK_W // 512, ctx_len=ctx_len)
            att = _diff_attn_call(qkv, da_lambda[jm], da_head_gain[jm], lam_init, ctx_len=ctx_len)
            xa = _oproj_call(att, da_w_o[jm].astype(BF16), xa, gatev(mods, 2), tm=tm, tn=512,
                             ctx_len=ctx_len)
        else:
            w_in = gdn_w_in[jm].astype(BF16)
            cw = gdn_conv[jm]
            qk = _proj_call(xa, mv, norm_mix[i], [w_in[:, :2 * GDN_QK_W]], [cw[:, :2 * GDN_QK_W]],
                            epi="gdn_qk", tn=512, tm=tm, q_tiles=GDN_QK_W // 512, ctx_len=ctx_len)
            vv = _proj_call(xa, mv, norm_mix[i], [w_in[:, 2 * GDN_QK_W:GDN_QKV_W]],
                            [cw[:, 2 * GDN_QK_W:GDN_QKV_W]], epi="gdn_v", tn=512, tm=tm, ctx_len=ctx_len)
            z = _proj_call(xa, mv, norm_mix[i], [w_in[:, GDN_QKV_W:GDN_QKV_W + GDN_V_W]], [],
                           epi="plain", tn=512, tm=tm, ctx_len=ctx_len)
            w_ab = w_in[:, GDN_QKV_W + GDN_V_W:].reshape(d, 2, 2 * GDN_V_HEADS)
            w_ab = jnp.concatenate([w_ab, jnp.zeros_like(w_ab)], axis=2).reshape(d, 2 * LANES)
            ab = _proj_call(xa, mv, norm_mix[i], [w_ab], [], epi="plain", tn=2 * LANES, tm=tm,
                            out_dtype=F32, ctx_len=ctx_len)
            zeros = jnp.zeros((2, GDN_V_HEADS), F32)
            alog = jnp.concatenate([zeros, gdn_a_log[jm], zeros, zeros], axis=1).reshape(2, 1, LANES)
            dtb = jnp.concatenate([zeros, gdn_dt_bias[jm], zeros, zeros], axis=1).reshape(2, 1, LANES)
            o = _gdn_scan_call(qk, vv, ab, alog, dtb, ctx_len=ctx_len)
            xa = _gdn_oproj_call(o, z, gdn_norm_gain[jm], gdn_w_o[jm].astype(BF16), xa, gatev(mods, 2),
                                 tm=tm // 2, tn=512, ctx_len=ctx_len)
        padc = D_FF_PAD - D_FF
        w_up = ffn_w_up[i].astype(BF16)
        w_g = jnp.pad(w_up[:, :D_FF], ((0, 0), (0, padc)))
        w_v = jnp.pad(w_up[:, D_FF:], ((0, 0), (0, padc)))
        cwf = ffn_conv[i]
        cw_g = jnp.pad(cwf[:, :D_FF], ((0, 0), (0, padc)))
        cw_v = jnp.pad(cwf[:, D_FF:], ((0, 0), (0, padc)))
        act = _proj_call(xa, modv(mods, 3, 4), norm_ffn[i], [w_g, w_v], [cw_g, cw_v], epi="ffn", tn=512,
                         tm=tm, ctx_len=ctx_len)
        w_dn = jnp.pad(ffn_w_down[i].astype(BF16), ((0, padc), (0, 0)))
        xa = _oproj_call(act, w_dn, xa, gatev(mods, 5), tm=tm, tn=512, ctx_len=ctx_len)
    return _final_norm_call(xa, final_norm, ctx_len=ctx_len)
```

```python
import functools
import math

import jax
import jax.numpy as jnp
from jax import lax
from jax.experimental import pallas as pl
from jax.experimental.pallas import tpu as pltpu

F32 = jnp.float32
BF16 = jnp.bfloat16

D_MODEL = 2048
DEPTH = 4
CTX_LEN = 256
GRID_W = 64
EPS = 1e-6
DA_HEADS = 8
DA_QK_DIM = 128
DA_V_DIM = 256
DA_QK_W = DA_HEADS * 2 * DA_QK_DIM
DA_V_W = DA_HEADS * DA_V_DIM
ROPE_BASE = 10000.0
GDN_QK_HEADS = 16
GDN_V_HEADS = 32
GDN_DIM = 128
GDN_QK_W = GDN_QK_HEADS * GDN_DIM
GDN_V_W = GDN_V_HEADS * GDN_DIM
GDN_QKV_W = 2 * GDN_QK_W + GDN_V_W
GDN_CONV = 5
CHUNK = 64
D_FF = 5504
FFN_CONV = 3

LANES = 128
HALO = 16
D_FF_PAD = 5632
VMEM_LIMIT = 56 * 1024 * 1024

NT_DIMS = (((1,), (1,)), ((), ()))
TN_DIMS = (((0,), (0,)), ((), ()))


def _silu(x):
    return x * jax.nn.sigmoid(x)


def _params(sem):
    return pltpu.CompilerParams(dimension_semantics=sem, vmem_limit_bytes=VMEM_LIMIT)


def _mods_kernel(c_ref, w_ref, b_ref, o_ref):
    s = _silu(c_ref[...]).astype(BF16)
    w = w_ref[0].astype(BF16)
    o_ref[0] = jnp.dot(s, w, preferred_element_type=F32) + b_ref[0]


def _mods_call(cvec, w_mod, b_mod):
    depth, d, n = w_mod.shape
    tn = 1024
    return pl.pallas_call(
        _mods_kernel,
        out_shape=jax.ShapeDtypeStruct((depth, cvec.shape[0], n), F32),
        grid=(depth, n // tn),
        in_specs=[
            pl.BlockSpec((cvec.shape[0], d), lambda i, j: (0, 0)),
            pl.BlockSpec((1, d, tn), lambda i, j: (i, 0, j)),
            pl.BlockSpec((1, 1, tn), lambda i, j: (i, 0, j)),
        ],
        out_specs=pl.BlockSpec((1, cvec.shape[0], tn), lambda i, j: (i, 0, j)),
        compiler_params=_params(("parallel", "parallel")),
        name="adaln_mods",
    )(cvec, w_mod, b_mod.reshape(depth, 1, n))


def _norm_mod(xv, g, nw, modv, ctx_len):
    y = xv * lax.rsqrt(jnp.mean(xv * xv, axis=-1, keepdims=True) + EPS) * nw
    is_ctx = g < ctx_len
    shift = jnp.where(is_ctx, modv[2:3], modv[0:1])
    scale = jnp.where(is_ctx, modv[3:4], modv[1:2])
    return (y * (1.0 + scale) + shift).astype(BF16)


def _token_conv(u, cw, g, taps, tm, ctx_len, tok):
    p = taps // 2
    acc = None
    for jj in range(taps):
        dlt = jj - p
        sh = u[HALO + dlt:HALO + dlt + tm]
        if dlt != 0:
            gd = g + dlt
            same = ((gd < ctx_len) & (g < ctx_len)) | ((gd >= ctx_len) & (g >= ctx_len))
            ok = (gd >= 0) & (gd < tok) & same
            sh = jnp.where(ok, sh, 0.0)
        term = sh * cw[jj:jj + 1]
        acc = term if acc is None else acc + term
    return acc


def _proj_kernel(*refs, n_w, taps, epi, tm, tn, ctx_len, tok, q_tiles):
    halo = taps > 0
    it = iter(refs)
    x_ref = next(it)
    xp_ref = next(it) if halo else None
    xn_ref = next(it) if halo else None
    modv_ref = next(it)
    nw_ref = next(it)
    w_refs = [next(it) for _ in range(n_w)]
    cw_refs = [next(it) for _ in range(n_w)] if halo else []
    ex_refs = [next(it) for _ in range(2)] if epi == "rope" else []
    out_ref = next(it)
    h_ref = next(it)

    t = pl.program_id(1)
    j = pl.program_id(2)
    off = HALO if halo else 0
    rows = 128

    @pl.when(j == 0)
    def _():
        modv = modv_ref[0]
        nw = nw_ref[...]
        for r0 in range(0, tm, rows):
            g = t * tm + r0 + lax.broadcasted_iota(jnp.int32, (rows, 1), 0)
            h_ref[off + r0:off + r0 + rows] = _norm_mod(x_ref[0, r0:r0 + rows], g, nw, modv, ctx_len)
        if halo:
            gi = lax.broadcasted_iota(jnp.int32, (HALO, 1), 0)
            h_ref[0:HALO] = _norm_mod(xp_ref[0], t * tm - HALO + gi, nw, modv, ctx_len)
            h_ref[off + tm:off + tm + HALO] = _norm_mod(xn_ref[0], t * tm + tm + gi, nw, modv, ctx_len)

    hv = h_ref[...]
    us = [jnp.dot(hv, w[...], preferred_element_type=F32) for w in w_refs]
    g = t * tm + lax.broadcasted_iota(jnp.int32, (tm, 1), 0)
    if halo:
        us = [_token_conv(u, cw[...], g, taps, tm, ctx_len, tok) for u, cw in zip(us, cw_refs)]

    if epi == "ffn":
        out_ref[0] = (_silu(us[0]) * us[1]).astype(out_ref.dtype)
    elif epi == "plain":
        out_ref[0] = us[0].astype(out_ref.dtype)
    elif epi == "gdn_v":
        a = _silu(us[0])
        for hh in range(tn // LANES):
            out_ref[0, hh] = a[:, hh * LANES:(hh + 1) * LANES].astype(out_ref.dtype)
    elif epi == "gdn_qk":
        a = _silu(us[0])
        qs = jnp.where(j < q_tiles, GDN_DIM ** -0.5, 1.0).astype(F32)
        for hh in range(tn // LANES):
            xs = a[:, hh * LANES:(hh + 1) * LANES]
            nrm = xs * lax.rsqrt(jnp.sum(xs * xs, axis=-1, keepdims=True) + EPS)
            out_ref[0, hh] = (nrm * qs).astype(out_ref.dtype)
    elif epi == "rope":
        u = us[0]
        cos = jnp.where(j < 2 * q_tiles, ex_refs[0][...], 1.0)
        sin = jnp.where(j < 2 * q_tiles, ex_refs[1][...], 0.0)
        lane = lax.broadcasted_iota(jnp.int32, (1, tn), 1)
        swapped = jnp.where((lane % 64) < 32, pltpu.roll(u, tn - 32, 1), pltpu.roll(u, 32, 1))
        sc = jnp.where(j < q_tiles, DA_QK_DIM ** -0.5, 1.0).astype(F32)
        out_ref[0] = ((u * cos + swapped * sin) * sc).astype(out_ref.dtype)
    else:
        raise ValueError(epi)


def _proj_call(x, modv, nw, ws, cws, *, epi, tn, tm, out_dtype=BF16, extras=(), q_tiles=0,
               ctx_len=CTX_LEN):
    b, tok, d = x.shape
    n = ws[0].shape[1]
    taps = cws[0].shape[0] if cws else 0
    halo = taps > 0
    nt = tok // tm
    hb = tm // HALO
    nhb = tok // HALO

    in_specs = [pl.BlockSpec((1, tm, d), lambda bi, t, j: (bi, t, 0))]
    args = [x]
    if halo:
        in_specs += [
            pl.BlockSpec((1, HALO, d), lambda bi, t, j: (bi, jnp.maximum(t * hb - 1, 0), 0)),
            pl.BlockSpec((1, HALO, d), lambda bi, t, j: (bi, jnp.minimum((t + 1) * hb, nhb - 1), 0)),
        ]
        args += [x, x]
    in_specs += [pl.BlockSpec((1, 4, d), lambda bi, t, j: (bi, 0, 0)),
                 pl.BlockSpec((1, d), lambda bi, t, j: (0, 0))]
    args += [modv, nw.reshape(1, d)]
    for w in ws:
        in_specs.append(pl.BlockSpec((d, tn), lambda bi, t, j: (0, j)))
        args.append(w)
    for cw in cws:
        in_specs.append(pl.BlockSpec((taps, tn), lambda bi, t, j: (0, j)))
        args.append(cw)
    for e in extras:
        in_specs.append(pl.BlockSpec((tm, tn), lambda bi, t, j: (t, 0)))
        args.append(e)

    if epi in ("gdn_v", "gdn_qk"):
        out_shape = jax.ShapeDtypeStruct((b, n // LANES, tok, LANES), out_dtype)
        out_spec = pl.BlockSpec((1, tn // LANES, tm, LANES), lambda bi, t, j: (bi, j, t, 0))
    else:
        out_shape = jax.ShapeDtypeStruct((b, tok, n), out_dtype)
        out_spec = pl.BlockSpec((1, tm, tn), lambda bi, t, j: (bi, t, j))

    kern = functools.partial(_proj_kernel, n_w=len(ws), taps=taps, epi=epi, tm=tm, tn=tn,
                             ctx_len=ctx_len, tok=tok, q_tiles=q_tiles)
    return pl.pallas_call(
        kern,
        out_shape=out_shape,
        grid=(b, nt, n // tn),
        in_specs=in_specs,
        out_specs=out_spec,
        scratch_shapes=[pltpu.VMEM((tm + (2 * HALO if halo else 0), d), BF16)],
        compiler_params=_params(("parallel", "parallel", "arbitrary")),
        name="norm_proj_" + epi,
    )(*args)


def _row_gate(g_ref, t, tm, ctx_len):
    g = t * tm + lax.broadcasted_iota(jnp.int32, (tm, 1), 0)
    return jnp.where(g < ctx_len, g_ref[0, 1:2], g_ref[0, 0:1])


def _oproj_kernel(y_ref, w_ref, x_ref, g_ref, o_ref, *, tm, ctx_len):
    acc = jnp.dot(y_ref[0], w_ref[...], preferred_element_type=F32)
    o_ref[0] = x_ref[0] + _row_gate(g_ref, pl.program_id(1), tm, ctx_len) * acc


def _oproj_call(y, w, x, gate, *, tm, tn, ctx_len=CTX_LEN):
    b, tok, d = x.shape
    k = y.shape[-1]
    return pl.pallas_call(
        functools.partial(_oproj_kernel, tm=tm, ctx_len=ctx_len),
        out_shape=jax.ShapeDtypeStruct(x.shape, F32),
        grid=(b, tok // tm, d // tn),
        in_specs=[
            pl.BlockSpec((1, tm, k), lambda bi, t, j: (bi, t, 0)),
            pl.BlockSpec((k, tn), lambda bi, t, j: (0, j)),
            pl.BlockSpec((1, tm, tn), lambda bi, t, j: (bi, t, j)),
            pl.BlockSpec((1, 2, tn), lambda bi, t, j: (bi, 0, j)),
        ],
        out_specs=pl.BlockSpec((1, tm, tn), lambda bi, t, j: (bi, t, j)),
        compiler_params=_params(("parallel", "parallel", "arbitrary")),
        name="out_proj_residual",
    )(y, w, x, gate)


def _gdn_oproj_kernel(o_ref, z_ref, gain_ref, w_ref, x_ref, g_ref, out_ref, y_ref, *, tm, ctx_len):
    j = pl.program_id(2)

    @pl.when(j == 0)
    def _():
        gain = gain_ref[...]
        for h in range(GDN_V_HEADS):
            o = o_ref[0, 0, h].astype(F32) + o_ref[1, 0, h].astype(F32)
            y = o * lax.rsqrt(jnp.mean(o * o, axis=-1, keepdims=True) + EPS) * gain
            z = z_ref[0, :, h * LANES:(h + 1) * LANES].astype(F32)
            y_ref[:, h * LANES:(h + 1) * LANES] = (y * _silu(z)).astype(BF16)

    acc = jnp.dot(y_ref[...], w_ref[...], preferred_element_type=F32)
    out_ref[0] = x_ref[0] + _row_gate(g_ref, pl.program_id(1), tm, ctx_len) * acc


def _gdn_oproj_call(o, z, gain, w, x, gate, *, tm, tn, ctx_len=CTX_LEN):
    b, tok, d = x.shape
    k = w.shape[0]
    return pl.pallas_call(
        functools.partial(_gdn_oproj_kernel, tm=tm, ctx_len=ctx_len),
        out_shape=jax.ShapeDtypeStruct(x.shape, F32),
        grid=(b, tok // tm, d // tn),
        in_specs=[
            pl.BlockSpec((2, 1, GDN_V_HEADS, tm, LANES), lambda bi, t, j: (0, bi, 0, t, 0)),
            pl.BlockSpec((1, tm, k), lambda bi, t, j: (bi, t, 0)),
            pl.BlockSpec((1, LANES), lambda bi, t, j: (0, 0)),
            pl.BlockSpec((k, tn), lambda bi, t, j: (0, j)),
            pl.BlockSpec((1, tm, tn), lambda bi, t, j: (bi, t, j)),
            pl.BlockSpec((1, 2, tn), lambda bi, t, j: (bi, 0, j)),
        ],
        out_specs=pl.BlockSpec((1, tm, tn), lambda bi, t, j: (bi, t, j)),
        scratch_shapes=[pltpu.VMEM((tm, k), BF16)],
        compiler_params=_params(("parallel", "parallel", "arbitrary")),
        name="gdn_out_proj_residual",
    )(o, z, gain.reshape(1, LANES), w, x, gate)


def _diff_attn_kernel(q_ref, k_ref, v_ref, lam_ref, gain_ref, o_ref, *, lam_init, ctx_len):
    lv = lam_ref[...]
    lam = (jnp.exp(jnp.sum(lv[0:1] * lv[1:2], axis=-1, keepdims=True))
           - jnp.exp(jnp.sum(lv[2:3] * lv[3:4], axis=-1, keepdims=True)) + lam_init)

    def attend(kk, vv):
        q = q_ref[0]
        ps = []
        for c in range(2):
            s = lax.dot_general(q[:, c * DA_QK_DIM:(c + 1) * DA_QK_DIM],
                                kk[:, c * DA_QK_DIM:(c + 1) * DA_QK_DIM], NT_DIMS,
                                preferred_element_type=F32)
            p = jnp.exp(s - jnp.max(s, axis=-1, keepdims=True))
            ps.append(p * (1.0 / jnp.sum(p, axis=-1, keepdims=True)))
        a = ps[0] - lam * ps[1]
        o = jnp.dot(a.astype(BF16), vv, preferred_element_type=F32)
        o = o * lax.rsqrt(jnp.mean(o * o, axis=-1, keepdims=True) + EPS) * gain_ref[...]
        o_ref[0] = (o * (1.0 - lam_init)).astype(o_ref.dtype)

    t = pl.program_id(2)

    @pl.when(t == 0)
    def _():
        attend(k_ref[0, 0:ctx_len], v_ref[0, 0:ctx_len])

    @pl.when(t != 0)
    def _():
        attend(k_ref[0], v_ref[0])


def _diff_attn_call(qkv, lam_vecs, head_gain, lam_init, *, ctx_len=CTX_LEN):
    b, tok, _ = qkv.shape
    hw = 2 * DA_QK_DIM
    tq = ctx_len
    return pl.pallas_call(
        functools.partial(_diff_attn_kernel, lam_init=lam_init, ctx_len=ctx_len),
        out_shape=jax.ShapeDtypeStruct((b, tok, DA_V_W), BF16),
        grid=(b, DA_HEADS, tok // tq),
        in_specs=[
            pl.BlockSpec((1, tq, hw), lambda bi, h, t: (bi, t, h)),
            pl.BlockSpec((1, tok, hw), lambda bi, h, t: (bi, 0, DA_HEADS + h)),
            pl.BlockSpec((1, tok, DA_V_DIM), lambda bi, h, t: (bi, 0, 2 * DA_HEADS + h)),
            pl.BlockSpec((4, DA_QK_DIM), lambda bi, h, t: (0, 0)),
            pl.BlockSpec((1, DA_V_DIM), lambda bi, h, t: (0, 0)),
        ],
        out_specs=pl.BlockSpec((1, tq, DA_V_DIM), lambda bi, h, t: (bi, t, h)),
        compiler_params=_params(("parallel", "parallel", "arbitrary")),
        name="diff_attention",
    )(qkv, qkv, qkv, lam_vecs, head_gain.reshape(1, DA_V_DIM))


def _split3(x):
    x1 = x.astype(BF16)
    r1 = x - x1.astype(F32)
    x2 = r1.astype(BF16)
    x3 = (r1 - x2.astype(F32)).astype(BF16)
    return x1, x2, x3


def _blockdiag(x2):
    first = lax.broadcasted_iota(jnp.int32, x2.shape, 1) < CHUNK
    return jnp.concatenate([jnp.where(first, x2, 0.0), jnp.where(first, 0.0, x2)], axis=0)


def _unit_lower_inverses(a2s, base):
    c = CHUNK
    nt = c // 8
    lane = lax.broadcasted_iota(jnp.int32, (8, LANES), 1)
    sub = lax.broadcasted_iota(jnp.int32, (8, LANES), 0)
    col = lane % c
    first = lane < c
    row2 = lax.broadcasted_iota(jnp.int32, (c, LANES), 0)
    col2 = lax.broadcasted_iota(jnp.int32, (c, LANES), 1) % c

    diag = (row2 // base) == (col2 // base)
    a_ts = [[jnp.where(diag, a2, 0.0)[8 * r:8 * r + 8] for r in range(nt)] for a2 in a2s]
    t_ts = [[jnp.where(col == 8 * r + sub, 1.0, 0.0).astype(F32) for r in range(nt)] for _ in a2s]
    for jcol in range(base - 1):
        for a_t, t_t in zip(a_ts, t_ts):
            for blk in range(c // base):
                rj = blk * base + jcol
                trow = t_t[rj // 8][rj % 8:rj % 8 + 1, :]
                for r in range(blk * base // 8, (blk + 1) * base // 8):
                    if 8 * r + 7 <= rj:
                        continue
                    acol = jnp.where(first, a_t[r][:, rj:rj + 1], a_t[r][:, c + rj:c + rj + 1])
                    t_t[r] = t_t[r] - acol * trow
    tbds = [_blockdiag(jnp.concatenate(t_t, axis=0)) for t_t in t_ts]

    size = base
    while size < c:
        off = ((row2 // (2 * size)) == (col2 // (2 * size))) & ((row2 // size) != (col2 // size))
        tbs = [tbd.astype(BF16) for tbd in tbds]
        tqs = [jnp.dot(tb, _blockdiag(jnp.where(off, a2, 0.0)).astype(BF16), preferred_element_type=F32)
               for tb, a2 in zip(tbs, a2s)]
        tbds = [tbd - jnp.dot(tq.astype(BF16), tb, preferred_element_type=F32)
                for tbd, tq, tb in zip(tbds, tqs, tbs)]
        size *= 2
    return tbds


def _gdn_scan_kernel(q_ref, k_ref, v_ref, ab_ref, alog_ref, dt_ref, o_ref,
                     s_ref, cg_ref, cb_ref, rg_ref, rb_ref, gt_ref, *, base):
    c = CHUNK
    d = pl.program_id(1)
    step = pl.program_id(2)
    rev = d == 1

    @pl.when(step == 0)
    def _():
        s_ref[...] = jnp.zeros(s_ref.shape, F32)

    ab = ab_ref[0]
    beta_all = jax.nn.sigmoid(ab)
    xs = ab + dt_ref[0]
    g_all = -jnp.exp(alog_ref[0]) * (jnp.maximum(xs, 0.0) + jnp.log1p(jnp.exp(-jnp.abs(xs))))
    ri = lax.broadcasted_iota(jnp.int32, (c, c), 0)
    ci = lax.broadcasted_iota(jnp.int32, (c, c), 1)
    sgn = 1 - 2 * d
    cum = jnp.where((ci - ri) * sgn <= 0, 1.0, 0.0).astype(BF16)
    gam_all = sum(jnp.dot(cum, part, preferred_element_type=F32) for part in _split3(g_all))
    gtot_all = jnp.sum(g_all, axis=0, keepdims=True)

    pi = lax.broadcasted_iota(jnp.int32, (GDN_QK_HEADS, LANES), 0)
    li = lax.broadcasted_iota(jnp.int32, (GDN_QK_HEADS, LANES), 1)
    zpad = jnp.zeros((c, LANES), BF16)

    def pair_rows(x, col0):
        sel_e = jnp.where(li == col0 + 2 * pi, 1.0, 0.0).astype(BF16)
        sel_o = jnp.where(li == col0 + 2 * pi + 1, 1.0, 0.0).astype(BF16)
        acc = jnp.zeros((GDN_QK_HEADS, LANES), F32)
        for part in _split3(x):
            acc = acc + lax.dot_general(sel_e, jnp.concatenate([part, zpad], axis=0), NT_DIMS,
                                        preferred_element_type=F32)
            acc = acc + lax.dot_general(sel_o, jnp.concatenate([zpad, part], axis=0), NT_DIMS,
                                        preferred_element_type=F32)
        return acc

    rg_ref[...] = pair_rows(gam_all, GDN_V_HEADS)
    rb_ref[...] = pair_rows(beta_all, 0)
    for h in range(GDN_V_HEADS):
        cg_ref[h] = jnp.broadcast_to(gam_all[:, GDN_V_HEADS + h:GDN_V_HEADS + h + 1], (c, LANES))
        cb_ref[h] = jnp.broadcast_to(beta_all[:, h:h + 1], (c, LANES))
        gt_ref[h:h + 1, :] = jnp.broadcast_to(gtot_all[:, GDN_V_HEADS + h:GDN_V_HEADS + h + 1], (1, LANES))

    row2 = lax.broadcasted_iota(jnp.int32, (c, LANES), 0)
    lane2 = lax.broadcasted_iota(jnp.int32, (c, LANES), 1)
    col2 = lane2 % c
    first2 = lane2 < c

    pairs = range(GDN_QK_HEADS)
    heads = range(GDN_V_HEADS)

    qkks = []
    for p in pairs:
        q = q_ref[0, p]
        k = k_ref[0, p]
        qkks.append(lax.dot_general(jnp.concatenate([q, k], axis=0), jnp.concatenate([k, k], axis=0),
                                    NT_DIMS, preferred_element_type=F32))
    a2s, qkm2s = [], []
    for p in pairs:
        gc2 = jnp.where(first2, cg_ref[2 * p], cg_ref[2 * p + 1])
        bc2 = jnp.where(first2, cb_ref[2 * p], cb_ref[2 * p + 1])
        gr2 = rg_ref[p:p + 1, :]
        br2 = rb_ref[p:p + 1, :]
        e2 = jnp.exp(-jnp.abs(gc2 - gr2))
        a2s.append(jnp.where(row2 > col2, jnp.where(rev, br2, bc2) * qkks[p][c:2 * c] * e2, 0.0))
        qkm2s.append(jnp.where((row2 - col2) * sgn >= 0, qkks[p][0:c] * e2, 0.0))

    tbds = _unit_lower_inverses(a2s, base)

    uws = []
    for p in pairs:
        kf = k_ref[0, p].astype(F32)
        rhs = []
        for h in (2 * p, 2 * p + 1):
            cb = cb_ref[h]
            vb = v_ref[0, h].astype(F32) * cb
            kb = kf * (cb * jnp.exp(cg_ref[h]))
            rhs.append(jnp.concatenate([vb, kb], axis=1).astype(BF16))
        tuse = jnp.where(rev, tbds[p].T, tbds[p]).astype(BF16)
        uws.append(jnp.dot(tuse, jnp.concatenate(rhs, axis=0), preferred_element_type=F32))

    ws_qs = []
    for h in heads:
        w = uws[h // 2][(h % 2) * c:(h % 2 + 1) * c, GDN_DIM:2 * GDN_DIM]
        qd = q_ref[0, h // 2].astype(F32) * jnp.exp(cg_ref[h])
        wq = jnp.concatenate([w.astype(BF16), qd.astype(BF16)], axis=0)
        ws_qs.append(jnp.dot(wq, s_ref[h].astype(BF16), preferred_element_type=F32))

    for h in heads:
        u = uws[h // 2][(h % 2) * c:(h % 2 + 1) * c, 0:GDN_DIM]
        v_new = (u - ws_qs[h][0:c]).astype(BF16)
        qkm = qkm2s[h // 2][:, (h % 2) * c:(h % 2 + 1) * c].astype(BF16)
        o = ws_qs[h][c:2 * c] + jnp.dot(qkm, v_new, preferred_element_type=F32)
        o_ref[0, 0, h] = o.astype(o_ref.dtype)
        gt = gt_ref[h:h + 1, :]
        k_dec = (k_ref[0, h // 2].astype(F32) * jnp.exp(gt - cg_ref[h])).astype(BF16)
        s_ref[h] = s_ref[h] * jnp.exp(gt) + lax.dot_general(k_dec, v_new, TN_DIMS,
                                                           preferred_element_type=F32)


def _gdn_scan_call(qk, v, ab, alog, dt, *, ctx_len=CTX_LEN, base=16):
    b, _, tok, _ = qk.shape
    nc = tok // CHUNK
    ncc = ctx_len // CHUNK

    def chunk(dd, s):
        back = jnp.where(s < ncc, ncc - 1 - s, nc - 1 - (s - ncc))
        return jnp.where(dd == 0, s, back)

    return pl.pallas_call(
        functools.partial(_gdn_scan_kernel, base=base),
        out_shape=jax.ShapeDtypeStruct((2, b, GDN_V_HEADS, tok, LANES), BF16),
        grid=(b, 2, nc),
        in_specs=[
            pl.BlockSpec((1, GDN_QK_HEADS, CHUNK, LANES), lambda bi, dd, s: (bi, 0, chunk(dd, s), 0)),
            pl.BlockSpec((1, GDN_QK_HEADS, CHUNK, LANES), lambda bi, dd, s: (bi, 1, chunk(dd, s), 0)),
            pl.BlockSpec((1, GDN_V_HEADS, CHUNK, LANES), lambda bi, dd, s: (bi, 0, chunk(dd, s), 0)),
            pl.BlockSpec((1, CHUNK, LANES), lambda bi, dd, s: (bi, chunk(dd, s), dd)),
            pl.BlockSpec((1, 1, LANES), lambda bi, dd, s: (dd, 0, 0)),
            pl.BlockSpec((1, 1, LANES), lambda bi, dd, s: (dd, 0, 0)),
        ],
        out_specs=pl.BlockSpec((1, 1, GDN_V_HEADS, CHUNK, LANES),
                               lambda bi, dd, s: (dd, bi, 0, chunk(dd, s), 0)),
        scratch_shapes=[
            pltpu.VMEM((GDN_V_HEADS, GDN_DIM, GDN_DIM), F32),
            pltpu.VMEM((GDN_V_HEADS, CHUNK, LANES), F32),
            pltpu.VMEM((GDN_V_HEADS, CHUNK, LANES), F32),
            pltpu.VMEM((GDN_QK_HEADS, LANES), F32),
            pltpu.VMEM((GDN_QK_HEADS, LANES), F32),
            pltpu.VMEM((GDN_V_HEADS, LANES), F32),
        ],
        compiler_params=_params(("parallel", "parallel", "arbitrary")),
        name="gdn_chunk_scan",
    )(qk, qk, v, ab, alog, dt)


def _final_norm_kernel(x_ref, w_ref, o_ref):
    xv = x_ref[0]
    o_ref[0] = xv * lax.rsqrt(jnp.mean(xv * xv, axis=-1, keepdims=True) + EPS) * w_ref[...]


def _final_norm_call(x, w, *, ctx_len=CTX_LEN):
    b, tok, d = x.shape
    tr = ctx_len
    return pl.pallas_call(
        _final_norm_kernel,
        out_shape=jax.ShapeDtypeStruct((b, tok - ctx_len, d), F32),
        grid=(b, (tok - ctx_len) // tr),
        in_specs=[pl.BlockSpec((1, tr, d), lambda bi, t: (bi, t + 1, 0)),
                  pl.BlockSpec((1, d), lambda bi, t: (0, 0))],
        out_specs=pl.BlockSpec((1, tr, d), lambda bi, t: (bi, t, 0)),
        compiler_params=_params(("parallel", "parallel")),
        name="final_rmsnorm",
    )(x, w.reshape(1, d))


def _rope_tables(seq, ctx_len, width):
    t = jnp.arange(seq, dtype=jnp.int32)
    rows = (t // GRID_W).astype(F32)
    cols = (t % GRID_W).astype(F32)
    n_freq = DA_QK_DIM // 4
    inv_freq = ROPE_BASE ** (-jnp.arange(n_freq, dtype=F32) / n_freq)
    ar = rows[:, None] * inv_freq
    ac = cols[:, None] * inv_freq
    cos = jnp.concatenate([jnp.cos(ar), jnp.cos(ar), jnp.cos(ac), jnp.cos(ac)], axis=1)
    sin = jnp.concatenate([-jnp.sin(ar), jnp.sin(ar), -jnp.sin(ac), jnp.sin(ac)], axis=1)
    cos = jnp.concatenate([jnp.ones((ctx_len, DA_QK_DIM), F32), cos], axis=0)
    sin = jnp.concatenate([jnp.zeros((ctx_len, DA_QK_DIM), F32), sin], axis=0)
    reps = width // DA_QK_DIM
    return jnp.tile(cos, (1, reps)), jnp.tile(sin, (1, reps))


def kernel(x, c, ctx, c_ctx, w_mod, b_mod, norm_mix, norm_ffn, da_w_qkv, da_lambda, da_head_gain, da_w_o,
           gdn_w_in, gdn_conv, gdn_a_log, gdn_dt_bias, gdn_norm_gain, gdn_w_o, ffn_w_up, ffn_conv,
           ffn_w_down, final_norm):
    bsz, seq, d = x.shape
    ctx_len = ctx.shape[1]
    tm = (ctx_len + seq) // 3

    xa = jnp.concatenate([ctx, x], axis=1)
    pad_rows = 16 - (bsz + 1)
    cvec = jnp.concatenate([c, c_ctx[None, :], jnp.zeros((pad_rows, d), F32)], axis=0)
    mods_all = _mods_call(cvec, w_mod, b_mod)
    cos_t, sin_t = _rope_tables(seq, ctx_len, 512)

    def sel(mods, idx):
        lat = mods[:bsz, idx * d:(idx + 1) * d]
        cx = jnp.broadcast_to(mods[bsz:bsz + 1, idx * d:(idx + 1) * d], (bsz, d))
        return lat, cx

    def modv(mods, i_shift, i_scale):
        sl, sc = sel(mods, i_shift)
        cl, cc = sel(mods, i_scale)
        return jnp.stack([sl, cl, sc, cc], axis=1)

    def gatev(mods, idx):
        gl, gc = sel(mods, idx)
        return jnp.stack([gl, gc], axis=1)

    for i in range(DEPTH):
        mods = mods_all[i]
        jm = i // 2
        mv = modv(mods, 0, 1)
        if i % 2 == 0:
            lam_init = 0.8 - 0.6 * math.exp(-0.3 * i)
            qkv = _proj_call(xa, mv, norm_mix[i], [da_w_qkv[jm].astype(BF16)], [], epi="rope", tn=512,
                             tm=tm, extras=(cos_t, sin_t), q_tiles=DA_QK_W // 512, ctx_len=ctx_len)
            att = _diff_attn_call(qkv, da_lambda[jm], da_head_gain[jm], lam_init, ctx_len=ctx_len)
            xa = _oproj_call(att, da_w_o[jm].astype(BF16), xa, gatev(mods, 2), tm=tm, tn=512,
                             ctx_len=ctx_len)
        else:
            w_in = gdn_w_in[jm].astype(BF16)
            cw = gdn_conv[jm]
            qk = _proj_call(xa, mv, norm_mix[i], [w_in[:, :2 * GDN_QK_W]], [cw[:, :2 * GDN_QK_W]],
                            epi="gdn_qk", tn=512, tm=tm, q_tiles=GDN_QK_W // 512, ctx_len=ctx_len)
            vv = _proj_call(xa, mv, norm_mix[i], [w_in[:, 2 * GDN_QK_W:GDN_QKV_W]],
                            [cw[:, 2 * GDN_QK_W:GDN_QKV_W]], epi="gdn_v", tn=512, tm=tm, ctx_len=ctx_len)
            z = _proj_call(xa, mv, norm_mix[i], [w_in[:, GDN_QKV_W:GDN_QKV_W + GDN_V_W]], [],
                           epi="plain", tn=512, tm=tm, ctx_len=ctx_len)
            w_ab = w_in[:, GDN_QKV_W + GDN_V_W:].reshape(d, 2, 2 * GDN_V_HEADS)
            w_ab = jnp.concatenate([w_ab, jnp.zeros_like(w_ab)], axis=2).reshape(d, 2 * LANES)
            ab = _proj_call(xa, mv, norm_mix[i], [w_ab], [], epi="plain", tn=2 * LANES, tm=tm,
                            out_dtype=F32, ctx_len=ctx_len)
            zeros = jnp.zeros((2, GDN_V_HEADS), F32)
            alog = jnp.concatenate([zeros, gdn_a_log[jm], zeros, zeros], axis=1).reshape(2, 1, LANES)
            dtb = jnp.concatenate([zeros, gdn_dt_bias[jm], zeros, zeros], axis=1).reshape(2, 1, LANES)
            o = _gdn_scan_call(qk, vv, ab, alog, dtb, ctx_len=ctx_len)
            xa = _gdn_oproj_call(o, z, gdn_norm_gain[jm], gdn_w_o[jm].astype(BF16), xa, gatev(mods, 2),
                                 tm=tm // 2, tn=512, ctx_len=ctx_len)
        padc = D_FF_PAD - D_FF
        w_up = ffn_w_up[i].astype(BF16)
        w_g = jnp.pad(w_up[:, :D_FF], ((0, 0), (0, padc)))
        w_v = jnp.pad(w_up[:, D_FF:], ((0, 0), (0, padc)))
        cwf = ffn_conv[i]
        cw_g = jnp.pad(cwf[:, :D_FF], ((0, 0), (0, padc)))
        cw_v = jnp.pad(cwf[:, D_FF:], ((0, 0), (0, padc)))
        act = _proj_call(xa, modv(mods, 3, 4), norm_ffn[i], [w_g, w_v], [cw_g, cw_v], epi="ffn", tn=512,
                         tm=tm, ctx_len=ctx_len)
        w_dn = jnp.pad(ffn_w_down[i].astype(BF16), ((0, padc), (0, 0)))
        xa = _oproj_call(act, w_dn, xa, gatev(mods, 5), tm=tm, tn=512, ctx_len=ctx_len)
    return _final_norm_call(xa, final_norm, ctx_len=ctx_len)
```

```python
import functools
import math

import jax
import jax.numpy as jnp
from jax import lax
from jax.experimental import pallas as pl
from jax.experimental.pallas import tpu as pltpu

F32 = jnp.float32
BF16 = jnp.bfloat16

D_MODEL = 2048
DEPTH = 4
CTX_LEN = 256
GRID_W = 64
EPS = 1e-6
DA_HEADS = 8
DA_QK_DIM = 128
DA_V_DIM = 256
DA_QK_W = DA_HEADS * 2 * DA_QK_DIM
DA_V_W = DA_HEADS * DA_V_DIM
ROPE_BASE = 10000.0
GDN_QK_HEADS = 16
GDN_V_HEADS = 32
GDN_DIM = 128
GDN_QK_W = GDN_QK_HEADS * GDN_DIM
GDN_V_W = GDN_V_HEADS * GDN_DIM
GDN_QKV_W = 2 * GDN_QK_W + GDN_V_W
GDN_CONV = 5
CHUNK = 64
D_FF = 5504
FFN_CONV = 3

LANES = 128
HALO = 16
D_FF_PAD = 5632
VMEM_LIMIT = 56 * 1024 * 1024

NT_DIMS = (((1,), (1,)), ((), ()))
TN_DIMS = (((0,), (0,)), ((), ()))


def _silu(x):
    return x * jax.nn.sigmoid(x)


def _params(sem):
    return pltpu.CompilerParams(dimension_semantics=sem, vmem_limit_bytes=VMEM_LIMIT)


def _mods_kernel(c_ref, w_ref, b_ref, o_ref):
    s = _silu(c_ref[...]).astype(BF16)
    w = w_ref[0].astype(BF16)
    o_ref[0] = jnp.dot(s, w, preferred_element_type=F32) + b_ref[0]


def _mods_call(cvec, w_mod, b_mod):
    depth, d, n = w_mod.shape
    tn = 1024
    return pl.pallas_call(
        _mods_kernel,
        out_shape=jax.ShapeDtypeStruct((depth, cvec.shape[0], n), F32),
        grid=(depth, n // tn),
        in_specs=[
            pl.BlockSpec((cvec.shape[0], d), lambda i, j: (0, 0)),
            pl.BlockSpec((1, d, tn), lambda i, j: (i, 0, j)),
            pl.BlockSpec((1, 1, tn), lambda i, j: (i, 0, j)),
        ],
        out_specs=pl.BlockSpec((1, cvec.shape[0], tn), lambda i, j: (i, 0, j)),
        compiler_params=_params(("parallel", "parallel")),
        name="adaln_mods",
    )(cvec, w_mod, b_mod.reshape(depth, 1, n))


def _norm_mod(xv, g, nw, modv, ctx_len):
    y = xv * lax.rsqrt(jnp.mean(xv * xv, axis=-1, keepdims=True) + EPS) * nw
    is_ctx = g < ctx_len
    shift = jnp.where(is_ctx, modv[2:3], modv[0:1])
    scale = jnp.where(is_ctx, modv[3:4], modv[1:2])
    return (y * (1.0 + scale) + shift).astype(BF16)


def _token_conv(u, cw, g, taps, tm, ctx_len, tok):
    p = taps // 2
    acc = None
    for jj in range(taps):
        dlt = jj - p
        sh = u[HALO + dlt:HALO + dlt + tm]
        if dlt != 0:
            gd = g + dlt
            same = ((gd < ctx_len) & (g < ctx_len)) | ((gd >= ctx_len) & (g >= ctx_len))
            ok = (gd >= 0) & (gd < tok) & same
            sh = jnp.where(ok, sh, 0.0)
        term = sh * cw[jj:jj + 1]
        acc = term if acc is None else acc + term
    return acc


def _proj_kernel(*refs, n_w, taps, epi, tm, tn, ctx_len, tok, q_tiles):
    halo = taps > 0
    it = iter(refs)
    x_ref = next(it)
    xp_ref = next(it) if halo else None
    xn_ref = next(it) if halo else None
    modv_ref = next(it)
    nw_ref = next(it)
    w_refs = [next(it) for _ in range(n_w)]
    cw_refs = [next(it) for _ in range(n_w)] if halo else []
    ex_refs = [next(it) for _ in range(2)] if epi == "rope" else []
    out_ref = next(it)
    h_ref = next(it)

    t = pl.program_id(1)
    j = pl.program_id(2)
    off = HALO if halo else 0
    rows = 128

    @pl.when(j == 0)
    def _():
        modv = modv_ref[0]
        nw = nw_ref[...]
        for r0 in range(0, tm, rows):
            g = t * tm + r0 + lax.broadcasted_iota(jnp.int32, (rows, 1), 0)
            h_ref[off + r0:off + r0 + rows] = _norm_mod(x_ref[0, r0:r0 + rows], g, nw, modv, ctx_len)
        if halo:
            gi = lax.broadcasted_iota(jnp.int32, (HALO, 1), 0)
            h_ref[0:HALO] = _norm_mod(xp_ref[0], t * tm - HALO + gi, nw, modv, ctx_len)
            h_ref[off + tm:off + tm + HALO] = _norm_mod(xn_ref[0], t * tm + tm + gi, nw, modv, ctx_len)

    hv = h_ref[...]
    us = [jnp.dot(hv, w[...], preferred_element_type=F32) for w in w_refs]
    g = t * tm + lax.broadcasted_iota(jnp.int32, (tm, 1), 0)
    if halo:
        us = [_token_conv(u, cw[...], g, taps, tm, ctx_len, tok) for u, cw in zip(us, cw_refs)]

    if epi == "ffn":
        out_ref[0] = (_silu(us[0]) * us[1]).astype(out_ref.dtype)
    elif epi == "plain":
        out_ref[0] = us[0].astype(out_ref.dtype)
    elif epi == "gdn_v":
        a = _silu(us[0])
        for hh in range(tn // LANES):
            out_ref[0, hh] = a[:, hh * LANES:(hh + 1) * LANES].astype(out_ref.dtype)
    elif epi == "gdn_qk":
        a = _silu(us[0])
        qs = jnp.where(j < q_tiles, GDN_DIM ** -0.5, 1.0).astype(F32)
        for hh in range(tn // LANES):
            xs = a[:, hh * LANES:(hh + 1) * LANES]
            nrm = xs * lax.rsqrt(jnp.sum(xs * xs, axis=-1, keepdims=True) + EPS)
            out_ref[0, hh] = (nrm * qs).astype(out_ref.dtype)
    elif epi == "rope":
        u = us[0]
        cos = jnp.where(j < 2 * q_tiles, ex_refs[0][...], 1.0)
        sin = jnp.where(j < 2 * q_tiles, ex_refs[1][...], 0.0)
        lane = lax.broadcasted_iota(jnp.int32, (1, tn), 1)
        swapped = jnp.where((lane % 64) < 32, pltpu.roll(u, tn - 32, 1), pltpu.roll(u, 32, 1))
        sc = jnp.where(j < q_tiles, DA_QK_DIM ** -0.5, 1.0).astype(F32)
        out_ref[0] = ((u * cos + swapped * sin) * sc).astype(out_ref.dtype)
    else:
        raise ValueError(epi)


def _proj_call(x, modv, nw, ws, cws, *, epi, tn, tm, out_dtype=BF16, extras=(), q_tiles=0,
               ctx_len=CTX_LEN):
    b, tok, d = x.shape
    n = ws[0].shape[1]
    taps = cws[0].shape[0] if cws else 0
    halo = taps > 0
    nt = tok // tm
    hb = tm // HALO
    nhb = tok // HALO

    in_specs = [pl.BlockSpec((1, tm, d), lambda bi, t, j: (bi, t, 0))]
    args = [x]
    if halo:
        in_specs += [
            pl.BlockSpec((1, HALO, d), lambda bi, t, j: (bi, jnp.maximum(t * hb - 1, 0), 0)),
            pl.BlockSpec((1, HALO, d), lambda bi, t, j: (bi, jnp.minimum((t + 1) * hb, nhb - 1), 0)),
        ]
        args += [x, x]
    in_specs += [pl.BlockSpec((1, 4, d), lambda bi, t, j: (bi, 0, 0)),
                 pl.BlockSpec((1, d), lambda bi, t, j: (0, 0))]
    args += [modv, nw.reshape(1, d)]
    for w in ws:
        in_specs.append(pl.BlockSpec((d, tn), lambda bi, t, j: (0, j)))
        args.append(w)
    for cw in cws:
        in_specs.append(pl.BlockSpec((taps, tn), lambda bi, t, j: (0, j)))
        args.append(cw)
    for e in extras:
        in_specs.append(pl.BlockSpec((tm, tn), lambda bi, t, j: (t, 0)))
        args.append(e)

    if epi in ("gdn_v", "gdn_qk"):
        out_shape = jax.ShapeDtypeStruct((b, n // LANES, tok, LANES), out_dtype)
        out_spec = pl.BlockSpec((1, tn // LANES, tm, LANES), lambda bi, t, j: (bi, j, t, 0))
    else:
        out_shape = jax.ShapeDtypeStruct((b, tok, n), out_dtype)
        out_spec = pl.BlockSpec((1, tm, tn), lambda bi, t, j: (bi, t, j))

    kern = functools.partial(_proj_kernel, n_w=len(ws), taps=taps, epi=epi, tm=tm, tn=tn,
                             ctx_len=ctx_len, tok=tok, q_tiles=q_tiles)
    return pl.pallas_call(
        kern,
        out_shape=out_shape,
        grid=(b, nt, n // tn),
        in_specs=in_specs,
        out_specs=out_spec,
        scratch_shapes=[pltpu.VMEM((tm + (2 * HALO if halo else 0), d), BF16)],
        compiler_params=_params(("parallel", "parallel", "arbitrary")),
        name="norm_proj_" + epi,
    )(*args)


def _row_gate(g_ref, t, tm, ctx_len):
    g = t * tm + lax.broadcasted_iota(jnp.int32, (tm, 1), 0)
    return jnp.where(g < ctx_len, g_ref[0, 1:2], g_ref[0, 0:1])


def _oproj_kernel(y_ref, w_ref, x_ref, g_ref, o_ref, *, tm, ctx_len):
    acc = jnp.dot(y_ref[0], w_ref[...], preferred_element_type=F32)
    o_ref[0] = x_ref[0] + _row_gate(g_ref, pl.program_id(1), tm, ctx_len) * acc


def _oproj_call(y, w, x, gate, *, tm, tn, ctx_len=CTX_LEN):
    b, tok, d = x.shape
    k = y.shape[-1]
    return pl.pallas_call(
        functools.partial(_oproj_kernel, tm=tm, ctx_len=ctx_len),
        out_shape=jax.ShapeDtypeStruct(x.shape, F32),
        grid=(b, tok // tm, d // tn),
        in_specs=[
            pl.BlockSpec((1, tm, k), lambda bi, t, j: (bi, t, 0)),
            pl.BlockSpec((k, tn), lambda bi, t, j: (0, j)),
            pl.BlockSpec((1, tm, tn), lambda bi, t, j: (bi, t, j)),
            pl.BlockSpec((1, 2, tn), lambda bi, t, j: (bi, 0, j)),
        ],
        out_specs=pl.BlockSpec((1, tm, tn), lambda bi, t, j: (bi, t, j)),
        compiler_params=_params(("parallel", "parallel", "arbitrary")),
        name="out_proj_residual",
    )(y, w, x, gate)


def _gdn_oproj_kernel(o_ref, z_ref, gain_ref, w_ref, x_ref, g_ref, out_ref, y_ref, *, tm, ctx_len):
    j = pl.program_id(2)

    @pl.when(j == 0)
    def _():
        gain = gain_ref[...]
        for h in range(GDN_V_HEADS):
            o = o_ref[0, 0, h].astype(F32) + o_ref[1, 0, h].astype(F32)
            y = o * lax.rsqrt(jnp.mean(o * o, axis=-1, keepdims=True) + EPS) * gain
            z = z_ref[0, :, h * LANES:(h + 1) * LANES].astype(F32)
            y_ref[:, h * LANES:(h + 1) * LANES] = (y * _silu(z)).astype(BF16)

    acc = jnp.dot(y_ref[...], w_ref[...], preferred_element_type=F32)
    out_ref[0] = x_ref[0] + _row_gate(g_ref, pl.program_id(1), tm, ctx_len) * acc


def _gdn_oproj_call(o, z, gain, w, x, gate, *, tm, tn, ctx_len=CTX_LEN):
    b, tok, d = x.shape
    k = w.shape[0]
    return pl.pallas_call(
        functools.partial(_gdn_oproj_kernel, tm=tm, ctx_len=ctx_len),
        out_shape=jax.ShapeDtypeStruct(x.shape, F32),
        grid=(b, tok // tm, d // tn),
        in_specs=[
            pl.BlockSpec((2, 1, GDN_V_HEADS, tm, LANES), lambda bi, t, j: (0, bi, 0, t, 0)),
            pl.BlockSpec((1, tm, k), lambda bi, t, j: (bi, t, 0)),
            pl.BlockSpec((1, LANES), lambda bi, t, j: (0, 0)),
            pl.BlockSpec((k, tn), lambda bi, t, j: (0, j)),
            pl.BlockSpec((1, tm, tn), lambda bi, t, j: (bi, t, j)),
            pl.BlockSpec((1, 2, tn), lambda bi, t, j: (bi, 0, j)),
        ],
        out_specs=pl.BlockSpec((1, tm, tn), lambda bi, t, j: (bi, t, j)),
        scratch_shapes=[pltpu.VMEM((tm, k), BF16)],
        compiler_params=_params(("parallel", "parallel", "arbitrary")),
        name="gdn_out_proj_residual",
    )(o, z, gain.reshape(1, LANES), w, x, gate)


def _diff_attn_kernel(q_ref, k_ref, v_ref, lam_ref, gain_ref, o_ref, *, lam_init, ctx_len):
    lv = lam_ref[...]
    lam = (jnp.exp(jnp.sum(lv[0:1] * lv[1:2], axis=-1, keepdims=True))
           - jnp.exp(jnp.sum(lv[2:3] * lv[3:4], axis=-1, keepdims=True)) + lam_init)

    def attend(kk, vv):
        q = q_ref[0]
        ps = []
        for c in range(2):
            s = lax.dot_general(q[:, c * DA_QK_DIM:(c + 1) * DA_QK_DIM],
                                kk[:, c * DA_QK_DIM:(c + 1) * DA_QK_DIM], NT_DIMS,
                                preferred_element_type=F32)
            p = jnp.exp(s - jnp.max(s, axis=-1, keepdims=True))
            ps.append(p * (1.0 / jnp.sum(p, axis=-1, keepdims=True)))
        a = ps[0] - lam * ps[1]
        o = jnp.dot(a.astype(BF16), vv, preferred_element_type=F32)
        o = o * lax.rsqrt(jnp.mean(o * o, axis=-1, keepdims=True) + EPS) * gain_ref[...]
        o_ref[0] = (o * (1.0 - lam_init)).astype(o_ref.dtype)

    t = pl.program_id(2)

    @pl.when(t == 0)
    def _():
        attend(k_ref[0, 0:ctx_len], v_ref[0, 0:ctx_len])

    @pl.when(t != 0)
    def _():
        attend(k_ref[0], v_ref[0])


def _diff_attn_call(qkv, lam_vecs, head_gain, lam_init, *, ctx_len=CTX_LEN):
    b, tok, _ = qkv.shape
    hw = 2 * DA_QK_DIM
    tq = ctx_len
    return pl.pallas_call(
        functools.partial(_diff_attn_kernel, lam_init=lam_init, ctx_len=ctx_len),
        out_shape=jax.ShapeDtypeStruct((b, tok, DA_V_W), BF16),
        grid=(b, DA_HEADS, tok // tq),
        in_specs=[
            pl.BlockSpec((1, tq, hw), lambda bi, h, t: (bi, t, h)),
            pl.BlockSpec((1, tok, hw), lambda bi, h, t: (bi, 0, DA_HEADS + h)),
            pl.BlockSpec((1, tok, DA_V_DIM), lambda bi, h, t: (bi, 0, 2 * DA_HEADS + h)),
            pl.BlockSpec((4, DA_QK_DIM), lambda bi, h, t: (0, 0)),
            pl.BlockSpec((1, DA_V_DIM), lambda bi, h, t: (0, 0)),
        ],
        out_specs=pl.BlockSpec((1, tq, DA_V_DIM), lambda bi, h, t: (bi, t, h)),
        compiler_params=_params(("parallel", "parallel", "arbitrary")),
        name="diff_attention",
    )(qkv, qkv, qkv, lam_vecs, head_gain.reshape(1, DA_V_DIM))


def _split3(x):
    x1 = x.astype(BF16)
    r1 = x - x1.astype(F32)
    x2 = r1.astype(BF16)
    x3 = (r1 - x2.astype(F32)).astype(BF16)
    return x1, x2, x3


def _blockdiag(x2):
    first = lax.broadcasted_iota(jnp.int32, x2.shape, 1) < CHUNK
    return jnp.concatenate([jnp.where(first, x2, 0.0), jnp.where(first, 0.0, x2)], axis=0)


def _unit_lower_inverses(a2s, base):
    c = CHUNK
    nt = c // base
    assert base == 8
    lane = lax.broadcasted_iota(jnp.int32, (base, LANES), 1)
    sub = lax.broadcasted_iota(jnp.int32, (base, LANES), 0)
    row2 = lax.broadcasted_iota(jnp.int32, (c, LANES), 0)
    col2 = lax.broadcasted_iota(jnp.int32, (c, LANES), 1) % c

    in_blk = [((lane % c) // base) == r for r in range(nt)]
    dgs = []
    for a2 in a2s:
        dg = jnp.where(in_blk[0], a2[0:base], 0.0)
        for r in range(1, nt):
            dg = jnp.where(in_blk[r], a2[base * r:base * (r + 1)], dg)
        dgs.append(dg)
    ts = [jnp.where((lane % base) == sub, 1.0, 0.0).astype(F32) for _ in a2s]
    for jcol in range(base - 1):
        idx = (lane // base) * base + jcol
        for i, dg in enumerate(dgs):
            acol = jnp.take_along_axis(dg, idx, axis=1)
            ts[i] = ts[i] - acol * ts[i][jcol:jcol + 1, :]
    tbds = [_blockdiag(jnp.concatenate([jnp.where(in_blk[r], t, 0.0) for r in range(nt)], axis=0))
            for t in ts]

    size = base
    while size < c:
        off = ((row2 // (2 * size)) == (col2 // (2 * size))) & ((row2 // size) != (col2 // size))
        tbs = [tbd.astype(BF16) for tbd in tbds]
        tqs = [jnp.dot(tb, _blockdiag(jnp.where(off, a2, 0.0)).astype(BF16), preferred_element_type=F32)
               for tb, a2 in zip(tbs, a2s)]
        tbds = [tbd - jnp.dot(tq.astype(BF16), tb, preferred_element_type=F32)
                for tbd, tq, tb in zip(tbds, tqs, tbs)]
        size *= 2
    return tbds


def _gdn_scan_kernel(q_ref, k_ref, v_ref, ab_ref, alog_ref, dt_ref, o_ref,
                     s_ref, cg_ref, cb_ref, rg_ref, rb_ref, gt_ref, *, base):
    c = CHUNK
    d = pl.program_id(1)
    step = pl.program_id(2)
    rev = d == 1

    @pl.when(step == 0)
    def _():
        s_ref[...] = jnp.zeros(s_ref.shape, F32)

    ab = ab_ref[0]
    beta_all = jax.nn.sigmoid(ab)
    xs = ab + dt_ref[0]
    g_all = -jnp.exp(alog_ref[0]) * (jnp.maximum(xs, 0.0) + jnp.log1p(jnp.exp(-jnp.abs(xs))))
    ri = lax.broadcasted_iota(jnp.int32, (c, c), 0)
    ci = lax.broadcasted_iota(jnp.int32, (c, c), 1)
    sgn = 1 - 2 * d
    cum = jnp.where((ci - ri) * sgn <= 0, 1.0, 0.0).astype(BF16)
    gam_all = sum(jnp.dot(cum, part, preferred_element_type=F32) for part in _split3(g_all))
    gtot_all = jnp.sum(g_all, axis=0, keepdims=True)

    pi = lax.broadcasted_iota(jnp.int32, (GDN_QK_HEADS, LANES), 0)
    li = lax.broadcasted_iota(jnp.int32, (GDN_QK_HEADS, LANES), 1)
    zpad = jnp.zeros((c, LANES), BF16)

    def pair_rows(x, col0):
        sel_e = jnp.where(li == col0 + 2 * pi, 1.0, 0.0).astype(BF16)
        sel_o = jnp.where(li == col0 + 2 * pi + 1, 1.0, 0.0).astype(BF16)
        acc = jnp.zeros((GDN_QK_HEADS, LANES), F32)
        for part in _split3(x):
            acc = acc + lax.dot_general(sel_e, jnp.concatenate([part, zpad], axis=0), NT_DIMS,
                                        preferred_element_type=F32)
            acc = acc + lax.dot_general(sel_o, jnp.concatenate([zpad, part], axis=0), NT_DIMS,
                                        preferred_element_type=F32)
        return acc

    rg_ref[...] = pair_rows(gam_all, GDN_V_HEADS)
    rb_ref[...] = pair_rows(beta_all, 0)
    for h in range(GDN_V_HEADS):
        cg_ref[h] = jnp.broadcast_to(gam_all[:, GDN_V_HEADS + h:GDN_V_HEADS + h + 1], (c, LANES))
        cb_ref[h] = jnp.broadcast_to(beta_all[:, h:h + 1], (c, LANES))
        gt_ref[h:h + 1, :] = jnp.broadcast_to(gtot_all[:, GDN_V_HEADS + h:GDN_V_HEADS + h + 1], (1, LANES))

    row2 = lax.broadcasted_iota(jnp.int32, (c, LANES), 0)
    lane2 = lax.broadcasted_iota(jnp.int32, (c, LANES), 1)
    col2 = lane2 % c
    first2 = lane2 < c

    pairs = range(GDN_QK_HEADS)
    heads = range(GDN_V_HEADS)

    qkks = []
    for p in pairs:
        q = q_ref[0, p]
        k = k_ref[0, p]
        qkks.append(lax.dot_general(jnp.concatenate([q, k], axis=0), jnp.concatenate([k, k], axis=0),
                                    NT_DIMS, preferred_element_type=F32))
    a2s, qkm2s = [], []
    for p in pairs:
        gc2 = jnp.where(first2, cg_ref[2 * p], cg_ref[2 * p + 1])
        bc2 = jnp.where(first2, cb_ref[2 * p], cb_ref[2 * p + 1])
        gr2 = rg_ref[p:p + 1, :]
        br2 = rb_ref[p:p + 1, :]
        e2 = jnp.exp(-jnp.abs(gc2 - gr2))
        a2s.append(jnp.where(row2 > col2, jnp.where(rev, br2, bc2) * qkks[p][c:2 * c] * e2, 0.0))
        qkm2s.append(jnp.where((row2 - col2) * sgn >= 0, qkks[p][0:c] * e2, 0.0))

    tbds = _unit_lower_inverses(a2s, base)

    uws = []
    for p in pairs:
        kf = k_ref[0, p].astype(F32)
        rhs = []
        for h in (2 * p, 2 * p + 1):
            cb = cb_ref[h]
            vb = v_ref[0, h].astype(F32) * cb
            kb = kf * (cb * jnp.exp(cg_ref[h]))
            rhs.append(jnp.concatenate([vb, kb], axis=1).astype(BF16))
        tuse = jnp.where(rev, tbds[p].T, tbds[p]).astype(BF16)
        uws.append(jnp.dot(tuse, jnp.concatenate(rhs, axis=0), preferred_element_type=F32))

    ws_qs = []
    for h in heads:
        w = uws[h // 2][(h % 2) * c:(h % 2 + 1) * c, GDN_DIM:2 * GDN_DIM]
        qd = q_ref[0, h // 2].astype(F32) * jnp.exp(cg_ref[h])
        wq = jnp.concatenate([w.astype(BF16), qd.astype(BF16)], axis=0)
        ws_qs.append(jnp.dot(wq, s_ref[h].astype(BF16), preferred_element_type=F32))

    for h in heads:
        u = uws[h // 2][(h % 2) * c:(h % 2 + 1) * c, 0:GDN_DIM]
        v_new = (u - ws_qs[h][0:c]).astype(BF16)
        qkm = qkm2s[h // 2][:, (h % 2) * c:(h % 2 + 1) * c].astype(BF16)
        o = ws_qs[h][c:2 * c] + jnp.dot(qkm, v_new, preferred_element_type=F32)
        o_ref[0, 0, h] = o.astype(o_ref.dtype)
        gt = gt_ref[h:h + 1, :]
        k_dec = (k_ref[0, h // 2].astype(F32) * jnp.exp(gt - cg_ref[h])).astype(BF16)
        s_ref[h] = s_ref[h] * jnp.exp(gt) + lax.dot_general(k_dec, v_new, TN_DIMS,
                                                           preferred_element_type=F32)


def _gdn_scan_call(qk, v, ab, alog, dt, *, ctx_len=CTX_LEN, base=8):
    b, _, tok, _ = qk.shape
    nc = tok // CHUNK
    ncc = ctx_len // CHUNK

    def chunk(dd, s):
        back = jnp.where(s < ncc, ncc - 1 - s, nc - 1 - (s - ncc))
        return jnp.where(dd == 0, s, back)

    return pl.pallas_call(
        functools.partial(_gdn_scan_kernel, base=base),
        out_shape=jax.ShapeDtypeStruct((2, b, GDN_V_HEADS, tok, LANES), BF16),
        grid=(b, 2, nc),
        in_specs=[
            pl.BlockSpec((1, GDN_QK_HEADS, CHUNK, LANES), lambda bi, dd, s: (bi, 0, chunk(dd, s), 0)),
            pl.BlockSpec((1, GDN_QK_HEADS, CHUNK, LANES), lambda bi, dd, s: (bi, 1, chunk(dd, s), 0)),
            pl.BlockSpec((1, GDN_V_HEADS, CHUNK, LANES), lambda bi, dd, s: (bi, 0, chunk(dd, s), 0)),
            pl.BlockSpec((1, CHUNK, LANES), lambda bi, dd, s: (bi, chunk(dd, s), dd)),
            pl.BlockSpec((1, 1, LANES), lambda bi, dd, s: (dd, 0, 0)),
            pl.BlockSpec((1, 1, LANES), lambda bi, dd, s: (dd, 0, 0)),
        ],
        out_specs=pl.BlockSpec((1, 1, GDN_V_HEADS, CHUNK, LANES),
                               lambda bi, dd, s: (dd, bi, 0, chunk(dd, s), 0)),
        scratch_shapes=[
            pltpu.VMEM((GDN_V_HEADS, GDN_DIM, GDN_DIM), F32),
            pltpu.VMEM((GDN_V_HEADS, CHUNK, LANES), F32),
            pltpu.VMEM((GDN_V_HEADS, CHUNK, LANES), F32),
            pltpu.VMEM((GDN_QK_HEADS, LANES), F32),
            pltpu.VMEM((GDN_QK_HEADS, LANES), F32),
            pltpu.VMEM((GDN_V_HEADS, LANES), F32),
        ],
        compiler_params=_params(("parallel", "parallel", "arbitrary")),
        name="gdn_chunk_scan",
    )(qk, qk, v, ab, alog, dt)


def _final_norm_kernel(x_ref, w_ref, o_ref):
    xv = x_ref[0]
    o_ref[0] = xv * lax.rsqrt(jnp.mean(xv * xv, axis=-1, keepdims=True) + EPS) * w_ref[...]


def _final_norm_call(x, w, *, ctx_len=CTX_LEN):
    b, tok, d = x.shape
    tr = ctx_len
    return pl.pallas_call(
        _final_norm_kernel,
        out_shape=jax.ShapeDtypeStruct((b, tok - ctx_len, d), F32),
        grid=(b, (tok - ctx_len) // tr),
        in_specs=[pl.BlockSpec((1, tr, d), lambda bi, t: (bi, t + 1, 0)),
                  pl.BlockSpec((1, d), lambda bi, t: (0, 0))],
        out_specs=pl.BlockSpec((1, tr, d), lambda bi, t: (bi, t, 0)),
        compiler_params=_params(("parallel", "parallel")),
        name="final_rmsnorm",
    )(x, w.reshape(1, d))


def _rope_tables(seq, ctx_len, width):
    t = jnp.arange(seq, dtype=jnp.int32)
    rows = (t // GRID_W).astype(F32)
    cols = (t % GRID_W).astype(F32)
    n_freq = DA_QK_DIM // 4
    inv_freq = ROPE_BASE ** (-jnp.arange(n_freq, dtype=F32) / n_freq)
    ar = rows[:, None] * inv_freq
    ac = cols[:, None] * inv_freq
    cos = jnp.concatenate([jnp.cos(ar), jnp.cos(ar), jnp.cos(ac), jnp.cos(ac)], axis=1)
    sin = jnp.concatenate([-jnp.sin(ar), jnp.sin(ar), -jnp.sin(ac), jnp.sin(ac)], axis=1)
    cos = jnp.concatenate([jnp.ones((ctx_len, DA_QK_DIM), F32), cos], axis=0)
    sin = jnp.concatenate([jnp.zeros((ctx_len, DA_QK_DIM), F32), sin], axis=0)
    reps = width // DA_QK_DIM
    return jnp.tile(cos, (1, reps)), jnp.tile(sin, (1, reps))


def kernel(x, c, ctx, c_ctx, w_mod, b_mod, norm_mix, norm_ffn, da_w_qkv, da_lambda, da_head_gain, da_w_o,
           gdn_w_in, gdn_conv, gdn_a_log, gdn_dt_bias, gdn_norm_gain, gdn_w_o, ffn_w_up, ffn_conv,
           ffn_w_down, final_norm):
    bsz, seq, d = x.shape
    ctx_len = ctx.shape[1]
    tm = (ctx_len + seq) // 3

    xa = jnp.concatenate([ctx, x], axis=1)
    pad_rows = 16 - (bsz + 1)
    cvec = jnp.concatenate([c, c_ctx[None, :], jnp.zeros((pad_rows, d), F32)], axis=0)
    mods_all = _mods_call(cvec, w_mod, b_mod)
    cos_t, sin_t = _rope_tables(seq, ctx_len, 512)

    def sel(mods, idx):
        lat = mods[:bsz, idx * d:(idx + 1) * d]
        cx = jnp.broadcast_to(mods[bsz:bsz + 1, idx * d:(idx + 1) * d], (bsz, d))
        return lat, cx

    def modv(mods, i_shift, i_scale):
        sl, sc = sel(mods, i_shift)
        cl, cc = sel(mods, i_scale)
        return jnp.stack([sl, cl, sc, cc], axis=1)

    def gatev(mods, idx):
        gl, gc = sel(mods, idx)
        return jnp.stack([gl, gc], axis=1)

    for i in range(DEPTH):
        mods = mods_all[i]
        jm = i // 2
        mv = modv(mods, 0, 1)
        if i % 2 == 0:
            lam_init = 0.8 - 0.6 * math.exp(-0.3 * i)
            qkv = _proj_call(xa, mv, norm_mix[i], [da_w_qkv[jm].astype(BF16)], [], epi="rope", tn=512,
                             tm=tm, extras=(cos_t, sin_t), q_tiles=DA_QK_W // 512, ctx_len=ctx_len)
            att = _diff_attn_call(qkv, da_lambda[jm], da_head_gain[jm], lam_init, ctx_len=ctx_len)
            xa = _oproj_call(att, da_w_o[jm].astype(BF16), xa, gatev(mods, 2), tm=tm, tn=512,
                             ctx_len=ctx_len)
        else:
            w_in = gdn_w_in[jm].astype(BF16)
            cw = gdn_conv[jm]
            qk = _proj_call(xa, mv, norm_mix[i], [w_in[:, :2 * GDN_QK_W]], [cw[:, :2 * GDN_QK_W]],
                            epi="gdn_qk", tn=512, tm=tm, q_tiles=GDN_QK_W // 512, ctx_len=ctx_len)
            vv = _proj_call(xa, mv, norm_mix[i], [w_in[:, 2 * GDN_QK_W:GDN_QKV_W]],
                            [cw[:, 2 * GDN_QK_W:GDN_QKV_W]], epi="gdn_v", tn=512, tm=tm, ctx_len=ctx_len)
            z = _proj_call(xa, mv, norm_mix[i], [w_in[:, GDN_QKV_W:GDN_QKV_W + GDN_V_W]], [],
                           epi="plain", tn=512, tm=tm, ctx_len=ctx_len)
            w_ab = w_in[:, GDN_QKV_W + GDN_V_W:].reshape(d, 2, 2 * GDN_V_HEADS)
            w_ab = jnp.concatenate([w_ab, jnp.zeros_like(w_ab)], axis=2).reshape(d, 2 * LANES)
            ab = _proj_call(xa, mv, norm_mix[i], [w_ab], [], epi="plain", tn=2 * LANES, tm=tm,
                            out_dtype=F32, ctx_len=ctx_len)
            zeros = jnp.zeros((2, GDN_V_HEADS), F32)
            alog = jnp.concatenate([zeros, gdn_a_log[jm], zeros, zeros], axis=1).reshape(2, 1, LANES)
            dtb = jnp.concatenate([zeros, gdn_dt_bias[jm], zeros, zeros], axis=1).reshape(2, 1, LANES)
            o = _gdn_scan_call(qk, vv, ab, alog, dtb, ctx_len=ctx_len)
            xa = _gdn_oproj_call(o, z, gdn_norm_gain[jm], gdn_w_o[jm].astype(BF16), xa, gatev(mods, 2),
                                 tm=tm // 2, tn=512, ctx_len=ctx_len)
        padc = D_FF_PAD - D_FF
        w_up = ffn_w_up[i].astype(BF16)
        w_g = jnp.pad(w_up[:, :D_FF], ((0, 0), (0, padc)))
        w_v = jnp.pad(w_up[:, D_FF:], ((0, 0), (0, padc)))
        cwf = ffn_conv[i]
        cw_g = jnp.pad(cwf[:, :D_FF], ((0, 0), (0, padc)))
        cw_v = jnp.pad(cwf[:, D_FF:], ((0, 0), (0, padc)))
        act = _proj_call(xa, modv(mods, 3, 4), norm_ffn[i], [w_g, w_v], [cw_g, cw_v], epi="ffn", tn=512,
                         tm=tm, ctx_len=ctx_len)
        w_dn = jnp.pad(ffn_w_down[i].astype(BF16), ((0, padc), (0, 0)))
        xa = _oproj_call(act, w_dn, xa, gatev(mods, 5), tm=tm, tn=512, ctx_len=ctx_len)
    return _final_norm_call(xa, final_norm, ctx_len=ctx_len)
```

```python
import functools
import math

import jax
import jax.numpy as jnp
from jax import lax
from jax.experimental import pallas as pl
from jax.experimental.pallas import tpu as pltpu

F32 = jnp.float32
BF16 = jnp.bfloat16

D_MODEL = 2048
DEPTH = 4
CTX_LEN = 256
GRID_W = 64
EPS = 1e-6
DA_HEADS = 8
DA_QK_DIM = 128
DA_V_DIM = 256
DA_QK_W = DA_HEADS * 2 * DA_QK_DIM
DA_V_W = DA_HEADS * DA_V_DIM
ROPE_BASE = 10000.0
GDN_QK_HEADS = 16
GDN_V_HEADS = 32
GDN_DIM = 128
GDN_QK_W = GDN_QK_HEADS * GDN_DIM
GDN_V_W = GDN_V_HEADS * GDN_DIM
GDN_QKV_W = 2 * GDN_QK_W + GDN_V_W
GDN_CONV = 5
CHUNK = 64
D_FF = 5504
FFN_CONV = 3

LANES = 128
HALO = 16
D_FF_PAD = 5632
VMEM_LIMIT = 56 * 1024 * 1024

NT_DIMS = (((1,), (1,)), ((), ()))
TN_DIMS = (((0,), (0,)), ((), ()))


def _silu(x):
    return x * jax.nn.sigmoid(x)


def _params(sem):
    return pltpu.CompilerParams(dimension_semantics=sem, vmem_limit_bytes=VMEM_LIMIT)


def _mods_kernel(c_ref, w_ref, b_ref, o_ref):
    s = _silu(c_ref[...]).astype(BF16)
    w = w_ref[0].astype(BF16)
    o_ref[0] = jnp.dot(s, w, preferred_element_type=F32) + b_ref[0]


def _mods_call(cvec, w_mod, b_mod):
    depth, d, n = w_mod.shape
    tn = 1024
    return pl.pallas_call(
        _mods_kernel,
        out_shape=jax.ShapeDtypeStruct((depth, cvec.shape[0], n), F32),
        grid=(depth, n // tn),
        in_specs=[
            pl.BlockSpec((cvec.shape[0], d), lambda i, j: (0, 0)),
            pl.BlockSpec((1, d, tn), lambda i, j: (i, 0, j)),
            pl.BlockSpec((1, 1, tn), lambda i, j: (i, 0, j)),
        ],
        out_specs=pl.BlockSpec((1, cvec.shape[0], tn), lambda i, j: (i, 0, j)),
        compiler_params=_params(("parallel", "parallel")),
        name="adaln_mods",
    )(cvec, w_mod, b_mod.reshape(depth, 1, n))


MXU_N = 256
EPI_ROWS = 128
NORM_ROWS = 16


def _proj_kernel(*refs, n_w, taps, epi, tm, tn, ctx_len, tok, q_tiles):
    halo = taps > 0
    ns = tn // MXU_N
    it = iter(refs)
    x_ref = next(it)
    xp_ref = next(it) if halo else None
    xn_ref = next(it) if halo else None
    modv_ref = next(it)
    nw_ref = next(it)
    w_refs = [next(it) for _ in range(n_w)]
    cw_refs = [next(it) for _ in range(n_w)] if halo else []
    ex_refs = [next(it) for _ in range(2)] if epi == "rope" else []
    out_ref = next(it)
    h_ref = next(it)
    u_refs = [[next(it) for _ in range(ns)] for _ in range(n_w)]

    t = pl.program_id(1)
    j = pl.program_id(2)
    off = HALO if halo else 0
    p = taps // 2

    @pl.when(j == 0)
    def _():
        nw = nw_ref[...]
        gain_l = nw * (1.0 + modv_ref[0, 1:2])
        gain_c = nw * (1.0 + modv_ref[0, 3:4])
        shift_l = modv_ref[0, 0:1]
        shift_c = modv_ref[0, 2:3]

        def norm_rows(src_ref, n_rows, dst0, g0, keep):
            def body(i, carry):
                r = pl.multiple_of(i * NORM_ROWS, NORM_ROWS)
                xv = src_ref[0, pl.ds(r, NORM_ROWS), :]
                inv = lax.rsqrt(jnp.mean(xv * xv, axis=-1, keepdims=True) + EPS)
                is_ctx = (g0 + r + lax.broadcasted_iota(jnp.int32, (NORM_ROWS, 1), 0)) < ctx_len
                hv = (xv * inv) * jnp.where(is_ctx, gain_c, gain_l) + jnp.where(is_ctx, shift_c, shift_l)
                if keep is not None:
                    hv = jnp.where(keep, hv, 0.0)
                h_ref[pl.ds(dst0 + r, NORM_ROWS), :] = hv.astype(BF16)
                return carry
            lax.fori_loop(0, n_rows // NORM_ROWS, body, 0, unroll=2 if n_rows > NORM_ROWS else 1)

        norm_rows(x_ref, tm, off, t * tm, None)
        if halo:
            keep_prev = (t * tm != 0) & (t * tm != ctx_len)
            keep_next = (t * tm + tm != tok) & (t * tm + tm != ctx_len)
            norm_rows(xp_ref, HALO, 0, t * tm - HALO, keep_prev)
            norm_rows(xn_ref, HALO, off + tm, t * tm + tm, keep_next)

    for s in range(ns):
        for w, u_s in zip(w_refs, u_refs):
            u_s[s][...] = jnp.dot(h_ref[...], w[:, s * MXU_N:(s + 1) * MXU_N], preferred_element_type=F32)

    def conv_rows(u_ref, cw_ref, s, r0, nrows, edge):
        if not halo:
            return u_ref[r0:r0 + nrows, :]
        acc = None
        for jj in range(taps):
            dlt = jj - p
            sh = u_ref[off + r0 + dlt:off + r0 + dlt + nrows, :]
            if edge is not None and dlt != 0:
                ri = r0 + lax.broadcasted_iota(jnp.int32, (nrows, 1), 0)
                same = ((ri >= edge) & (ri + dlt >= edge)) | ((ri < edge) & (ri + dlt < edge))
                sh = jnp.where(same, sh, 0.0)
            term = sh * cw_ref[jj:jj + 1, s * MXU_N:(s + 1) * MXU_N]
            acc = term if acc is None else acc + term
        return acc

    def finish(s, r0, nrows, edge=None):
        vals = [conv_rows(u_s[s], cw, s, r0, nrows, edge)
                for u_s, cw in zip(u_refs, cw_refs if halo else [None] * n_w)]
        rs = slice(r0, r0 + nrows)
        cs = slice(s * MXU_N, (s + 1) * MXU_N)
        hpt = MXU_N // LANES
        if epi == "ffn":
            out_ref[0, rs, cs] = (_silu(vals[0]) * vals[1]).astype(out_ref.dtype)
        elif epi == "plain":
            out_ref[0, rs, cs] = vals[0].astype(out_ref.dtype)
        elif epi == "gdn_v":
            a = _silu(vals[0])
            for hh in range(hpt):
                out_ref[0, s * hpt + hh, rs] = a[:, hh * LANES:(hh + 1) * LANES].astype(out_ref.dtype)
        elif epi == "gdn_qk":
            a = _silu(vals[0])
            qs = jnp.where(j < q_tiles, GDN_DIM ** -0.5, 1.0).astype(F32)
            for hh in range(hpt):
                xs = a[:, hh * LANES:(hh + 1) * LANES]
                nrm = xs * lax.rsqrt(jnp.sum(xs * xs, axis=-1, keepdims=True) + EPS)
                out_ref[0, s * hpt + hh, rs] = (nrm * qs).astype(out_ref.dtype)
        elif epi == "rope":
            u = vals[0]
            cos = jnp.where(j < 2 * q_tiles, ex_refs[0][rs, :], 1.0)
            sin = jnp.where(j < 2 * q_tiles, ex_refs[1][rs, :], 0.0)
            lane = lax.broadcasted_iota(jnp.int32, (1, MXU_N), 1)
            swapped = jnp.where((lane % 64) < 32, pltpu.roll(u, MXU_N - 32, 1), pltpu.roll(u, 32, 1))
            sc = jnp.where(j < q_tiles, DA_QK_DIM ** -0.5, 1.0).astype(F32)
            out_ref[0, rs, cs] = ((u * cos + swapped * sin) * sc).astype(out_ref.dtype)
        else:
            raise ValueError(epi)

    for s in range(ns):
        for r0 in range(0, tm, EPI_ROWS):
            finish(s, r0, EPI_ROWS)

    if halo and ctx_len % tm != 0:
        edge = ctx_len % tm

        @pl.when(t == ctx_len // tm)
        def _():
            for s in range(ns):
                finish(s, edge - HALO, 2 * HALO, edge)


def _proj_call(x, modv, nw, ws, cws, *, epi, tn, tm, out_dtype=BF16, extras=(), q_tiles=0,
               ctx_len=CTX_LEN):
    b, tok, d = x.shape
    n = ws[0].shape[1]
    taps = cws[0].shape[0] if cws else 0
    halo = taps > 0
    nt = tok // tm
    hb = tm // HALO
    nhb = tok // HALO

    in_specs = [pl.BlockSpec((1, tm, d), lambda bi, t, j: (bi, t, 0))]
    args = [x]
    if halo:
        in_specs += [
            pl.BlockSpec((1, HALO, d), lambda bi, t, j: (bi, jnp.maximum(t * hb - 1, 0), 0)),
            pl.BlockSpec((1, HALO, d), lambda bi, t, j: (bi, jnp.minimum((t + 1) * hb, nhb - 1), 0)),
        ]
        args += [x, x]
    in_specs += [pl.BlockSpec((1, 4, d), lambda bi, t, j: (bi, 0, 0)),
                 pl.BlockSpec((1, d), lambda bi, t, j: (0, 0))]
    args += [modv, nw.reshape(1, d)]
    for w in ws:
        in_specs.append(pl.BlockSpec((d, tn), lambda bi, t, j: (0, j)))
        args.append(w)
    for cw in cws:
        in_specs.append(pl.BlockSpec((taps, tn), lambda bi, t, j: (0, j)))
        args.append(cw)
    for e in extras:
        in_specs.append(pl.BlockSpec((tm, MXU_N), lambda bi, t, j: (t, 0)))
        args.append(e)

    if epi in ("gdn_v", "gdn_qk"):
        out_shape = jax.ShapeDtypeStruct((b, n // LANES, tok, LANES), out_dtype)
        out_spec = pl.BlockSpec((1, tn // LANES, tm, LANES), lambda bi, t, j: (bi, j, t, 0))
    else:
        out_shape = jax.ShapeDtypeStruct((b, tok, n), out_dtype)
        out_spec = pl.BlockSpec((1, tm, tn), lambda bi, t, j: (bi, t, j))

    kern = functools.partial(_proj_kernel, n_w=len(ws), taps=taps, epi=epi, tm=tm, tn=tn,
                             ctx_len=ctx_len, tok=tok, q_tiles=q_tiles)
    return pl.pallas_call(
        kern,
        out_shape=out_shape,
        grid=(b, nt, n // tn),
        in_specs=in_specs,
        out_specs=out_spec,
        scratch_shapes=([pltpu.VMEM((tm + (2 * HALO if halo else 0), d), BF16)]
                        + [pltpu.VMEM((tm + (2 * HALO if halo else 0), MXU_N), F32)
                           for _ in range(len(ws) * (tn // MXU_N))]),
        compiler_params=_params(("parallel", "parallel", "arbitrary")),
        name="norm_proj_" + epi,
    )(*args)


def _row_gate(g_ref, t, tm, ctx_len):
    g = t * tm + lax.broadcasted_iota(jnp.int32, (tm, 1), 0)
    return jnp.where(g < ctx_len, g_ref[0, 1:2], g_ref[0, 0:1])


def _oproj_kernel(y_ref, w_ref, x_ref, g_ref, o_ref, *, tm, ctx_len):
    acc = jnp.dot(y_ref[0], w_ref[...], preferred_element_type=F32)
    o_ref[0] = x_ref[0] + _row_gate(g_ref, pl.program_id(1), tm, ctx_len) * acc


def _oproj_call(y, w, x, gate, *, tm, tn, ctx_len=CTX_LEN):
    b, tok, d = x.shape
    k = y.shape[-1]
    return pl.pallas_call(
        functools.partial(_oproj_kernel, tm=tm, ctx_len=ctx_len),
        out_shape=jax.ShapeDtypeStruct(x.shape, F32),
        grid=(b, tok // tm, d // tn),
        in_specs=[
            pl.BlockSpec((1, tm, k), lambda bi, t, j: (bi, t, 0)),
            pl.BlockSpec((k, tn), lambda bi, t, j: (0, j)),
            pl.BlockSpec((1, tm, tn), lambda bi, t, j: (bi, t, j)),
            pl.BlockSpec((1, 2, tn), lambda bi, t, j: (bi, 0, j)),
        ],
        out_specs=pl.BlockSpec((1, tm, tn), lambda bi, t, j: (bi, t, j)),
        compiler_params=_params(("parallel", "parallel", "arbitrary")),
        name="out_proj_residual",
    )(y, w, x, gate)


def _gdn_oproj_kernel(o_ref, z_ref, gain_ref, w_ref, x_ref, g_ref, out_ref, y_ref, *, tm, ctx_len):
    j = pl.program_id(2)

    @pl.when(j == 0)
    def _():
        gain = gain_ref[...]
        for h in range(GDN_V_HEADS):
            o = o_ref[0, 0, h].astype(F32) + o_ref[1, 0, h].astype(F32)
            y = o * lax.rsqrt(jnp.mean(o * o, axis=-1, keepdims=True) + EPS) * gain
            z = z_ref[0, :, h * LANES:(h + 1) * LANES].astype(F32)
            y_ref[:, h * LANES:(h + 1) * LANES] = (y * _silu(z)).astype(BF16)

    acc = jnp.dot(y_ref[...], w_ref[...], preferred_element_type=F32)
    out_ref[0] = x_ref[0] + _row_gate(g_ref, pl.program_id(1), tm, ctx_len) * acc


def _gdn_oproj_call(o, z, gain, w, x, gate, *, tm, tn, ctx_len=CTX_LEN):
    b, tok, d = x.shape
    k = w.shape[0]
    return pl.pallas_call(
        functools.partial(_gdn_oproj_kernel, tm=tm, ctx_len=ctx_len),
        out_shape=jax.ShapeDtypeStruct(x.shape, F32),
        grid=(b, tok // tm, d // tn),
        in_specs=[
            pl.BlockSpec((2, 1, GDN_V_HEADS, tm, LANES), lambda bi, t, j: (0, bi, 0, t, 0)),
            pl.BlockSpec((1, tm, k), lambda bi, t, j: (bi, t, 0)),
            pl.BlockSpec((1, LANES), lambda bi, t, j: (0, 0)),
            pl.BlockSpec((k, tn), lambda bi, t, j: (0, j)),
            pl.BlockSpec((1, tm, tn), lambda bi, t, j: (bi, t, j)),
            pl.BlockSpec((1, 2, tn), lambda bi, t, j: (bi, 0, j)),
        ],
        out_specs=pl.BlockSpec((1, tm, tn), lambda bi, t, j: (bi, t, j)),
        scratch_shapes=[pltpu.VMEM((tm, k), BF16)],
        compiler_params=_params(("parallel", "parallel", "arbitrary")),
        name="gdn_out_proj_residual",
    )(o, z, gain.reshape(1, LANES), w, x, gate)


def _diff_attn_kernel(q_ref, k_ref, v_ref, lam_ref, gain_ref, o_ref, *, lam_init, ctx_len):
    lv = lam_ref[...]
    lam = (jnp.exp(jnp.sum(lv[0:1] * lv[1:2], axis=-1, keepdims=True))
           - jnp.exp(jnp.sum(lv[2:3] * lv[3:4], axis=-1, keepdims=True)) + lam_init)

    def attend(kk, vv):
        q = q_ref[0]
        ps = []
        for c in range(2):
            s = lax.dot_general(q[:, c * DA_QK_DIM:(c + 1) * DA_QK_DIM],
                                kk[:, c * DA_QK_DIM:(c + 1) * DA_QK_DIM], NT_DIMS,
                                preferred_element_type=F32)
            p = jnp.exp(s - jnp.max(s, axis=-1, keepdims=True))
            ps.append(p * (1.0 / jnp.sum(p, axis=-1, keepdims=True)))
        a = ps[0] - lam * ps[1]
        o = jnp.dot(a.astype(BF16), vv, preferred_element_type=F32)
        o = o * lax.rsqrt(jnp.mean(o * o, axis=-1, keepdims=True) + EPS) * gain_ref[...]
        o_ref[0] = (o * (1.0 - lam_init)).astype(o_ref.dtype)

    t = pl.program_id(2)

    @pl.when(t == 0)
    def _():
        attend(k_ref[0, 0:ctx_len], v_ref[0, 0:ctx_len])

    @pl.when(t != 0)
    def _():
        attend(k_ref[0], v_ref[0])


def _diff_attn_call(qkv, lam_vecs, head_gain, lam_init, *, ctx_len=CTX_LEN):
    b, tok, _ = qkv.shape
    hw = 2 * DA_QK_DIM
    tq = ctx_len
    return pl.pallas_call(
        functools.partial(_diff_attn_kernel, lam_init=lam_init, ctx_len=ctx_len),
        out_shape=jax.ShapeDtypeStruct((b, tok, DA_V_W), BF16),
        grid=(b, DA_HEADS, tok // tq),
        in_specs=[
            pl.BlockSpec((1, tq, hw), lambda bi, h, t: (bi, t, h)),
            pl.BlockSpec((1, tok, hw), lambda bi, h, t: (bi, 0, DA_HEADS + h)),
            pl.BlockSpec((1, tok, DA_V_DIM), lambda bi, h, t: (bi, 0, 2 * DA_HEADS + h)),
            pl.BlockSpec((4, DA_QK_DIM), lambda bi, h, t: (0, 0)),
            pl.BlockSpec((1, DA_V_DIM), lambda bi, h, t: (0, 0)),
        ],
        out_specs=pl.BlockSpec((1, tq, DA_V_DIM), lambda bi, h, t: (bi, t, h)),
        compiler_params=_params(("parallel", "parallel", "arbitrary")),
        name="diff_attention",
    )(qkv, qkv, qkv, lam_vecs, head_gain.reshape(1, DA_V_DIM))


def _split3(x):
    x1 = x.astype(BF16)
    r1 = x - x1.astype(F32)
    x2 = r1.astype(BF16)
    x3 = (r1 - x2.astype(F32)).astype(BF16)
    return x1, x2, x3


def _blockdiag(x2):
    first = lax.broadcasted_iota(jnp.int32, x2.shape, 1) < CHUNK
    return jnp.concatenate([jnp.where(first, x2, 0.0), jnp.where(first, 0.0, x2)], axis=0)


def _unit_lower_inverses(a2s, base):
    c = CHUNK
    nt = c // base
    assert base == 8
    lane = lax.broadcasted_iota(jnp.int32, (base, LANES), 1)
    sub = lax.broadcasted_iota(jnp.int32, (base, LANES), 0)
    row2 = lax.broadcasted_iota(jnp.int32, (c, LANES), 0)
    col2 = lax.broadcasted_iota(jnp.int32, (c, LANES), 1) % c

    in_blk = [((lane % c) // base) == r for r in range(nt)]
    dgs = []
    for a2 in a2s:
        dg = jnp.where(in_blk[0], a2[0:base], 0.0)
        for r in range(1, nt):
            dg = jnp.where(in_blk[r], a2[base * r:base * (r + 1)], dg)
        dgs.append(dg)
    ts = [jnp.where((lane % base) == sub, 1.0, 0.0).astype(F32) for _ in a2s]
    for jcol in range(base - 1):
        idx = (lane // base) * base + jcol
        for i, dg in enumerate(dgs):
            acol = jnp.take_along_axis(dg, idx, axis=1)
            ts[i] = ts[i] - acol * ts[i][jcol:jcol + 1, :]
    tbds = [_blockdiag(jnp.concatenate([jnp.where(in_blk[r], t, 0.0) for r in range(nt)], axis=0))
            for t in ts]

    size = base
    while size < c:
        off = ((row2 // (2 * size)) == (col2 // (2 * size))) & ((row2 // size) != (col2 // size))
        tbs = [tbd.astype(BF16) for tbd in tbds]
        tqs = [jnp.dot(tb, _blockdiag(jnp.where(off, a2, 0.0)).astype(BF16), preferred_element_type=F32)
               for tb, a2 in zip(tbs, a2s)]
        tbds = [tbd - jnp.dot(tq.astype(BF16), tb, preferred_element_type=F32)
                for tbd, tq, tb in zip(tbds, tqs, tbs)]
        size *= 2
    return tbds


def _gdn_scan_kernel(q_ref, k_ref, v_ref, ab_ref, alog_ref, dt_ref, o_ref,
                     s_ref, cg_ref, cb_ref, rg_ref, rb_ref, gt_ref, *, base):
    c = CHUNK
    d = pl.program_id(1)
    step = pl.program_id(2)
    rev = d == 1

    @pl.when(step == 0)
    def _():
        s_ref[...] = jnp.zeros(s_ref.shape, F32)

    ab = ab_ref[0]
    beta_all = jax.nn.sigmoid(ab)
    xs = ab + dt_ref[0]
    g_all = -jnp.exp(alog_ref[0]) * (jnp.maximum(xs, 0.0) + jnp.log1p(jnp.exp(-jnp.abs(xs))))
    ri = lax.broadcasted_iota(jnp.int32, (c, c), 0)
    ci = lax.broadcasted_iota(jnp.int32, (c, c), 1)
    sgn = 1 - 2 * d
    cum = jnp.where((ci - ri) * sgn <= 0, 1.0, 0.0).astype(BF16)
    gam_all = sum(jnp.dot(cum, part, preferred_element_type=F32) for part in _split3(g_all))
    gtot_all = jnp.sum(g_all, axis=0, keepdims=True)

    pi = lax.broadcasted_iota(jnp.int32, (GDN_QK_HEADS, LANES), 0)
    li = lax.broadcasted_iota(jnp.int32, (GDN_QK_HEADS, LANES), 1)
    zpad = jnp.zeros((c, LANES), BF16)

    def pair_rows(x, col0):
        sel_e = jnp.where(li == col0 + 2 * pi, 1.0, 0.0).astype(BF16)
        sel_o = jnp.where(li == col0 + 2 * pi + 1, 1.0, 0.0).astype(BF16)
        acc = jnp.zeros((GDN_QK_HEADS, LANES), F32)
        for part in _split3(x):
            acc = acc + lax.dot_general(sel_e, jnp.concatenate([part, zpad], axis=0), NT_DIMS,
                                        preferred_element_type=F32)
            acc = acc + lax.dot_general(sel_o, jnp.concatenate([zpad, part], axis=0), NT_DIMS,
                                        preferred_element_type=F32)
        return acc

    rg_ref[...] = pair_rows(gam_all, GDN_V_HEADS)
    rb_ref[...] = pair_rows(beta_all, 0)
    for h in range(GDN_V_HEADS):
        cg_ref[h] = jnp.broadcast_to(gam_all[:, GDN_V_HEADS + h:GDN_V_HEADS + h + 1], (c, LANES))
        cb_ref[h] = jnp.broadcast_to(beta_all[:, h:h + 1], (c, LANES))
        gt_ref[h:h + 1, :] = jnp.broadcast_to(gtot_all[:, GDN_V_HEADS + h:GDN_V_HEADS + h + 1], (1, LANES))

    row2 = lax.broadcasted_iota(jnp.int32, (c, LANES), 0)
    lane2 = lax.broadcasted_iota(jnp.int32, (c, LANES), 1)
    col2 = lane2 % c
    first2 = lane2 < c

    pairs = range(GDN_QK_HEADS)
    heads = range(GDN_V_HEADS)

    qkks = []
    for p in pairs:
        q = q_ref[0, p]
        k = k_ref[0, p]
        qkks.append(lax.dot_general(jnp.concatenate([q, k], axis=0), jnp.concatenate([k, k], axis=0),
                                    NT_DIMS, preferred_element_type=F32))
    a2s, qkm2s = [], []
    for p in pairs:
        gc2 = jnp.where(first2, cg_ref[2 * p], cg_ref[2 * p + 1])
        bc2 = jnp.where(first2, cb_ref[2 * p], cb_ref[2 * p + 1])
        gr2 = rg_ref[p:p + 1, :]
        br2 = rb_ref[p:p + 1, :]
        e2 = jnp.exp(-jnp.abs(gc2 - gr2))
        a2s.append(jnp.where(row2 > col2, jnp.where(rev, br2, bc2) * qkks[p][c:2 * c] * e2, 0.0))
        qkm2s.append(jnp.where((row2 - col2) * sgn >= 0, qkks[p][0:c] * e2, 0.0))

    tbds = _unit_lower_inverses(a2s, base)

    uws = []
    for p in pairs:
        kf = k_ref[0, p].astype(F32)
        rhs = []
        for h in (2 * p, 2 * p + 1):
            cb = cb_ref[h]
            vb = v_ref[0, h].astype(F32) * cb
            kb = kf * (cb * jnp.exp(cg_ref[h]))
            rhs.append(jnp.concatenate([vb, kb], axis=1).astype(BF16))
        tuse = jnp.where(rev, tbds[p].T, tbds[p]).astype(BF16)
        uws.append(jnp.dot(tuse, jnp.concatenate(rhs, axis=0), preferred_element_type=F32))

    ws_qs = []
    for h in heads:
        w = uws[h // 2][(h % 2) * c:(h % 2 + 1) * c, GDN_DIM:2 * GDN_DIM]
        qd = q_ref[0, h // 2].astype(F32) * jnp.exp(cg_ref[h])
        wq = jnp.concatenate([w.astype(BF16), qd.astype(BF16)], axis=0)
        ws_qs.append(jnp.dot(wq, s_ref[h].astype(BF16), preferred_element_type=F32))

    for h in heads:
        u = uws[h // 2][(h % 2) * c:(h % 2 + 1) * c, 0:GDN_DIM]
        v_new = (u - ws_qs[h][0:c]).astype(BF16)
        qkm = qkm2s[h // 2][:, (h % 2) * c:(h % 2 + 1) * c].astype(BF16)
        o = ws_qs[h][c:2 * c] + jnp.dot(qkm, v_new, preferred_element_type=F32)
        o_ref[0, 0, h] = o.astype(o_ref.dtype)
        gt = gt_ref[h:h + 1, :]
        k_dec = (k_ref[0, h // 2].astype(F32) * jnp.exp(gt - cg_ref[h])).astype(BF16)
        s_ref[h] = s_ref[h] * jnp.exp(gt) + lax.dot_general(k_dec, v_new, TN_DIMS,
                                                           preferred_element_type=F32)


def _gdn_scan_call(qk, v, ab, alog, dt, *, ctx_len=CTX_LEN, base=8):
    b, _, tok, _ = qk.shape
    nc = tok // CHUNK
    ncc = ctx_len // CHUNK

    def chunk(dd, s):
        back = jnp.where(s < ncc, ncc - 1 - s, nc - 1 - (s - ncc))
        return jnp.where(dd == 0, s, back)

    return pl.pallas_call(
        functools.partial(_gdn_scan_kernel, base=base),
        out_shape=jax.ShapeDtypeStruct((2, b, GDN_V_HEADS, tok, LANES), BF16),
        grid=(b, 2, nc),
        in_specs=[
            pl.BlockSpec((1, GDN_QK_HEADS, CHUNK, LANES), lambda bi, dd, s: (bi, 0, chunk(dd, s), 0)),
            pl.BlockSpec((1, GDN_QK_HEADS, CHUNK, LANES), lambda bi, dd, s: (bi, 1, chunk(dd, s), 0)),
            pl.BlockSpec((1, GDN_V_HEADS, CHUNK, LANES), lambda bi, dd, s: (bi, 0, chunk(dd, s), 0)),
            pl.BlockSpec((1, CHUNK, LANES), lambda bi, dd, s: (bi, chunk(dd, s), dd)),
            pl.BlockSpec((1, 1, LANES), lambda bi, dd, s: (dd, 0, 0)),
            pl.BlockSpec((1, 1, LANES), lambda bi, dd, s: (dd, 0, 0)),
        ],
        out_specs=pl.BlockSpec((1, 1, GDN_V_HEADS, CHUNK, LANES),
                               lambda bi, dd, s: (dd, bi, 0, chunk(dd, s), 0)),
        scratch_shapes=[
            pltpu.VMEM((GDN_V_HEADS, GDN_DIM, GDN_DIM), F32),
            pltpu.VMEM((GDN_V_HEADS, CHUNK, LANES), F32),
            pltpu.VMEM((GDN_V_HEADS, CHUNK, LANES), F32),
            pltpu.VMEM((GDN_QK_HEADS, LANES), F32),
            pltpu.VMEM((GDN_QK_HEADS, LANES), F32),
            pltpu.VMEM((GDN_V_HEADS, LANES), F32),
        ],
        compiler_params=_params(("parallel", "parallel", "arbitrary")),
        name="gdn_chunk_scan",
    )(qk, qk, v, ab, alog, dt)


def _final_norm_kernel(x_ref, w_ref, o_ref):
    xv = x_ref[0]
    o_ref[0] = xv * lax.rsqrt(jnp.mean(xv * xv, axis=-1, keepdims=True) + EPS) * w_ref[...]


def _final_norm_call(x, w, *, ctx_len=CTX_LEN):
    b, tok, d = x.shape
    tr = ctx_len
    return pl.pallas_call(
        _final_norm_kernel,
        out_shape=jax.ShapeDtypeStruct((b, tok - ctx_len, d), F32),
        grid=(b, (tok - ctx_len) // tr),
        in_specs=[pl.BlockSpec((1, tr, d), lambda bi, t: (bi, t + 1, 0)),
                  pl.BlockSpec((1, d), lambda bi, t: (0, 0))],
        out_specs=pl.BlockSpec((1, tr, d), lambda bi, t: (bi, t, 0)),
        compiler_params=_params(("parallel", "parallel")),
        name="final_rmsnorm",
    )(x, w.reshape(1, d))


def _rope_tables(seq, ctx_len, width):
    t = jnp.arange(seq, dtype=jnp.int32)
    rows = (t // GRID_W).astype(F32)
    cols = (t % GRID_W).astype(F32)
    n_freq = DA_QK_DIM // 4
    inv_freq = ROPE_BASE ** (-jnp.arange(n_freq, dtype=F32) / n_freq)
    ar = rows[:, None] * inv_freq
    ac = cols[:, None] * inv_freq
    cos = jnp.concatenate([jnp.cos(ar), jnp.cos(ar), jnp.cos(ac), jnp.cos(ac)], axis=1)
    sin = jnp.concatenate([-jnp.sin(ar), jnp.sin(ar), -jnp.sin(ac), jnp.sin(ac)], axis=1)
    cos = jnp.concatenate([jnp.ones((ctx_len, DA_QK_DIM), F32), cos], axis=0)
    sin = jnp.concatenate([jnp.zeros((ctx_len, DA_QK_DIM), F32), sin], axis=0)
    reps = width // DA_QK_DIM
    return jnp.tile(cos, (1, reps)), jnp.tile(sin, (1, reps))


def kernel(x, c, ctx, c_ctx, w_mod, b_mod, norm_mix, norm_ffn, da_w_qkv, da_lambda, da_head_gain, da_w_o,
           gdn_w_in, gdn_conv, gdn_a_log, gdn_dt_bias, gdn_norm_gain, gdn_w_o, ffn_w_up, ffn_conv,
           ffn_w_down, final_norm):
    bsz, seq, d = x.shape
    ctx_len = ctx.shape[1]
    tm = (ctx_len + seq) // 3
    tn_wide = 1024
    tn_ffn = 512

    xa = jnp.concatenate([ctx, x], axis=1)
    pad_rows = 16 - (bsz + 1)
    cvec = jnp.concatenate([c, c_ctx[None, :], jnp.zeros((pad_rows, d), F32)], axis=0)
    mods_all = _mods_call(cvec, w_mod, b_mod)
    cos_t, sin_t = _rope_tables(seq, ctx_len, MXU_N)

    def sel(mods, idx):
        lat = mods[:bsz, idx * d:(idx + 1) * d]
        cx = jnp.broadcast_to(mods[bsz:bsz + 1, idx * d:(idx + 1) * d], (bsz, d))
        return lat, cx

    def modv(mods, i_shift, i_scale):
        sl, sc = sel(mods, i_shift)
        cl, cc = sel(mods, i_scale)
        return jnp.stack([sl, cl, sc, cc], axis=1)

    def gatev(mods, idx):
        gl, gc = sel(mods, idx)
        return jnp.stack([gl, gc], axis=1)

    for i in range(DEPTH):
        mods = mods_all[i]
        jm = i // 2
        mv = modv(mods, 0, 1)
        if i % 2 == 0:
            lam_init = 0.8 - 0.6 * math.exp(-0.3 * i)
            qkv = _proj_call(xa, mv, norm_mix[i], [da_w_qkv[jm].astype(BF16)], [], epi="rope", tn=tn_wide,
                             tm=tm, extras=(cos_t, sin_t), q_tiles=DA_QK_W // tn_wide, ctx_len=ctx_len)
            att = _diff_attn_call(qkv, da_lambda[jm], da_head_gain[jm], lam_init, ctx_len=ctx_len)
            xa = _oproj_call(att, da_w_o[jm].astype(BF16), xa, gatev(mods, 2), tm=tm, tn=512,
                             ctx_len=ctx_len)
        else:
            w_in = gdn_w_in[jm].astype(BF16)
            cw = gdn_conv[jm]
            qk = _proj_call(xa, mv, norm_mix[i], [w_in[:, :2 * GDN_QK_W]], [cw[:, :2 * GDN_QK_W]],
                            epi="gdn_qk", tn=tn_wide, tm=tm, q_tiles=GDN_QK_W // tn_wide, ctx_len=ctx_len)
            vv = _proj_call(xa, mv, norm_mix[i], [w_in[:, 2 * GDN_QK_W:GDN_QKV_W]],
                            [cw[:, 2 * GDN_QK_W:GDN_QKV_W]], epi="gdn_v", tn=tn_wide, tm=tm,
                            ctx_len=ctx_len)
            z = _proj_call(xa, mv, norm_mix[i], [w_in[:, GDN_QKV_W:GDN_QKV_W + GDN_V_W]], [],
                           epi="plain", tn=tn_wide, tm=tm, ctx_len=ctx_len)
            w_ab = w_in[:, GDN_QKV_W + GDN_V_W:].reshape(d, 2, 2 * GDN_V_HEADS)
            w_ab = jnp.concatenate([w_ab, jnp.zeros_like(w_ab)], axis=2).reshape(d, 2 * LANES)
            ab = _proj_call(xa, mv, norm_mix[i], [w_ab], [], epi="plain", tn=2 * LANES, tm=tm,
                            out_dtype=F32, ctx_len=ctx_len)
            zeros = jnp.zeros((2, GDN_V_HEADS), F32)
            alog = jnp.concatenate([zeros, gdn_a_log[jm], zeros, zeros], axis=1).reshape(2, 1, LANES)
            dtb = jnp.concatenate([zeros, gdn_dt_bias[jm], zeros, zeros], axis=1).reshape(2, 1, LANES)
            o = _gdn_scan_call(qk, vv, ab, alog, dtb, ctx_len=ctx_len)
            xa = _gdn_oproj_call(o, z, gdn_norm_gain[jm], gdn_w_o[jm].astype(BF16), xa, gatev(mods, 2),
                                 tm=tm // 2, tn=512, ctx_len=ctx_len)
        padc = D_FF_PAD - D_FF
        w_up = ffn_w_up[i].astype(BF16)
        w_g = jnp.pad(w_up[:, :D_FF], ((0, 0), (0, padc)))
        w_v = jnp.pad(w_up[:, D_FF:], ((0, 0), (0, padc)))
        cwf = ffn_conv[i]
        cw_g = jnp.pad(cwf[:, :D_FF], ((0, 0), (0, padc)))
        cw_v = jnp.pad(cwf[:, D_FF:], ((0, 0), (0, padc)))
        act = _proj_call(xa, modv(mods, 3, 4), norm_ffn[i], [w_g, w_v], [cw_g, cw_v], epi="ffn", tn=tn_ffn,
                         tm=tm, ctx_len=ctx_len)
        w_dn = jnp.pad(ffn_w_down[i].astype(BF16), ((0, padc), (0, 0)))
        xa = _oproj_call(act, w_dn, xa, gatev(mods, 5), tm=tm, tn=512, ctx_len=ctx_len)
    return _final_norm_call(xa, final_norm, ctx_len=ctx_len)
```

```python
import functools
import math

import jax
import jax.numpy as jnp
from jax import lax
from jax.experimental import pallas as pl
from jax.experimental.pallas import tpu as pltpu

F32 = jnp.float32
BF16 = jnp.bfloat16

D_MODEL = 2048
DEPTH = 4
CTX_LEN = 256
GRID_W = 64
EPS = 1e-6
DA_HEADS = 8
DA_QK_DIM = 128
DA_V_DIM = 256
DA_QK_W = DA_HEADS * 2 * DA_QK_DIM
DA_V_W = DA_HEADS * DA_V_DIM
ROPE_BASE = 10000.0
GDN_QK_HEADS = 16
GDN_V_HEADS = 32
GDN_DIM = 128
GDN_QK_W = GDN_QK_HEADS * GDN_DIM
GDN_V_W = GDN_V_HEADS * GDN_DIM
GDN_QKV_W = 2 * GDN_QK_W + GDN_V_W
GDN_CONV = 5
CHUNK = 64
D_FF = 5504
FFN_CONV = 3

LANES = 128
HALO = 16
D_FF_PAD = 5632
VMEM_LIMIT = 56 * 1024 * 1024

NT_DIMS = (((1,), (1,)), ((), ()))
TN_DIMS = (((0,), (0,)), ((), ()))


def _silu(x):
    return x * jax.nn.sigmoid(x)


def _params(sem):
    return pltpu.CompilerParams(dimension_semantics=sem, vmem_limit_bytes=VMEM_LIMIT)


def _mods_kernel(c_ref, w_ref, b_ref, o_ref):
    s = _silu(c_ref[...]).astype(BF16)
    w = w_ref[0].astype(BF16)
    o_ref[0] = jnp.dot(s, w, preferred_element_type=F32) + b_ref[0]


def _mods_call(cvec, w_mod, b_mod):
    depth, d, n = w_mod.shape
    tn = 1024
    return pl.pallas_call(
        _mods_kernel,
        out_shape=jax.ShapeDtypeStruct((depth, cvec.shape[0], n), F32),
        grid=(depth, n // tn),
        in_specs=[
            pl.BlockSpec((cvec.shape[0], d), lambda i, j: (0, 0)),
            pl.BlockSpec((1, d, tn), lambda i, j: (i, 0, j)),
            pl.BlockSpec((1, 1, tn), lambda i, j: (i, 0, j)),
        ],
        out_specs=pl.BlockSpec((1, cvec.shape[0], tn), lambda i, j: (i, 0, j)),
        compiler_params=_params(("parallel", "parallel")),
        name="adaln_mods",
    )(cvec, w_mod, b_mod.reshape(depth, 1, n))


MXU_N = 256
EPI_ROWS = 128
NORM_ROWS = 16


def _proj_kernel(*refs, n_w, taps, epi, tm, tn, ctx_len, tok, q_tiles):
    halo = taps > 0
    ns = tn // MXU_N
    it = iter(refs)
    x_ref = next(it)
    xp_ref = next(it) if halo else None
    xn_ref = next(it) if halo else None
    modv_ref = next(it)
    nw_ref = next(it)
    w_refs = [next(it) for _ in range(n_w)]
    cw_refs = [next(it) for _ in range(n_w)] if halo else []
    ex_refs = [next(it) for _ in range(2)] if epi == "rope" else []
    out_ref = next(it)
    h_ref = next(it)
    u_refs = [[next(it) for _ in range(ns)] for _ in range(n_w)]

    t = pl.program_id(1)
    j = pl.program_id(2)
    off = HALO if halo else 0
    p = taps // 2

    @pl.when(j == 0)
    def _():
        nw = nw_ref[...]
        gain_l = nw * (1.0 + modv_ref[0, 1:2])
        gain_c = nw * (1.0 + modv_ref[0, 3:4])
        shift_l = modv_ref[0, 0:1]
        shift_c = modv_ref[0, 2:3]

        def norm_rows(src_ref, n_rows, dst0, g0, keep):
            def body(i, carry):
                r = pl.multiple_of(i * NORM_ROWS, NORM_ROWS)
                xv = src_ref[0, pl.ds(r, NORM_ROWS), :]
                inv = lax.rsqrt(jnp.mean(xv * xv, axis=-1, keepdims=True) + EPS)
                is_ctx = (g0 + r + lax.broadcasted_iota(jnp.int32, (NORM_ROWS, 1), 0)) < ctx_len
                hv = (xv * inv) * jnp.where(is_ctx, gain_c, gain_l) + jnp.where(is_ctx, shift_c, shift_l)
                if keep is not None:
                    hv = jnp.where(keep, hv, 0.0)
                h_ref[pl.ds(dst0 + r, NORM_ROWS), :] = hv.astype(BF16)
                return carry
            lax.fori_loop(0, n_rows // NORM_ROWS, body, 0, unroll=2 if n_rows > NORM_ROWS else 1)

        norm_rows(x_ref, tm, off, t * tm, None)
        if halo:
            keep_prev = (t * tm != 0) & (t * tm != ctx_len)
            keep_next = (t * tm + tm != tok) & (t * tm + tm != ctx_len)
            norm_rows(xp_ref, HALO, 0, t * tm - HALO, keep_prev)
            norm_rows(xn_ref, HALO, off + tm, t * tm + tm, keep_next)

    for s in range(ns):
        for w, u_s in zip(w_refs, u_refs):
            u_s[s][...] = jnp.dot(h_ref[...], w[:, s * MXU_N:(s + 1) * MXU_N], preferred_element_type=F32)

    def conv_rows(u_ref, cw_ref, s, r0, nrows, edge):
        if not halo:
            return u_ref[r0:r0 + nrows, :]
        acc = None
        for jj in range(taps):
            dlt = jj - p
            sh = u_ref[off + r0 + dlt:off + r0 + dlt + nrows, :]
            if edge is not None and dlt != 0:
                ri = r0 + lax.broadcasted_iota(jnp.int32, (nrows, 1), 0)
                same = ((ri >= edge) & (ri + dlt >= edge)) | ((ri < edge) & (ri + dlt < edge))
                sh = jnp.where(same, sh, 0.0)
            term = sh * cw_ref[jj:jj + 1, s * MXU_N:(s + 1) * MXU_N]
            acc = term if acc is None else acc + term
        return acc

    def finish(s, r0, nrows, edge=None):
        vals = [conv_rows(u_s[s], cw, s, r0, nrows, edge)
                for u_s, cw in zip(u_refs, cw_refs if halo else [None] * n_w)]
        rs = slice(r0, r0 + nrows)
        cs = slice(s * MXU_N, (s + 1) * MXU_N)
        hpt = MXU_N // LANES
        if epi == "ffn":
            out_ref[0, rs, cs] = (_silu(vals[0]) * vals[1]).astype(out_ref.dtype)
        elif epi == "plain":
            out_ref[0, rs, cs] = vals[0].astype(out_ref.dtype)
        elif epi == "gdn_v":
            a = _silu(vals[0])
            for hh in range(hpt):
                out_ref[0, s * hpt + hh, rs] = a[:, hh * LANES:(hh + 1) * LANES].astype(out_ref.dtype)
        elif epi == "gdn_qk":
            a = _silu(vals[0])
            qs = jnp.where(j < q_tiles, GDN_DIM ** -0.5, 1.0).astype(F32)
            for hh in range(hpt):
                xs = a[:, hh * LANES:(hh + 1) * LANES]
                nrm = xs * lax.rsqrt(jnp.sum(xs * xs, axis=-1, keepdims=True) + EPS)
                out_ref[0, s * hpt + hh, rs] = (nrm * qs).astype(out_ref.dtype)
        elif epi == "rope":
            u = vals[0]
            cos = jnp.where(j < 2 * q_tiles, ex_refs[0][rs, :], 1.0)
            sin = jnp.where(j < 2 * q_tiles, ex_refs[1][rs, :], 0.0)
            lane = lax.broadcasted_iota(jnp.int32, (1, MXU_N), 1)
            swapped = jnp.where((lane % 64) < 32, pltpu.roll(u, MXU_N - 32, 1), pltpu.roll(u, 32, 1))
            sc = jnp.where(j < q_tiles, DA_QK_DIM ** -0.5 * LOG2E, 1.0).astype(F32)
            out_ref[0, rs, cs] = ((u * cos + swapped * sin) * sc).astype(out_ref.dtype)
        else:
            raise ValueError(epi)

    for s in range(ns):
        for r0 in range(0, tm, EPI_ROWS):
            finish(s, r0, EPI_ROWS)

    if halo and ctx_len % tm != 0:
        edge = ctx_len % tm

        @pl.when(t == ctx_len // tm)
        def _():
            for s in range(ns):
                finish(s, edge - HALO, 2 * HALO, edge)


def _proj_call(x, modv, nw, ws, cws, *, epi, tn, tm, out_dtype=BF16, extras=(), q_tiles=0,
               ctx_len=CTX_LEN):
    b, tok, d = x.shape
    n = ws[0].shape[1]
    taps = cws[0].shape[0] if cws else 0
    halo = taps > 0
    nt = tok // tm
    hb = tm // HALO
    nhb = tok // HALO

    in_specs = [pl.BlockSpec((1, tm, d), lambda bi, t, j: (bi, t, 0))]
    args = [x]
    if halo:
        in_specs += [
            pl.BlockSpec((1, HALO, d), lambda bi, t, j: (bi, jnp.maximum(t * hb - 1, 0), 0)),
            pl.BlockSpec((1, HALO, d), lambda bi, t, j: (bi, jnp.minimum((t + 1) * hb, nhb - 1), 0)),
        ]
        args += [x, x]
    in_specs += [pl.BlockSpec((1, 4, d), lambda bi, t, j: (bi, 0, 0)),
                 pl.BlockSpec((1, d), lambda bi, t, j: (0, 0))]
    args += [modv, nw.reshape(1, d)]
    for w in ws:
        in_specs.append(pl.BlockSpec((d, tn), lambda bi, t, j: (0, j)))
        args.append(w)
    for cw in cws:
        in_specs.append(pl.BlockSpec((taps, tn), lambda bi, t, j: (0, j)))
        args.append(cw)
    for e in extras:
        in_specs.append(pl.BlockSpec((tm, MXU_N), lambda bi, t, j: (t, 0)))
        args.append(e)

    if epi in ("gdn_v", "gdn_qk"):
        out_shape = jax.ShapeDtypeStruct((b, n // LANES, tok, LANES), out_dtype)
        out_spec = pl.BlockSpec((1, tn // LANES, tm, LANES), lambda bi, t, j: (bi, j, t, 0))
    else:
        out_shape = jax.ShapeDtypeStruct((b, tok, n), out_dtype)
        out_spec = pl.BlockSpec((1, tm, tn), lambda bi, t, j: (bi, t, j))

    kern = functools.partial(_proj_kernel, n_w=len(ws), taps=taps, epi=epi, tm=tm, tn=tn,
                             ctx_len=ctx_len, tok=tok, q_tiles=q_tiles)
    return pl.pallas_call(
        kern,
        out_shape=out_shape,
        grid=(b, nt, n // tn),
        in_specs=in_specs,
        out_specs=out_spec,
        scratch_shapes=([pltpu.VMEM((tm + (2 * HALO if halo else 0), d), BF16)]
                        + [pltpu.VMEM((tm + (2 * HALO if halo else 0), MXU_N), F32)
                           for _ in range(len(ws) * (tn // MXU_N))]),
        compiler_params=_params(("parallel", "parallel", "arbitrary")),
        name="norm_proj_" + epi,
    )(*args)


def _row_gate(g_ref, t, tm, ctx_len):
    g = t * tm + lax.broadcasted_iota(jnp.int32, (tm, 1), 0)
    return jnp.where(g < ctx_len, g_ref[0, 1:2], g_ref[0, 0:1])


def _oproj_kernel(y_ref, w_ref, x_ref, g_ref, o_ref, *, tm, ctx_len):
    acc = jnp.dot(y_ref[0], w_ref[...], preferred_element_type=F32)
    o_ref[0] = x_ref[0] + _row_gate(g_ref, pl.program_id(1), tm, ctx_len) * acc


def _oproj_call(y, w, x, gate, *, tm, tn, ctx_len=CTX_LEN):
    b, tok, d = x.shape
    k = y.shape[-1]
    return pl.pallas_call(
        functools.partial(_oproj_kernel, tm=tm, ctx_len=ctx_len),
        out_shape=jax.ShapeDtypeStruct(x.shape, F32),
        grid=(b, tok // tm, d // tn),
        in_specs=[
            pl.BlockSpec((1, tm, k), lambda bi, t, j: (bi, t, 0)),
            pl.BlockSpec((k, tn), lambda bi, t, j: (0, j)),
            pl.BlockSpec((1, tm, tn), lambda bi, t, j: (bi, t, j)),
            pl.BlockSpec((1, 2, tn), lambda bi, t, j: (bi, 0, j)),
        ],
        out_specs=pl.BlockSpec((1, tm, tn), lambda bi, t, j: (bi, t, j)),
        compiler_params=_params(("parallel", "parallel", "arbitrary")),
        name="out_proj_residual",
    )(y, w, x, gate)


def _gdn_gate_kernel(o_ref, z_ref, gain_ref, y_ref):
    gain = gain_ref[...]
    for h in range(GDN_V_HEADS):
        o = o_ref[0, 0, h].astype(F32) + o_ref[1, 0, h].astype(F32)
        y = o * lax.rsqrt(jnp.mean(o * o, axis=-1, keepdims=True) + EPS) * gain
        z = z_ref[0, :, h * LANES:(h + 1) * LANES].astype(F32)
        y_ref[0, :, h * LANES:(h + 1) * LANES] = (y * _silu(z)).astype(BF16)


def _gdn_gate_call(o, z, gain, *, tr):
    b, tok, k = z.shape
    return pl.pallas_call(
        _gdn_gate_kernel,
        out_shape=jax.ShapeDtypeStruct(z.shape, BF16),
        grid=(b, tok // tr),
        in_specs=[
            pl.BlockSpec((2, 1, GDN_V_HEADS, tr, LANES), lambda bi, t: (0, bi, 0, t, 0)),
            pl.BlockSpec((1, tr, k), lambda bi, t: (bi, t, 0)),
            pl.BlockSpec((1, LANES), lambda bi, t: (0, 0)),
        ],
        out_specs=pl.BlockSpec((1, tr, k), lambda bi, t: (bi, t, 0)),
        compiler_params=_params(("parallel", "parallel")),
        name="gdn_gated_norm",
    )(o, z, gain.reshape(1, LANES))


ATT_HEADS = 2
LOG2E = math.log2(math.e)


def _diff_attn_kernel(q_ref, k_ref, v_ref, lam_ref, gain_ref, o_ref, *, lam_init, ctx_len):
    lv = lam_ref[...]
    lam = (jnp.exp(jnp.sum(lv[0:1] * lv[1:2], axis=-1, keepdims=True))
           - jnp.exp(jnp.sum(lv[2:3] * lv[3:4], axis=-1, keepdims=True)) + lam_init)

    def attend(kk, vv):
        q = q_ref[0]
        scores = []
        for hc in range(2 * ATT_HEADS):
            cs = slice(hc * DA_QK_DIM, (hc + 1) * DA_QK_DIM)
            scores.append(lax.dot_general(q[:, cs], kk[:, cs], NT_DIMS, preferred_element_type=F32))
        for h in range(ATT_HEADS):
            ps, ls = [], []
            for c in range(2):
                s = scores[2 * h + c]
                p = jnp.exp2(s - jnp.max(s, axis=-1, keepdims=True))
                ps.append(p)
                ls.append(jnp.sum(p, axis=-1, keepdims=True))
            inv0 = 1.0 / ls[0]
            a = ps[0] - (lam * ls[0] * (1.0 / ls[1])) * ps[1]
            vs = slice(h * DA_V_DIM, (h + 1) * DA_V_DIM)
            o = jnp.dot(a.astype(BF16), vv[:, vs], preferred_element_type=F32) * inv0
            o = o * lax.rsqrt(jnp.mean(o * o, axis=-1, keepdims=True) + EPS) * gain_ref[...]
            o_ref[0, :, vs] = (o * (1.0 - lam_init)).astype(o_ref.dtype)

    t = pl.program_id(2)

    @pl.when(t == 0)
    def _():
        attend(k_ref[0, 0:ctx_len], v_ref[0, 0:ctx_len])

    @pl.when(t != 0)
    def _():
        attend(k_ref[0], v_ref[0])


def _diff_attn_call(qkv, lam_vecs, head_gain, lam_init, *, ctx_len=CTX_LEN):
    b, tok, _ = qkv.shape
    hw = ATT_HEADS * 2 * DA_QK_DIM
    ng = DA_HEADS // ATT_HEADS
    tq = ctx_len
    return pl.pallas_call(
        functools.partial(_diff_attn_kernel, lam_init=lam_init, ctx_len=ctx_len),
        out_shape=jax.ShapeDtypeStruct((b, tok, DA_V_W), BF16),
        grid=(b, ng, tok // tq),
        in_specs=[
            pl.BlockSpec((1, tq, hw), lambda bi, h, t: (bi, t, h)),
            pl.BlockSpec((1, tok, hw), lambda bi, h, t: (bi, 0, ng + h)),
            pl.BlockSpec((1, tok, hw), lambda bi, h, t: (bi, 0, 2 * ng + h)),
            pl.BlockSpec((4, DA_QK_DIM), lambda bi, h, t: (0, 0)),
            pl.BlockSpec((1, DA_V_DIM), lambda bi, h, t: (0, 0)),
        ],
        out_specs=pl.BlockSpec((1, tq, hw), lambda bi, h, t: (bi, t, h)),
        compiler_params=_params(("parallel", "parallel", "arbitrary")),
        name="diff_attention",
    )(qkv, qkv, qkv, lam_vecs, head_gain.reshape(1, DA_V_DIM))


def _split3(x):
    x1 = x.astype(BF16)
    r1 = x - x1.astype(F32)
    x2 = r1.astype(BF16)
    x3 = (r1 - x2.astype(F32)).astype(BF16)
    return x1, x2, x3


def _blockdiag(x2):
    first = lax.broadcasted_iota(jnp.int32, x2.shape, 1) < CHUNK
    return jnp.concatenate([jnp.where(first, x2, 0.0), jnp.where(first, 0.0, x2)], axis=0)


def _unit_lower_inverses(a2s, base):
    c = CHUNK
    nt = c // base
    assert base == 8
    lane = lax.broadcasted_iota(jnp.int32, (base, LANES), 1)
    sub = lax.broadcasted_iota(jnp.int32, (base, LANES), 0)
    row2 = lax.broadcasted_iota(jnp.int32, (c, LANES), 0)
    col2 = lax.broadcasted_iota(jnp.int32, (c, LANES), 1) % c

    in_blk = [((lane % c) // base) == r for r in range(nt)]
    dgs = []
    for a2 in a2s:
        dg = jnp.where(in_blk[0], a2[0:base], 0.0)
        for r in range(1, nt):
            dg = jnp.where(in_blk[r], a2[base * r:base * (r + 1)], dg)
        dgs.append(dg)
    ts = [jnp.where((lane % base) == sub, 1.0, 0.0).astype(F32) for _ in a2s]
    for jcol in range(base - 1):
        idx = (lane // base) * base + jcol
        for i, dg in enumerate(dgs):
            acol = jnp.take_along_axis(dg, idx, axis=1)
            ts[i] = ts[i] - acol * ts[i][jcol:jcol + 1, :]
    tbds = [_blockdiag(jnp.concatenate([jnp.where(in_blk[r], t, 0.0) for r in range(nt)], axis=0))
            for t in ts]

    size = base
    while size < c:
        off = ((row2 // (2 * size)) == (col2 // (2 * size))) & ((row2 // size) != (col2 // size))
        tbs = [tbd.astype(BF16) for tbd in tbds]
        tqs = [jnp.dot(tb, _blockdiag(jnp.where(off, a2, 0.0)).astype(BF16), preferred_element_type=F32)
               for tb, a2 in zip(tbs, a2s)]
        tbds = [tbd - jnp.dot(tq.astype(BF16), tb, preferred_element_type=F32)
                for tbd, tq, tb in zip(tbds, tqs, tbs)]
        size *= 2
    return tbds


def _gdn_scan_kernel(q_ref, k_ref, v_ref, ab_ref, alog_ref, dt_ref, o_ref,
                     s_ref, cg_ref, cb_ref, rg_ref, rb_ref, gt_ref, *, base):
    c = CHUNK
    d = pl.program_id(1)
    step = pl.program_id(2)
    rev = d == 1

    @pl.when(step == 0)
    def _():
        s_ref[...] = jnp.zeros(s_ref.shape, F32)

    ab = ab_ref[0]
    beta_all = jax.nn.sigmoid(ab)
    xs = ab + dt_ref[0]
    g_all = -jnp.exp(alog_ref[0]) * (jnp.maximum(xs, 0.0) + jnp.log1p(jnp.exp(-jnp.abs(xs))))
    ri = lax.broadcasted_iota(jnp.int32, (c, c), 0)
    ci = lax.broadcasted_iota(jnp.int32, (c, c), 1)
    sgn = 1 - 2 * d
    cum = jnp.where((ci - ri) * sgn <= 0, 1.0, 0.0).astype(BF16)
    gam_all = sum(jnp.dot(cum, part, preferred_element_type=F32) for part in _split3(g_all))
    gtot_all = jnp.sum(g_all, axis=0, keepdims=True)

    pi = lax.broadcasted_iota(jnp.int32, (GDN_QK_HEADS, LANES), 0)
    li = lax.broadcasted_iota(jnp.int32, (GDN_QK_HEADS, LANES), 1)
    zpad = jnp.zeros((c, LANES), BF16)

    def pair_rows(x, col0):
        sel_e = jnp.where(li == col0 + 2 * pi, 1.0, 0.0).astype(BF16)
        sel_o = jnp.where(li == col0 + 2 * pi + 1, 1.0, 0.0).astype(BF16)
        acc = jnp.zeros((GDN_QK_HEADS, LANES), F32)
        for part in _split3(x):
            acc = acc + lax.dot_general(sel_e, jnp.concatenate([part, zpad], axis=0), NT_DIMS,
                                        preferred_element_type=F32)
            acc = acc + lax.dot_general(sel_o, jnp.concatenate([zpad, part], axis=0), NT_DIMS,
                                        preferred_element_type=F32)
        return acc

    rg_ref[...] = pair_rows(gam_all, GDN_V_HEADS)
    rb_ref[...] = pair_rows(beta_all, 0)
    for h in range(GDN_V_HEADS):
        cg_ref[h] = jnp.broadcast_to(gam_all[:, GDN_V_HEADS + h:GDN_V_HEADS + h + 1], (c, LANES))
        cb_ref[h] = jnp.broadcast_to(beta_all[:, h:h + 1], (c, LANES))
        gt_ref[h:h + 1, :] = jnp.broadcast_to(gtot_all[:, GDN_V_HEADS + h:GDN_V_HEADS + h + 1], (1, LANES))

    row2 = lax.broadcasted_iota(jnp.int32, (c, LANES), 0)
    lane2 = lax.broadcasted_iota(jnp.int32, (c, LANES), 1)
    col2 = lane2 % c
    first2 = lane2 < c

    pairs = range(GDN_QK_HEADS)
    heads = range(GDN_V_HEADS)

    qkks = []
    for p in pairs:
        q = q_ref[0, p]
        k = k_ref[0, p]
        qkks.append(lax.dot_general(jnp.concatenate([q, k], axis=0), jnp.concatenate([k, k], axis=0),
                                    NT_DIMS, preferred_element_type=F32))
    a2s, qkm2s = [], []
    for p in pairs:
        gc2 = jnp.where(first2, cg_ref[2 * p], cg_ref[2 * p + 1])
        bc2 = jnp.where(first2, cb_ref[2 * p], cb_ref[2 * p + 1])
        gr2 = rg_ref[p:p + 1, :]
        br2 = rb_ref[p:p + 1, :]
        e2 = jnp.exp(-jnp.abs(gc2 - gr2))
        a2s.append(jnp.where(row2 > col2, jnp.where(rev, br2, bc2) * qkks[p][c:2 * c] * e2, 0.0))
        qkm2s.append(jnp.where((row2 - col2) * sgn >= 0, qkks[p][0:c] * e2, 0.0))

    tbds = _unit_lower_inverses(a2s, base)

    uws = []
    for p in pairs:
        kf = k_ref[0, p].astype(F32)
        rhs = []
        for h in (2 * p, 2 * p + 1):
            cb = cb_ref[h]
            vb = v_ref[0, h].astype(F32) * cb
            kb = kf * (cb * jnp.exp(cg_ref[h]))
            rhs.append(jnp.concatenate([vb, kb], axis=1).astype(BF16))
        tuse = jnp.where(rev, tbds[p].T, tbds[p]).astype(BF16)
        uws.append(jnp.dot(tuse, jnp.concatenate(rhs, axis=0), preferred_element_type=F32))

    ws_qs = []
    for h in heads:
        w = uws[h // 2][(h % 2) * c:(h % 2 + 1) * c, GDN_DIM:2 * GDN_DIM]
        qd = q_ref[0, h // 2].astype(F32) * jnp.exp(cg_ref[h])
        wq = jnp.concatenate([w.astype(BF16), qd.astype(BF16)], axis=0)
        ws_qs.append(jnp.dot(wq, s_ref[h].astype(BF16), preferred_element_type=F32))

    for h in heads:
        u = uws[h // 2][(h % 2) * c:(h % 2 + 1) * c, 0:GDN_DIM]
        v_new = (u - ws_qs[h][0:c]).astype(BF16)
        qkm = qkm2s[h // 2][:, (h % 2) * c:(h % 2 + 1) * c].astype(BF16)
        o = ws_qs[h][c:2 * c] + jnp.dot(qkm, v_new, preferred_element_type=F32)
        o_ref[0, 0, h] = o.astype(o_ref.dtype)
        gt = gt_ref[h:h + 1, :]
        k_dec = (k_ref[0, h // 2].astype(F32) * jnp.exp(gt - cg_ref[h])).astype(BF16)
        s_ref[h] = s_ref[h] * jnp.exp(gt) + lax.dot_general(k_dec, v_new, TN_DIMS,
                                                           preferred_element_type=F32)


def _gdn_scan_call(qk, v, ab, alog, dt, *, ctx_len=CTX_LEN, base=8):
    b, _, tok, _ = qk.shape
    nc = tok // CHUNK
    ncc = ctx_len // CHUNK

    def chunk(dd, s):
        back = jnp.where(s < ncc, ncc - 1 - s, nc - 1 - (s - ncc))
        return jnp.where(dd == 0, s, back)

    return pl.pallas_call(
        functools.partial(_gdn_scan_kernel, base=base),
        out_shape=jax.ShapeDtypeStruct((2, b, GDN_V_HEADS, tok, LANES), BF16),
        grid=(b, 2, nc),
        in_specs=[
            pl.BlockSpec((1, GDN_QK_HEADS, CHUNK, LANES), lambda bi, dd, s: (bi, 0, chunk(dd, s), 0)),
            pl.BlockSpec((1, GDN_QK_HEADS, CHUNK, LANES), lambda bi, dd, s: (bi, 1, chunk(dd, s), 0)),
            pl.BlockSpec((1, GDN_V_HEADS, CHUNK, LANES), lambda bi, dd, s: (bi, 0, chunk(dd, s), 0)),
            pl.BlockSpec((1, CHUNK, LANES), lambda bi, dd, s: (bi, chunk(dd, s), dd)),
            pl.BlockSpec((1, 1, LANES), lambda bi, dd, s: (dd, 0, 0)),
            pl.BlockSpec((1, 1, LANES), lambda bi, dd, s: (dd, 0, 0)),
        ],
        out_specs=pl.BlockSpec((1, 1, GDN_V_HEADS, CHUNK, LANES),
                               lambda bi, dd, s: (dd, bi, 0, chunk(dd, s), 0)),
        scratch_shapes=[
            pltpu.VMEM((GDN_V_HEADS, GDN_DIM, GDN_DIM), F32),
            pltpu.VMEM((GDN_V_HEADS, CHUNK, LANES), F32),
            pltpu.VMEM((GDN_V_HEADS, CHUNK, LANES), F32),
            pltpu.VMEM((GDN_QK_HEADS, LANES), F32),
            pltpu.VMEM((GDN_QK_HEADS, LANES), F32),
            pltpu.VMEM((GDN_V_HEADS, LANES), F32),
        ],
        compiler_params=_params(("parallel", "parallel", "arbitrary")),
        name="gdn_chunk_scan",
    )(qk, qk, v, ab, alog, dt)


def _final_norm_kernel(x_ref, w_ref, o_ref):
    xv = x_ref[0]
    o_ref[0] = xv * lax.rsqrt(jnp.mean(xv * xv, axis=-1, keepdims=True) + EPS) * w_ref[...]


def _final_norm_call(x, w, *, ctx_len=CTX_LEN):
    b, tok, d = x.shape
    tr = ctx_len
    return pl.pallas_call(
        _final_norm_kernel,
        out_shape=jax.ShapeDtypeStruct((b, tok - ctx_len, d), F32),
        grid=(b, (tok - ctx_len) // tr),
        in_specs=[pl.BlockSpec((1, tr, d), lambda bi, t: (bi, t + 1, 0)),
                  pl.BlockSpec((1, d), lambda bi, t: (0, 0))],
        out_specs=pl.BlockSpec((1, tr, d), lambda bi, t: (bi, t, 0)),
        compiler_params=_params(("parallel", "parallel")),
        name="final_rmsnorm",
    )(x, w.reshape(1, d))


def _rope_tables(seq, ctx_len, width):
    t = jnp.arange(seq, dtype=jnp.int32)
    rows = (t // GRID_W).astype(F32)
    cols = (t % GRID_W).astype(F32)
    n_freq = DA_QK_DIM // 4
    inv_freq = ROPE_BASE ** (-jnp.arange(n_freq, dtype=F32) / n_freq)
    ar = rows[:, None] * inv_freq
    ac = cols[:, None] * inv_freq
    cos = jnp.concatenate([jnp.cos(ar), jnp.cos(ar), jnp.cos(ac), jnp.cos(ac)], axis=1)
    sin = jnp.concatenate([-jnp.sin(ar), jnp.sin(ar), -jnp.sin(ac), jnp.sin(ac)], axis=1)
    cos = jnp.concatenate([jnp.ones((ctx_len, DA_QK_DIM), F32), cos], axis=0)
    sin = jnp.concatenate([jnp.zeros((ctx_len, DA_QK_DIM), F32), sin], axis=0)
    reps = width // DA_QK_DIM
    return jnp.tile(cos, (1, reps)), jnp.tile(sin, (1, reps))


def kernel(x, c, ctx, c_ctx, w_mod, b_mod, norm_mix, norm_ffn, da_w_qkv, da_lambda, da_head_gain, da_w_o,
           gdn_w_in, gdn_conv, gdn_a_log, gdn_dt_bias, gdn_norm_gain, gdn_w_o, ffn_w_up, ffn_conv,
           ffn_w_down, final_norm):
    bsz, seq, d = x.shape
    ctx_len = ctx.shape[1]
    tm = (ctx_len + seq) // 3
    tn_wide = 1024
    tn_ffn = 512

    xa = jnp.concatenate([ctx, x], axis=1)
    pad_rows = 16 - (bsz + 1)
    cvec = jnp.concatenate([c, c_ctx[None, :], jnp.zeros((pad_rows, d), F32)], axis=0)
    mods_all = _mods_call(cvec, w_mod, b_mod)
    cos_t, sin_t = _rope_tables(seq, ctx_len, MXU_N)

    def sel(mods, idx):
        lat = mods[:bsz, idx * d:(idx + 1) * d]
        cx = jnp.broadcast_to(mods[bsz:bsz + 1, idx * d:(idx + 1) * d], (bsz, d))
        return lat, cx

    def modv(mods, i_shift, i_scale):
        sl, sc = sel(mods, i_shift)
        cl, cc = sel(mods, i_scale)
        return jnp.stack([sl, cl, sc, cc], axis=1)

    def gatev(mods, idx):
        gl, gc = sel(mods, idx)
        return jnp.stack([gl, gc], axis=1)

    for i in range(DEPTH):
        mods = mods_all[i]
        jm = i // 2
        mv = modv(mods, 0, 1)
        if i % 2 == 0:
            lam_init = 0.8 - 0.6 * math.exp(-0.3 * i)
            qkv = _proj_call(xa, mv, norm_mix[i], [da_w_qkv[jm].astype(BF16)], [], epi="rope", tn=tn_wide,
                             tm=tm, extras=(cos_t, sin_t), q_tiles=DA_QK_W // tn_wide, ctx_len=ctx_len)
            att = _diff_attn_call(qkv, da_lambda[jm], da_head_gain[jm], lam_init, ctx_len=ctx_len)
            xa = _oproj_call(att, da_w_o[jm].astype(BF16), xa, gatev(mods, 2), tm=tm, tn=512,
                             ctx_len=ctx_len)
        else:
            w_in = gdn_w_in[jm].astype(BF16)
            cw = gdn_conv[jm]
            qk = _proj_call(xa, mv, norm_mix[i], [w_in[:, :2 * GDN_QK_W]], [cw[:, :2 * GDN_QK_W]],
                            epi="gdn_qk", tn=tn_wide, tm=tm, q_tiles=GDN_QK_W // tn_wide, ctx_len=ctx_len)
            vv = _proj_call(xa, mv, norm_mix[i], [w_in[:, 2 * GDN_QK_W:GDN_QKV_W]],
                            [cw[:, 2 * GDN_QK_W:GDN_QKV_W]], epi="gdn_v", tn=tn_wide, tm=tm,
                            ctx_len=ctx_len)
            z = _proj_call(xa, mv, norm_mix[i], [w_in[:, GDN_QKV_W:GDN_QKV_W + GDN_V_W]], [],
                           epi="plain", tn=tn_wide, tm=tm, ctx_len=ctx_len)
            w_ab = w_in[:, GDN_QKV_W + GDN_V_W:].reshape(d, 2, 2 * GDN_V_HEADS)
            w_ab = jnp.concatenate([w_ab, jnp.zeros_like(w_ab)], axis=2).reshape(d, 2 * LANES)
            ab = _proj_call(xa, mv, norm_mix[i], [w_ab], [], epi="plain", tn=2 * LANES, tm=tm,
                            out_dtype=F32, ctx_len=ctx_len)
            zeros = jnp.zeros((2, GDN_V_HEADS), F32)
            alog = jnp.concatenate([zeros, gdn_a_log[jm], zeros, zeros], axis=1).reshape(2, 1, LANES)
            dtb = jnp.concatenate([zeros, gdn_dt_bias[jm], zeros, zeros], axis=1).reshape(2, 1, LANES)
            o = _gdn_scan_call(qk, vv, ab, alog, dtb, ctx_len=ctx_len)
            y = _gdn_gate_call(o, z, gdn_norm_gain[jm], tr=ctx_len)
            xa = _oproj_call(y, gdn_w_o[jm].astype(BF16), xa, gatev(mods, 2), tm=tm, tn=512,
                             ctx_len=ctx_len)
        padc = D_FF_PAD - D_FF
        w_up = ffn_w_up[i].astype(BF16)
        w_g = jnp.pad(w_up[:, :D_FF], ((0, 0), (0, padc)))
        w_v = jnp.pad(w_up[:, D_FF:], ((0, 0), (0, padc)))
        cwf = ffn_conv[i]
        cw_g = jnp.pad(cwf[:, :D_FF], ((0, 0), (0, padc)))
        cw_v = jnp.pad(cwf[:, D_FF:], ((0, 0), (0, padc)))
        act = _proj_call(xa, modv(mods, 3, 4), norm_ffn[i], [w_g, w_v], [cw_g, cw_v], epi="ffn", tn=tn_ffn,
                         tm=tm, ctx_len=ctx_len)
        w_dn = jnp.pad(ffn_w_down[i].astype(BF16), ((0, padc), (0, 0)))
        xa = _oproj_call(act, w_dn, xa, gatev(mods, 5), tm=tm, tn=512, ctx_len=ctx_len)
    return _final_norm_call(xa, final_norm, ctx_len=ctx_len)
```

```python
import functools
import math

import jax
import jax.numpy as jnp
from jax import lax
from jax.experimental import pallas as pl
from jax.experimental.pallas import tpu as pltpu

F32 = jnp.float32
BF16 = jnp.bfloat16

D_MODEL = 2048
DEPTH = 4
CTX_LEN = 256
GRID_W = 64
EPS = 1e-6
DA_HEADS = 8
DA_QK_DIM = 128
DA_V_DIM = 256
DA_QK_W = DA_HEADS * 2 * DA_QK_DIM
DA_V_W = DA_HEADS * DA_V_DIM
ROPE_BASE = 10000.0
GDN_QK_HEADS = 16
GDN_V_HEADS = 32
GDN_DIM = 128
GDN_QK_W = GDN_QK_HEADS * GDN_DIM
GDN_V_W = GDN_V_HEADS * GDN_DIM
GDN_QKV_W = 2 * GDN_QK_W + GDN_V_W
GDN_CONV = 5
CHUNK = 64
D_FF = 5504
FFN_CONV = 3

LANES = 128
HALO = 16
D_FF_PAD = 5632
VMEM_LIMIT = 56 * 1024 * 1024

NT_DIMS = (((1,), (1,)), ((), ()))
TN_DIMS = (((0,), (0,)), ((), ()))


def _silu(x):
    return x * jax.nn.sigmoid(x)


def _params(sem):
    return pltpu.CompilerParams(dimension_semantics=sem, vmem_limit_bytes=VMEM_LIMIT)


def _mods_kernel(c_ref, w_ref, b_ref, o_ref):
    s = _silu(c_ref[...]).astype(BF16)
    w = w_ref[0].astype(BF16)
    o_ref[0] = jnp.dot(s, w, preferred_element_type=F32) + b_ref[0]


def _mods_call(cvec, w_mod, b_mod):
    depth, d, n = w_mod.shape
    tn = 1024
    return pl.pallas_call(
        _mods_kernel,
        out_shape=jax.ShapeDtypeStruct((depth, cvec.shape[0], n), F32),
        grid=(depth, n // tn),
        in_specs=[
            pl.BlockSpec((cvec.shape[0], d), lambda i, j: (0, 0)),
            pl.BlockSpec((1, d, tn), lambda i, j: (i, 0, j)),
            pl.BlockSpec((1, 1, tn), lambda i, j: (i, 0, j)),
        ],
        out_specs=pl.BlockSpec((1, cvec.shape[0], tn), lambda i, j: (i, 0, j)),
        compiler_params=_params(("parallel", "parallel")),
        name="adaln_mods",
    )(cvec, w_mod, b_mod.reshape(depth, 1, n))


MXU_N = 256
EPI_ROWS = 128
NORM_ROWS = 16


def _proj_kernel(*refs, n_w, taps, epi, tm, tn, nj, ctx_len, tok, q_tiles):
    halo = taps > 0
    ns = tn // MXU_N
    it = iter(refs)
    x_ref = next(it)
    xp_ref = next(it) if halo else None
    xn_ref = next(it) if halo else None
    modv_ref = next(it)
    nw_ref = next(it)
    w_refs = [next(it) for _ in range(n_w)]
    cw_refs = [next(it) for _ in range(n_w)] if halo else []
    ex_refs = [next(it) for _ in range(2)] if epi == "rope" else []
    out_ref = next(it)
    h_ref = next(it)
    u_bufs = [[[next(it) for _ in range(ns)] for _ in range(n_w)] for _ in range(2)]

    t = pl.program_id(1)
    j = pl.program_id(2)
    off = HALO if halo else 0
    p = taps // 2

    def norm_phase():
        nw = nw_ref[...]
        gain_l = nw * (1.0 + modv_ref[0, 1:2])
        gain_c = nw * (1.0 + modv_ref[0, 3:4])
        shift_l = modv_ref[0, 0:1]
        shift_c = modv_ref[0, 2:3]

        def norm_rows(src_ref, n_rows, dst0, g0, keep):
            def body(i, carry):
                r = pl.multiple_of(i * NORM_ROWS, NORM_ROWS)
                xv = src_ref[0, pl.ds(r, NORM_ROWS), :]
                inv = lax.rsqrt(jnp.mean(xv * xv, axis=-1, keepdims=True) + EPS)
                is_ctx = (g0 + r + lax.broadcasted_iota(jnp.int32, (NORM_ROWS, 1), 0)) < ctx_len
                hv = (xv * inv) * jnp.where(is_ctx, gain_c, gain_l) + jnp.where(is_ctx, shift_c, shift_l)
                if keep is not None:
                    hv = jnp.where(keep, hv, 0.0)
                h_ref[pl.ds(dst0 + r, NORM_ROWS), :] = hv.astype(BF16)
                return carry
            lax.fori_loop(0, n_rows // NORM_ROWS, body, 0, unroll=2 if n_rows > NORM_ROWS else 1)

        norm_rows(x_ref, tm, off, t * tm, None)
        if halo:
            keep_prev = (t * tm != 0) & (t * tm != ctx_len)
            keep_next = (t * tm + tm != tok) & (t * tm + tm != ctx_len)
            norm_rows(xp_ref, HALO, 0, t * tm - HALO, keep_prev)
            norm_rows(xn_ref, HALO, off + tm, t * tm + tm, keep_next)

    def dots(u_refs):
        for s in range(ns):
            for w, u_s in zip(w_refs, u_refs):
                u_s[s][...] = jnp.dot(h_ref[...], w[:, s * MXU_N:(s + 1) * MXU_N],
                                      preferred_element_type=F32)

    def conv_rows(u_ref, cw_ref, s, r0, nrows, edge):
        if not halo:
            return u_ref[r0:r0 + nrows, :]
        acc = None
        for jj in range(taps):
            dlt = jj - p
            sh = u_ref[off + r0 + dlt:off + r0 + dlt + nrows, :]
            if edge is not None and dlt != 0:
                ri = r0 + lax.broadcasted_iota(jnp.int32, (nrows, 1), 0)
                same = ((ri >= edge) & (ri + dlt >= edge)) | ((ri < edge) & (ri + dlt < edge))
                sh = jnp.where(same, sh, 0.0)
            term = sh * cw_ref[jj:jj + 1, s * MXU_N:(s + 1) * MXU_N]
            acc = term if acc is None else acc + term
        return acc

    def finish(u_refs, jt, s, r0, nrows, edge=None):
        vals = [conv_rows(u_s[s], cw, s, r0, nrows, edge)
                for u_s, cw in zip(u_refs, cw_refs if halo else [None] * n_w)]
        rs = slice(r0, r0 + nrows)
        cs = slice(s * MXU_N, (s + 1) * MXU_N)
        hpt = MXU_N // LANES
        if epi == "ffn":
            out_ref[0, rs, cs] = (_silu(vals[0]) * vals[1]).astype(out_ref.dtype)
        elif epi == "plain":
            out_ref[0, rs, cs] = vals[0].astype(out_ref.dtype)
        elif epi == "gdn_v":
            a = _silu(vals[0])
            for hh in range(hpt):
                out_ref[0, s * hpt + hh, rs] = a[:, hh * LANES:(hh + 1) * LANES].astype(out_ref.dtype)
        elif epi == "gdn_qk":
            a = _silu(vals[0])
            qs = jnp.where(jt < q_tiles, GDN_DIM ** -0.5, 1.0).astype(F32)
            for hh in range(hpt):
                xs = a[:, hh * LANES:(hh + 1) * LANES]
                nrm = xs * lax.rsqrt(jnp.sum(xs * xs, axis=-1, keepdims=True) + EPS)
                out_ref[0, s * hpt + hh, rs] = (nrm * qs).astype(out_ref.dtype)
        elif epi == "rope":
            u = vals[0]
            cos = jnp.where(jt < 2 * q_tiles, ex_refs[0][rs, :], 1.0)
            sin = jnp.where(jt < 2 * q_tiles, ex_refs[1][rs, :], 0.0)
            lane = lax.broadcasted_iota(jnp.int32, (1, MXU_N), 1)
            swapped = jnp.where((lane % 64) < 32, pltpu.roll(u, MXU_N - 32, 1), pltpu.roll(u, 32, 1))
            sc = jnp.where(jt < q_tiles, DA_QK_DIM ** -0.5 * LOG2E, 1.0).astype(F32)
            out_ref[0, rs, cs] = ((u * cos + swapped * sin) * sc).astype(out_ref.dtype)
        else:
            raise ValueError(epi)

    def epilogue(u_refs, jt, u_next=None):
        chunks = [(s, r0) for s in range(ns) for r0 in range(0, tm, EPI_ROWS)]
        sub_dots = [(s, wi) for s in range(ns) for wi in range(n_w)] if u_next is not None else []
        per_dot = -(-len(chunks) // max(len(sub_dots), 1))
        for s, wi in sub_dots:
            u_next[wi][s][...] = jnp.dot(h_ref[...], w_refs[wi][:, s * MXU_N:(s + 1) * MXU_N],
                                         preferred_element_type=F32)
            for cs_, cr0 in chunks[:per_dot]:
                finish(u_refs, jt, cs_, cr0, EPI_ROWS)
            chunks = chunks[per_dot:]
        for cs_, cr0 in chunks:
            finish(u_refs, jt, cs_, cr0, EPI_ROWS)
        if halo and ctx_len % tm != 0:
            edge = ctx_len % tm

            @pl.when(t == ctx_len // tm)
            def _():
                for s in range(ns):
                    finish(u_refs, jt, s, edge - HALO, 2 * HALO, edge)

    @pl.when(j == 0)
    def _():
        norm_phase()
        dots(u_bufs[0])

    for par in range(2):
        @pl.when((j > 0) & (j < nj) & ((j & 1) == par))
        def _():
            epilogue(u_bufs[1 - par], j - 1, u_next=u_bufs[par])

    @pl.when(j == nj)
    def _():
        epilogue(u_bufs[(nj - 1) % 2], nj - 1)


def _proj_call(x, modv, nw, ws, cws, *, epi, tn, tm, out_dtype=BF16, extras=(), q_tiles=0,
               ctx_len=CTX_LEN):
    b, tok, d = x.shape
    n = ws[0].shape[1]
    taps = cws[0].shape[0] if cws else 0
    halo = taps > 0
    nt = tok // tm
    hb = tm // HALO
    nhb = tok // HALO

    in_specs = [pl.BlockSpec((1, tm, d), lambda bi, t, j: (bi, t, 0))]
    args = [x]
    if halo:
        in_specs += [
            pl.BlockSpec((1, HALO, d), lambda bi, t, j: (bi, jnp.maximum(t * hb - 1, 0), 0)),
            pl.BlockSpec((1, HALO, d), lambda bi, t, j: (bi, jnp.minimum((t + 1) * hb, nhb - 1), 0)),
        ]
        args += [x, x]
    in_specs += [pl.BlockSpec((1, 4, d), lambda bi, t, j: (bi, 0, 0)),
                 pl.BlockSpec((1, d), lambda bi, t, j: (0, 0))]
    args += [modv, nw.reshape(1, d)]
    nj = n // tn

    def mm_tile(j):
        return jnp.minimum(j, nj - 1)

    def epi_tile(j):
        return jnp.maximum(j - 1, 0)

    for w in ws:
        in_specs.append(pl.BlockSpec((d, tn), lambda bi, t, j: (0, mm_tile(j))))
        args.append(w)
    for cw in cws:
        in_specs.append(pl.BlockSpec((taps, tn), lambda bi, t, j: (0, epi_tile(j))))
        args.append(cw)
    for e in extras:
        in_specs.append(pl.BlockSpec((tm, MXU_N), lambda bi, t, j: (t, 0)))
        args.append(e)

    if epi in ("gdn_v", "gdn_qk"):
        out_shape = jax.ShapeDtypeStruct((b, n // LANES, tok, LANES), out_dtype)
        out_spec = pl.BlockSpec((1, tn // LANES, tm, LANES), lambda bi, t, j: (bi, epi_tile(j), t, 0))
    else:
        out_shape = jax.ShapeDtypeStruct((b, tok, n), out_dtype)
        out_spec = pl.BlockSpec((1, tm, tn), lambda bi, t, j: (bi, t, epi_tile(j)))

    kern = functools.partial(_proj_kernel, n_w=len(ws), taps=taps, epi=epi, tm=tm, tn=tn, nj=nj,
                             ctx_len=ctx_len, tok=tok, q_tiles=q_tiles)
    return pl.pallas_call(
        kern,
        out_shape=out_shape,
        grid=(b, nt, nj + 1),
        in_specs=in_specs,
        out_specs=out_spec,
        scratch_shapes=([pltpu.VMEM((tm + (2 * HALO if halo else 0), d), BF16)]
                        + [pltpu.VMEM((tm + (2 * HALO if halo else 0), MXU_N), F32)
                           for _ in range(2 * len(ws) * (tn // MXU_N))]),
        compiler_params=_params(("parallel", "parallel", "arbitrary")),
        name="norm_proj_" + epi,
    )(*args)


def _row_gate(g_ref, t, tm, ctx_len):
    g = t * tm + lax.broadcasted_iota(jnp.int32, (tm, 1), 0)
    return jnp.where(g < ctx_len, g_ref[0, 1:2], g_ref[0, 0:1])


def _oproj_kernel(y_ref, w_ref, x_ref, g_ref, o_ref, *, tm, ctx_len):
    acc = jnp.dot(y_ref[0], w_ref[...], preferred_element_type=F32)
    o_ref[0] = x_ref[0] + _row_gate(g_ref, pl.program_id(1), tm, ctx_len) * acc


def _oproj_call(y, w, x, gate, *, tm, tn, ctx_len=CTX_LEN):
    b, tok, d = x.shape
    k = y.shape[-1]
    return pl.pallas_call(
        functools.partial(_oproj_kernel, tm=tm, ctx_len=ctx_len),
        out_shape=jax.ShapeDtypeStruct(x.shape, F32),
        grid=(b, tok // tm, d // tn),
        in_specs=[
            pl.BlockSpec((1, tm, k), lambda bi, t, j: (bi, t, 0)),
            pl.BlockSpec((k, tn), lambda bi, t, j: (0, j)),
            pl.BlockSpec((1, tm, tn), lambda bi, t, j: (bi, t, j)),
            pl.BlockSpec((1, 2, tn), lambda bi, t, j: (bi, 0, j)),
        ],
        out_specs=pl.BlockSpec((1, tm, tn), lambda bi, t, j: (bi, t, j)),
        compiler_params=_params(("parallel", "parallel", "arbitrary")),
        name="out_proj_residual",
    )(y, w, x, gate)


def _gdn_gate_kernel(o_ref, z_ref, gain_ref, y_ref):
    gain = gain_ref[...]
    for h in range(GDN_V_HEADS):
        o = o_ref[0, 0, h].astype(F32) + o_ref[1, 0, h].astype(F32)
        y = o * lax.rsqrt(jnp.mean(o * o, axis=-1, keepdims=True) + EPS) * gain
        z = z_ref[0, :, h * LANES:(h + 1) * LANES].astype(F32)
        y_ref[0, :, h * LANES:(h + 1) * LANES] = (y * _silu(z)).astype(BF16)


def _gdn_gate_call(o, z, gain, *, tr):
    b, tok, k = z.shape
    return pl.pallas_call(
        _gdn_gate_kernel,
        out_shape=jax.ShapeDtypeStruct(z.shape, BF16),
        grid=(b, tok // tr),
        in_specs=[
            pl.BlockSpec((2, 1, GDN_V_HEADS, tr, LANES), lambda bi, t: (0, bi, 0, t, 0)),
            pl.BlockSpec((1, tr, k), lambda bi, t: (bi, t, 0)),
            pl.BlockSpec((1, LANES), lambda bi, t: (0, 0)),
        ],
        out_specs=pl.BlockSpec((1, tr, k), lambda bi, t: (bi, t, 0)),
        compiler_params=_params(("parallel", "parallel")),
        name="gdn_gated_norm",
    )(o, z, gain.reshape(1, LANES))


ATT_HEADS = 2
LOG2E = math.log2(math.e)


def _diff_attn_kernel(q_ref, k_ref, v_ref, lam_ref, gain_ref, o_ref, *, lam_init, ctx_len):
    lv = lam_ref[...]
    lam = (jnp.exp(jnp.sum(lv[0:1] * lv[1:2], axis=-1, keepdims=True))
           - jnp.exp(jnp.sum(lv[2:3] * lv[3:4], axis=-1, keepdims=True)) + lam_init)

    def attend(kk, vv):
        q = q_ref[0]
        scores = []
        for hc in range(2 * ATT_HEADS):
            cs = slice(hc * DA_QK_DIM, (hc + 1) * DA_QK_DIM)
            scores.append(lax.dot_general(q[:, cs], kk[:, cs], NT_DIMS, preferred_element_type=F32))
        for h in range(ATT_HEADS):
            ps, ls = [], []
            for c in range(2):
                s = scores[2 * h + c]
                p = jnp.exp2(s - jnp.max(s, axis=-1, keepdims=True))
                ps.append(p)
                ls.append(jnp.sum(p, axis=-1, keepdims=True))
            inv0 = 1.0 / ls[0]
            a = ps[0] - (lam * ls[0] * (1.0 / ls[1])) * ps[1]
            vs = slice(h * DA_V_DIM, (h + 1) * DA_V_DIM)
            o = jnp.dot(a.astype(BF16), vv[:, vs], preferred_element_type=F32) * inv0
            o = o * lax.rsqrt(jnp.mean(o * o, axis=-1, keepdims=True) + EPS) * gain_ref[...]
            o_ref[0, :, vs] = (o * (1.0 - lam_init)).astype(o_ref.dtype)

    t = pl.program_id(2)

    @pl.when(t == 0)
    def _():
        attend(k_ref[0, 0:ctx_len], v_ref[0, 0:ctx_len])

    @pl.when(t != 0)
    def _():
        attend(k_ref[0], v_ref[0])


def _diff_attn_call(qkv, lam_vecs, head_gain, lam_init, *, ctx_len=CTX_LEN):
    b, tok, _ = qkv.shape
    hw = ATT_HEADS * 2 * DA_QK_DIM
    ng = DA_HEADS // ATT_HEADS
    tq = ctx_len
    return pl.pallas_call(
        functools.partial(_diff_attn_kernel, lam_init=lam_init, ctx_len=ctx_len),
        out_shape=jax.ShapeDtypeStruct((b, tok, DA_V_W), BF16),
        grid=(b, ng, tok // tq),
        in_specs=[
            pl.BlockSpec((1, tq, hw), lambda bi, h, t: (bi, t, h)),
            pl.BlockSpec((1, tok, hw), lambda bi, h, t: (bi, 0, ng + h)),
            pl.BlockSpec((1, tok, hw), lambda bi, h, t: (bi, 0, 2 * ng + h)),
            pl.BlockSpec((4, DA_QK_DIM), lambda bi, h, t: (0, 0)),
            pl.BlockSpec((1, DA_V_DIM), lambda bi, h, t: (0, 0)),
        ],
        out_specs=pl.BlockSpec((1, tq, hw), lambda bi, h, t: (bi, t, h)),
        compiler_params=_params(("parallel", "parallel", "arbitrary")),
        name="diff_attention",
    )(qkv, qkv, qkv, lam_vecs, head_gain.reshape(1, DA_V_DIM))


PAIR_GROUP = 16


def _split3(x):
    x1 = x.astype(BF16)
    r1 = x - x1.astype(F32)
    x2 = r1.astype(BF16)
    x3 = (r1 - x2.astype(F32)).astype(BF16)
    return x1, x2, x3


def _blockdiag(x2):
    first = lax.broadcasted_iota(jnp.int32, x2.shape, 1) < CHUNK
    return jnp.concatenate([jnp.where(first, x2, 0.0), jnp.where(first, 0.0, x2)], axis=0)


def _unit_lower_inverses(a2s, base):
    c = CHUNK
    nt = c // base
    assert base == 8
    lane = lax.broadcasted_iota(jnp.int32, (base, LANES), 1)
    sub = lax.broadcasted_iota(jnp.int32, (base, LANES), 0)
    row2 = lax.broadcasted_iota(jnp.int32, (c, LANES), 0)
    col2 = lax.broadcasted_iota(jnp.int32, (c, LANES), 1) % c

    in_blk = [((lane % c) // base) == r for r in range(nt)]
    dgs = []
    for a2 in a2s:
        dg = jnp.where(in_blk[0], a2[0:base], 0.0)
        for r in range(1, nt):
            dg = jnp.where(in_blk[r], a2[base * r:base * (r + 1)], dg)
        dgs.append(dg)
    ts = [jnp.where((lane % base) == sub, 1.0, 0.0).astype(F32) for _ in a2s]
    for jcol in range(base - 1):
        idx = (lane // base) * base + jcol
        for i, dg in enumerate(dgs):
            acol = jnp.take_along_axis(dg, idx, axis=1)
            ts[i] = ts[i] - acol * ts[i][jcol:jcol + 1, :]
    tbds = [_blockdiag(jnp.concatenate([jnp.where(in_blk[r], t, 0.0) for r in range(nt)], axis=0))
            for t in ts]

    size = base
    while size < c:
        off = ((row2 // (2 * size)) == (col2 // (2 * size))) & ((row2 // size) != (col2 // size))
        tbs = [tbd.astype(BF16) for tbd in tbds]
        tqs = [jnp.dot(tb, _blockdiag(jnp.where(off, a2, 0.0)).astype(BF16), preferred_element_type=F32)
               for tb, a2 in zip(tbs, a2s)]
        tbds = [tbd - jnp.dot(tq.astype(BF16), tb, preferred_element_type=F32)
                for tbd, tq, tb in zip(tbds, tqs, tbs)]
        size *= 2
    return tbds


def _gdn_scan_kernel(q_ref, k_ref, v_ref, ab_ref, alog_ref, dt_ref, o_ref,
                     s_ref, cg_ref, cb_ref, rg_ref, rb_ref, gt_ref, *, base):
    c = CHUNK
    d = pl.program_id(1)
    step = pl.program_id(2)
    rev = d == 1

    @pl.when(step == 0)
    def _():
        s_ref[...] = jnp.zeros(s_ref.shape, F32)

    ab = ab_ref[0]
    beta_all = jax.nn.sigmoid(ab)
    xs = ab + dt_ref[0]
    g_all = -jnp.exp(alog_ref[0]) * (jnp.maximum(xs, 0.0) + jnp.log1p(jnp.exp(-jnp.abs(xs))))
    ri = lax.broadcasted_iota(jnp.int32, (c, c), 0)
    ci = lax.broadcasted_iota(jnp.int32, (c, c), 1)
    sgn = 1 - 2 * d
    cum = jnp.where((ci - ri) * sgn <= 0, 1.0, 0.0).astype(BF16)
    gam_all = sum(jnp.dot(cum, part, preferred_element_type=F32) for part in _split3(g_all))
    gtot_all = jnp.sum(g_all, axis=0, keepdims=True)

    def token_rows(x):
        xt = jnp.concatenate([x, jnp.zeros_like(x)], axis=0).T
        return xt + pltpu.roll(pltpu.roll(xt, LANES - 1, 0), c, 1)

    rg_ref[...] = token_rows(gam_all)
    rb_ref[...] = token_rows(beta_all)
    for h in range(GDN_V_HEADS):
        cg_ref[h] = jnp.broadcast_to(gam_all[:, GDN_V_HEADS + h:GDN_V_HEADS + h + 1], (c, LANES))
        cb_ref[h] = jnp.broadcast_to(beta_all[:, h:h + 1], (c, LANES))
        gt_ref[h:h + 1, :] = jnp.broadcast_to(gtot_all[:, GDN_V_HEADS + h:GDN_V_HEADS + h + 1], (1, LANES))

    row2 = lax.broadcasted_iota(jnp.int32, (c, LANES), 0)
    lane2 = lax.broadcasted_iota(jnp.int32, (c, LANES), 1)
    col2 = lane2 % c
    first2 = lane2 < c

    def run_pairs(pairs):
        heads = [h for p in pairs for h in (2 * p, 2 * p + 1)]
        qkks = {}
        for p in pairs:
            q = q_ref[0, p]
            k = k_ref[0, p]
            qkks[p] = lax.dot_general(jnp.concatenate([q, k], axis=0), jnp.concatenate([k, k], axis=0),
                                      NT_DIMS, preferred_element_type=F32)
        a2s, qkm2s = [], {}
        for p in pairs:
            gc2 = jnp.where(first2, cg_ref[2 * p], cg_ref[2 * p + 1])
            bc2 = jnp.where(first2, cb_ref[2 * p], cb_ref[2 * p + 1])
            gr2 = rg_ref[GDN_V_HEADS + 2 * p:GDN_V_HEADS + 2 * p + 1, :]
            br2 = rb_ref[2 * p:2 * p + 1, :]
            e2 = jnp.exp(-jnp.abs(gc2 - gr2))
            a2s.append(jnp.where(row2 > col2, jnp.where(rev, br2, bc2) * qkks[p][c:2 * c] * e2, 0.0))
            qkm2s[p] = jnp.where((row2 - col2) * sgn >= 0, qkks[p][0:c] * e2, 0.0)

        tbds = dict(zip(pairs, _unit_lower_inverses(a2s, base)))

        uws = {}
        for p in pairs:
            kf = k_ref[0, p].astype(F32)
            rhs = []
            for h in (2 * p, 2 * p + 1):
                cb = cb_ref[h]
                vb = v_ref[0, h].astype(F32) * cb
                kb = kf * (cb * jnp.exp(cg_ref[h]))
                rhs.append(jnp.concatenate([vb, kb], axis=1).astype(BF16))
            tuse = jnp.where(rev, tbds[p].T, tbds[p]).astype(BF16)
            uws[p] = jnp.dot(tuse, jnp.concatenate(rhs, axis=0), preferred_element_type=F32)

        ws_qs = {}
        for h in heads:
            w = uws[h // 2][(h % 2) * c:(h % 2 + 1) * c, GDN_DIM:2 * GDN_DIM]
            qd = q_ref[0, h // 2].astype(F32) * jnp.exp(cg_ref[h])
            wq = jnp.concatenate([w.astype(BF16), qd.astype(BF16)], axis=0)
            ws_qs[h] = jnp.dot(wq, s_ref[h].astype(BF16), preferred_element_type=F32)

        for h in heads:
            u = uws[h // 2][(h % 2) * c:(h % 2 + 1) * c, 0:GDN_DIM]
            v_new = (u - ws_qs[h][0:c]).astype(BF16)
            qkm = qkm2s[h // 2][:, (h % 2) * c:(h % 2 + 1) * c].astype(BF16)
            o = ws_qs[h][c:2 * c] + jnp.dot(qkm, v_new, preferred_element_type=F32)
            o_ref[0, 0, h] = o.astype(o_ref.dtype)
            gt = gt_ref[h:h + 1, :]
            k_dec = (k_ref[0, h // 2].astype(F32) * jnp.exp(gt - cg_ref[h])).astype(BF16)
            s_ref[h] = s_ref[h] * jnp.exp(gt) + lax.dot_general(k_dec, v_new, TN_DIMS,
                                                               preferred_element_type=F32)

    for g0 in range(0, GDN_QK_HEADS, PAIR_GROUP):
        run_pairs(list(range(g0, g0 + PAIR_GROUP)))


def _gdn_scan_call(qk, v, ab, alog, dt, *, ctx_len=CTX_LEN, base=8):
    b, _, tok, _ = qk.shape
    nc = tok // CHUNK
    ncc = ctx_len // CHUNK

    def chunk(dd, s):
        back = jnp.where(s < ncc, ncc - 1 - s, nc - 1 - (s - ncc))
        return jnp.where(dd == 0, s, back)

    return pl.pallas_call(
        functools.partial(_gdn_scan_kernel, base=base),
        out_shape=jax.ShapeDtypeStruct((2, b, GDN_V_HEADS, tok, LANES), BF16),
        grid=(b, 2, nc),
        in_specs=[
            pl.BlockSpec((1, GDN_QK_HEADS, CHUNK, LANES), lambda bi, dd, s: (bi, 0, chunk(dd, s), 0)),
            pl.BlockSpec((1, GDN_QK_HEADS, CHUNK, LANES), lambda bi, dd, s: (bi, 1, chunk(dd, s), 0)),
            pl.BlockSpec((1, GDN_V_HEADS, CHUNK, LANES), lambda bi, dd, s: (bi, 0, chunk(dd, s), 0)),
            pl.BlockSpec((1, CHUNK, LANES), lambda bi, dd, s: (bi, chunk(dd, s), dd)),
            pl.BlockSpec((1, 1, LANES), lambda bi, dd, s: (dd, 0, 0)),
            pl.BlockSpec((1, 1, LANES), lambda bi, dd, s: (dd, 0, 0)),
        ],
        out_specs=pl.BlockSpec((1, 1, GDN_V_HEADS, CHUNK, LANES),
                               lambda bi, dd, s: (dd, bi, 0, chunk(dd, s), 0)),
        scratch_shapes=[
            pltpu.VMEM((GDN_V_HEADS, GDN_DIM, GDN_DIM), F32),
            pltpu.VMEM((GDN_V_HEADS, CHUNK, LANES), F32),
            pltpu.VMEM((GDN_V_HEADS, CHUNK, LANES), F32),
            pltpu.VMEM((LANES, LANES), F32),
            pltpu.VMEM((LANES, LANES), F32),
            pltpu.VMEM((GDN_V_HEADS, LANES), F32),
        ],
        compiler_params=_params(("parallel", "parallel", "arbitrary")),
        name="gdn_chunk_scan",
    )(qk, qk, v, ab, alog, dt)


def _final_norm_kernel(x_ref, w_ref, o_ref):
    xv = x_ref[0]
    o_ref[0] = xv * lax.rsqrt(jnp.mean(xv * xv, axis=-1, keepdims=True) + EPS) * w_ref[...]


def _final_norm_call(x, w, *, ctx_len=CTX_LEN):
    b, tok, d = x.shape
    tr = ctx_len
    return pl.pallas_call(
        _final_norm_kernel,
        out_shape=jax.ShapeDtypeStruct((b, tok - ctx_len, d), F32),
        grid=(b, (tok - ctx_len) // tr),
        in_specs=[pl.BlockSpec((1, tr, d), lambda bi, t: (bi, t + 1, 0)),
                  pl.BlockSpec((1, d), lambda bi, t: (0, 0))],
        out_specs=pl.BlockSpec((1, tr, d), lambda bi, t: (bi, t, 0)),
        compiler_params=_params(("parallel", "parallel")),
        name="final_rmsnorm",
    )(x, w.reshape(1, d))


def _rope_tables(seq, ctx_len, width):
    t = jnp.arange(seq, dtype=jnp.int32)
    rows = (t // GRID_W).astype(F32)
    cols = (t % GRID_W).astype(F32)
    n_freq = DA_QK_DIM // 4
    inv_freq = ROPE_BASE ** (-jnp.arange(n_freq, dtype=F32) / n_freq)
    ar = rows[:, None] * inv_freq
    ac = cols[:, None] * inv_freq
    cos = jnp.concatenate([jnp.cos(ar), jnp.cos(ar), jnp.cos(ac), jnp.cos(ac)], axis=1)
    sin = jnp.concatenate([-jnp.sin(ar), jnp.sin(ar), -jnp.sin(ac), jnp.sin(ac)], axis=1)
    cos = jnp.concatenate([jnp.ones((ctx_len, DA_QK_DIM), F32), cos], axis=0)
    sin = jnp.concatenate([jnp.zeros((ctx_len, DA_QK_DIM), F32), sin], axis=0)
    reps = width // DA_QK_DIM
    return jnp.tile(cos, (1, reps)), jnp.tile(sin, (1, reps))


def kernel(x, c, ctx, c_ctx, w_mod, b_mod, norm_mix, norm_ffn, da_w_qkv, da_lambda, da_head_gain, da_w_o,
           gdn_w_in, gdn_conv, gdn_a_log, gdn_dt_bias, gdn_norm_gain, gdn_w_o, ffn_w_up, ffn_conv,
           ffn_w_down, final_norm):
    bsz, seq, d = x.shape
    ctx_len = ctx.shape[1]
    tm = (ctx_len + seq) // 3
    tn_wide = 1024
    tn_ffn = 512

    xa = jnp.concatenate([ctx, x], axis=1)
    pad_rows = 16 - (bsz + 1)
    cvec = jnp.concatenate([c, c_ctx[None, :], jnp.zeros((pad_rows, d), F32)], axis=0)
    mods_all = _mods_call(cvec, w_mod, b_mod)
    cos_t, sin_t = _rope_tables(seq, ctx_len, MXU_N)

    def sel(mods, idx):
        lat = mods[:bsz, idx * d:(idx + 1) * d]
        cx = jnp.broadcast_to(mods[bsz:bsz + 1, idx * d:(idx + 1) * d], (bsz, d))
        return lat, cx

    def modv(mods, i_shift, i_scale):
        sl, sc = sel(mods, i_shift)
        cl, cc = sel(mods, i_scale)
        return jnp.stack([sl, cl, sc, cc], axis=1)

    def gatev(mods, idx):
        gl, gc = sel(mods, idx)
        return jnp.stack([gl, gc], axis=1)

    for i in range(DEPTH):
        mods = mods_all[i]
        jm = i // 2
        mv = modv(mods, 0, 1)
        if i % 2 == 0:
            lam_init = 0.8 - 0.6 * math.exp(-0.3 * i)
            qkv = _proj_call(xa, mv, norm_mix[i], [da_w_qkv[jm].astype(BF16)], [], epi="rope", tn=tn_wide,
                             tm=tm, extras=(cos_t, sin_t), q_tiles=DA_QK_W // tn_wide, ctx_len=ctx_len)
            att = _diff_attn_call(qkv, da_lambda[jm], da_head_gain[jm], lam_init, ctx_len=ctx_len)
            xa = _oproj_call(att, da_w_o[jm].astype(BF16), xa, gatev(mods, 2), tm=tm, tn=512,
                             ctx_len=ctx_len)
        else:
            w_in = gdn_w_in[jm].astype(BF16)
            cw = gdn_conv[jm]
            qk = _proj_call(xa, mv, norm_mix[i], [w_in[:, :2 * GDN_QK_W]], [cw[:, :2 * GDN_QK_W]],
                            epi="gdn_qk", tn=tn_wide, tm=tm, q_tiles=GDN_QK_W // tn_wide, ctx_len=ctx_len)
            vv = _proj_call(xa, mv, norm_mix[i], [w_in[:, 2 * GDN_QK_W:GDN_QKV_W]],
                            [cw[:, 2 * GDN_QK_W:GDN_QKV_W]], epi="gdn_v", tn=tn_wide, tm=tm,
                            ctx_len=ctx_len)
            z = _proj_call(xa, mv, norm_mix[i], [w_in[:, GDN_QKV_W:GDN_QKV_W + GDN_V_W]], [],
                           epi="plain", tn=tn_wide, tm=tm, ctx_len=ctx_len)
            w_ab = w_in[:, GDN_QKV_W + GDN_V_W:].reshape(d, 2, 2 * GDN_V_HEADS)
            w_ab = jnp.concatenate([w_ab, jnp.zeros_like(w_ab)], axis=2).reshape(d, 2 * LANES)
            ab = _proj_call(xa, mv, norm_mix[i], [w_ab], [], epi="plain", tn=2 * LANES, tm=tm,
                            out_dtype=F32, ctx_len=ctx_len)
            zeros = jnp.zeros((2, GDN_V_HEADS), F32)
            alog = jnp.concatenate([zeros, gdn_a_log[jm], zeros, zeros], axis=1).reshape(2, 1, LANES)
            dtb = jnp.concatenate([zeros, gdn_dt_bias[jm], zeros, zeros], axis=1).reshape(2, 1, LANES)
            o = _gdn_scan_call(qk, vv, ab, alog, dtb, ctx_len=ctx_len)
            y = _gdn_gate_call(o, z, gdn_norm_gain[jm], tr=ctx_len)
            xa = _oproj_call(y, gdn_w_o[jm].astype(BF16), xa, gatev(mods, 2), tm=tm, tn=512,
                             ctx_len=ctx_len)
        padc = D_FF_PAD - D_FF
        w_up = ffn_w_up[i].astype(BF16)
        w_g = jnp.pad(w_up[:, :D_FF], ((0, 0), (0, padc)))
        w_v = jnp.pad(w_up[:, D_FF:], ((0, 0), (0, padc)))
        cwf = ffn_conv[i]
        cw_g = jnp.pad(cwf[:, :D_FF], ((0, 0), (0, padc)))
        cw_v = jnp.pad(cwf[:, D_FF:], ((0, 0), (0, padc)))
        act = _proj_call(xa, modv(mods, 3, 4), norm_ffn[i], [w_g, w_v], [cw_g, cw_v], epi="ffn", tn=tn_ffn,
                         tm=tm, ctx_len=ctx_len)
        w_dn = jnp.pad(ffn_w_down[i].astype(BF16), ((0, padc), (0, 0)))
        xa = _oproj_call(act, w_dn, xa, gatev(mods, 5), tm=tm, tn=512, ctx_len=ctx_len)
    return _final_norm_call(xa, final_norm, ctx_len=ctx_len)
```

```python
import functools
import math

import jax
import jax.numpy as jnp
from jax import lax
from jax.experimental import pallas as pl
from jax.experimental.pallas import tpu as pltpu

F32 = jnp.float32
BF16 = jnp.bfloat16

D_MODEL = 2048
DEPTH = 4
CTX_LEN = 256
GRID_W = 64
EPS = 1e-6
DA_HEADS = 8
DA_QK_DIM = 128
DA_V_DIM = 256
DA_QK_W = DA_HEADS * 2 * DA_QK_DIM
DA_V_W = DA_HEADS * DA_V_DIM
ROPE_BASE = 10000.0
GDN_QK_HEADS = 16
GDN_V_HEADS = 32
GDN_DIM = 128
GDN_QK_W = GDN_QK_HEADS * GDN_DIM
GDN_V_W = GDN_V_HEADS * GDN_DIM
GDN_QKV_W = 2 * GDN_QK_W + GDN_V_W
GDN_CONV = 5
CHUNK = 64
D_FF = 5504
FFN_CONV = 3

LANES = 128
HALO = 16
D_FF_PAD = 5632
VMEM_LIMIT = 56 * 1024 * 1024
LOG2E = math.log2(math.e)

NT_DIMS = (((1,), (1,)), ((), ()))
TN_DIMS = (((0,), (0,)), ((), ()))


def _silu(x):
    return x * jax.nn.sigmoid(x)


def _params(sem):
    return pltpu.CompilerParams(dimension_semantics=sem, vmem_limit_bytes=VMEM_LIMIT)


def _mods_kernel(c_ref, w_ref, b_ref, o_ref):
    s = _silu(c_ref[...]).astype(BF16)
    w = w_ref[0].astype(BF16)
    o_ref[0] = jnp.dot(s, w, preferred_element_type=F32) + b_ref[0]


def _mods_call(cvec, w_mod, b_mod):
    depth, d, n = w_mod.shape
    tn = 1024
    return pl.pallas_call(
        _mods_kernel,
        out_shape=jax.ShapeDtypeStruct((depth, cvec.shape[0], n), F32),
        grid=(depth, n // tn),
        in_specs=[
            pl.BlockSpec((cvec.shape[0], d), lambda i, j: (0, 0)),
            pl.BlockSpec((1, d, tn), lambda i, j: (i, 0, j)),
            pl.BlockSpec((1, 1, tn), lambda i, j: (i, 0, j)),
        ],
        out_specs=pl.BlockSpec((1, cvec.shape[0], tn), lambda i, j: (i, 0, j)),
        compiler_params=_params(("parallel", "parallel")),
        name="adaln_mods",
    )(cvec, w_mod, b_mod.reshape(depth, 1, n))


MXU_N = 256
EPI_ROWS = 128
NORM_ROWS = 16


def _proj_kernel(*refs, n_w, taps, epi, tm, tn, sw, ctx_len, tok, q_tiles):
    halo = taps > 0
    ns = tn // sw
    it = iter(refs)
    x_ref = next(it)
    xp_ref = next(it) if halo else None
    xn_ref = next(it) if halo else None
    modv_ref = next(it)
    nw_ref = next(it)
    w_refs = [next(it) for _ in range(n_w)]
    cw_refs = [next(it) for _ in range(n_w)] if halo else []
    ex_refs = [next(it) for _ in range(2)] if epi == "rope" else []
    out_ref = next(it)
    h_ref = next(it)
    u_refs = [[next(it) for _ in range(ns)] for _ in range(n_w)]

    t = pl.program_id(1)
    j = pl.program_id(2)
    off = HALO if halo else 0
    p = taps // 2

    @pl.when(j == 0)
    def _():
        nw = nw_ref[...]
        gain_l = nw * (1.0 + modv_ref[0, 1:2])
        gain_c = nw * (1.0 + modv_ref[0, 3:4])
        shift_l = modv_ref[0, 0:1]
        shift_c = modv_ref[0, 2:3]

        def norm_rows(src_ref, n_rows, dst0, g0, keep):
            def body(i, carry):
                r = pl.multiple_of(i * NORM_ROWS, NORM_ROWS)
                xv = src_ref[0, pl.ds(r, NORM_ROWS), :]
                inv = lax.rsqrt(jnp.mean(xv * xv, axis=-1, keepdims=True) + EPS)
                is_ctx = (g0 + r + lax.broadcasted_iota(jnp.int32, (NORM_ROWS, 1), 0)) < ctx_len
                hv = (xv * inv) * jnp.where(is_ctx, gain_c, gain_l) + jnp.where(is_ctx, shift_c, shift_l)
                if keep is not None:
                    hv = jnp.where(keep, hv, 0.0)
                h_ref[pl.ds(dst0 + r, NORM_ROWS), :] = hv.astype(BF16)
                return carry
            lax.fori_loop(0, n_rows // NORM_ROWS, body, 0, unroll=2 if n_rows > NORM_ROWS else 1)

        norm_rows(x_ref, tm, off, t * tm, None)
        if halo:
            keep_prev = (t * tm != 0) & (t * tm != ctx_len)
            keep_next = (t * tm + tm != tok) & (t * tm + tm != ctx_len)
            norm_rows(xp_ref, HALO, 0, t * tm - HALO, keep_prev)
            norm_rows(xn_ref, HALO, off + tm, t * tm + tm, keep_next)

    for s in range(ns):
        for w, u_s in zip(w_refs, u_refs):
            u_s[s][...] = jnp.dot(h_ref[...], w[:, s * sw:(s + 1) * sw], preferred_element_type=F32)

    def conv_rows(u_ref, cw_ref, s, r0, nrows, edge):
        if not halo:
            return u_ref[r0:r0 + nrows, :]
        acc = None
        for jj in range(taps):
            dlt = jj - p
            sh = u_ref[off + r0 + dlt:off + r0 + dlt + nrows, :]
            if edge is not None and dlt != 0:
                ri = r0 + lax.broadcasted_iota(jnp.int32, (nrows, 1), 0)
                same = ((ri >= edge) & (ri + dlt >= edge)) | ((ri < edge) & (ri + dlt < edge))
                sh = jnp.where(same, sh, 0.0)
            term = sh * cw_ref[jj:jj + 1, s * sw:(s + 1) * sw]
            acc = term if acc is None else acc + term
        return acc

    def finish(s, r0, nrows, edge=None):
        vals = [conv_rows(u_s[s], cw, s, r0, nrows, edge)
                for u_s, cw in zip(u_refs, cw_refs if halo else [None] * n_w)]
        rs = slice(r0, r0 + nrows)
        cs = slice(s * sw, (s + 1) * sw)
        hpt = sw // LANES
        if epi == "ffn":
            out_ref[0, rs, cs] = (_silu(vals[0]) * vals[1]).astype(out_ref.dtype)
        elif epi == "plain":
            out_ref[0, rs, cs] = vals[0].astype(out_ref.dtype)
        elif epi == "gdn_v":
            a = _silu(vals[0])
            for hh in range(hpt):
                out_ref[0, s * hpt + hh, rs] = a[:, hh * LANES:(hh + 1) * LANES].astype(out_ref.dtype)
        elif epi == "gdn_qk":
            a = _silu(vals[0])
            qs = jnp.where(j < q_tiles, GDN_DIM ** -0.5, 1.0).astype(F32)
            for hh in range(hpt):
                xs = a[:, hh * LANES:(hh + 1) * LANES]
                nrm = xs * lax.rsqrt(jnp.sum(xs * xs, axis=-1, keepdims=True) + EPS)
                out_ref[0, s * hpt + hh, rs] = (nrm * qs).astype(out_ref.dtype)
        elif epi == "rope":
            u = vals[0]
            cos = jnp.where(j < 2 * q_tiles, ex_refs[0][rs, :], 1.0)
            sin = jnp.where(j < 2 * q_tiles, ex_refs[1][rs, :], 0.0)
            lane = lax.broadcasted_iota(jnp.int32, (1, sw), 1)
            swapped = jnp.where((lane % 64) < 32, pltpu.roll(u, sw - 32, 1), pltpu.roll(u, 32, 1))
            sc = jnp.where(j < q_tiles, DA_QK_DIM ** -0.5 * LOG2E, 1.0).astype(F32)
            out_ref[0, rs, cs] = ((u * cos + swapped * sin) * sc).astype(out_ref.dtype)
        else:
            raise ValueError(epi)

    for s in range(ns):
        for r0 in range(0, tm, EPI_ROWS):
            finish(s, r0, EPI_ROWS)

    if halo and ctx_len % tm != 0:
        edge = ctx_len % tm

        @pl.when(t == ctx_len // tm)
        def _():
            for s in range(ns):
                finish(s, edge - HALO, 2 * HALO, edge)


def _proj_call(x, modv, nw, ws, cws, *, epi, tn, tm, sw=MXU_N, n=None, col0=0, out_dtype=BF16, extras=(),
               q_tiles=0, ctx_len=CTX_LEN):
    b, tok, d = x.shape
    n = ws[0].shape[1] if n is None else n
    j0 = col0 // tn
    taps = cws[0].shape[0] if cws else 0
    halo = taps > 0
    nt = tok // tm
    hb = tm // HALO
    nhb = tok // HALO

    in_specs = [pl.BlockSpec((1, tm, d), lambda bi, t, j: (bi, t, 0))]
    args = [x]
    if halo:
        in_specs += [
            pl.BlockSpec((1, HALO, d), lambda bi, t, j: (bi, jnp.maximum(t * hb - 1, 0), 0)),
            pl.BlockSpec((1, HALO, d), lambda bi, t, j: (bi, jnp.minimum((t + 1) * hb, nhb - 1), 0)),
        ]
        args += [x, x]
    in_specs += [pl.BlockSpec((1, 4, d), lambda bi, t, j: (bi, 0, 0)),
                 pl.BlockSpec((1, d), lambda bi, t, j: (0, 0))]
    args += [modv, nw.reshape(1, d)]
    for w in ws:
        in_specs.append(pl.BlockSpec((d, tn), lambda bi, t, j: (0, j0 + j)))
        args.append(w)
    for cw in cws:
        in_specs.append(pl.BlockSpec((taps, tn), lambda bi, t, j: (0, j0 + j)))
        args.append(cw)
    for e in extras:
        in_specs.append(pl.BlockSpec((tm, sw), lambda bi, t, j: (t, 0)))
        args.append(e)

    if epi in ("gdn_v", "gdn_qk"):
        out_shape = jax.ShapeDtypeStruct((b, n // LANES, tok, LANES), out_dtype)
        out_spec = pl.BlockSpec((1, tn // LANES, tm, LANES), lambda bi, t, j: (bi, j, t, 0))
    else:
        out_shape = jax.ShapeDtypeStruct((b, tok, n), out_dtype)
        out_spec = pl.BlockSpec((1, tm, tn), lambda bi, t, j: (bi, t, j))

    kern = functools.partial(_proj_kernel, n_w=len(ws), taps=taps, epi=epi, tm=tm, tn=tn, sw=sw,
                             ctx_len=ctx_len, tok=tok, q_tiles=q_tiles)
    return pl.pallas_call(
        kern,
        out_shape=out_shape,
        grid=(b, nt, n // tn),
        in_specs=in_specs,
        out_specs=out_spec,
        scratch_shapes=([pltpu.VMEM((tm + (2 * HALO if halo else 0), d), BF16)]
                        + [pltpu.VMEM((tm + (2 * HALO if halo else 0), sw), F32)
                           for _ in range(len(ws) * (tn // sw))]),
        compiler_params=_params(("parallel", "parallel", "arbitrary")),
        name="norm_proj_" + epi,
    )(*args)


def _row_gate(g_ref, t, tm, ctx_len):
    g = t * tm + lax.broadcasted_iota(jnp.int32, (tm, 1), 0)
    return jnp.where(g < ctx_len, g_ref[0, 1:2], g_ref[0, 0:1])


def _oproj_kernel(y_ref, w_ref, x_ref, g_ref, o_ref, *, tm, ctx_len):
    acc = jnp.dot(y_ref[0], w_ref[...], preferred_element_type=F32)
    o_ref[0] = x_ref[0] + _row_gate(g_ref, pl.program_id(1), tm, ctx_len) * acc


def _oproj_call(y, w, x, gate, *, tm, tn, ctx_len=CTX_LEN):
    b, tok, d = x.shape
    k = y.shape[-1]
    return pl.pallas_call(
        functools.partial(_oproj_kernel, tm=tm, ctx_len=ctx_len),
        out_shape=jax.ShapeDtypeStruct(x.shape, F32),
        grid=(b, tok // tm, d // tn),
        in_specs=[
            pl.BlockSpec((1, tm, k), lambda bi, t, j: (bi, t, 0)),
            pl.BlockSpec((k, tn), lambda bi, t, j: (0, j)),
            pl.BlockSpec((1, tm, tn), lambda bi, t, j: (bi, t, j)),
            pl.BlockSpec((1, 2, tn), lambda bi, t, j: (bi, 0, j)),
        ],
        out_specs=pl.BlockSpec((1, tm, tn), lambda bi, t, j: (bi, t, j)),
        compiler_params=_params(("parallel", "parallel", "arbitrary")),
        name="out_proj_residual",
    )(y, w, x, gate)


def _gdn_gate_kernel(o_ref, z_ref, gain_ref, y_ref):
    gain = gain_ref[...]
    for h in range(GDN_V_HEADS):
        o = o_ref[0, 0, h].astype(F32) + o_ref[1, 0, h].astype(F32)
        y = o * lax.rsqrt(jnp.mean(o * o, axis=-1, keepdims=True) + EPS) * gain
        z = z_ref[0, :, h * LANES:(h + 1) * LANES].astype(F32)
        y_ref[0, :, h * LANES:(h + 1) * LANES] = (y * _silu(z)).astype(BF16)


def _gdn_gate_call(o, z, gain, *, tr):
    b, tok, k = z.shape
    return pl.pallas_call(
        _gdn_gate_kernel,
        out_shape=jax.ShapeDtypeStruct(z.shape, BF16),
        grid=(b, tok // tr),
        in_specs=[
            pl.BlockSpec((2, 1, GDN_V_HEADS, tr, LANES), lambda bi, t: (0, bi, 0, t, 0)),
            pl.BlockSpec((1, tr, k), lambda bi, t: (bi, t, 0)),
            pl.BlockSpec((1, LANES), lambda bi, t: (0, 0)),
        ],
        out_specs=pl.BlockSpec((1, tr, k), lambda bi, t: (bi, t, 0)),
        compiler_params=_params(("parallel", "parallel")),
        name="gdn_gated_norm",
    )(o, z, gain.reshape(1, LANES))


ATT_HEADS = 2


def _diff_attn_kernel(q_ref, k_ref, v_ref, lam_ref, gain_ref, o_ref, *, lam_init, ctx_len):
    lv = lam_ref[...]
    lam = (jnp.exp(jnp.sum(lv[0:1] * lv[1:2], axis=-1, keepdims=True))
           - jnp.exp(jnp.sum(lv[2:3] * lv[3:4], axis=-1, keepdims=True)) + lam_init)

    def attend(kk, vv):
        q = q_ref[0]
        scores = []
        for hc in range(2 * ATT_HEADS):
            cs = slice(hc * DA_QK_DIM, (hc + 1) * DA_QK_DIM)
            scores.append(lax.dot_general(q[:, cs], kk[:, cs], NT_DIMS, preferred_element_type=F32))
        for h in range(ATT_HEADS):
            ps, ls = [], []
            for c in range(2):
                s = scores[2 * h + c]
                p = jnp.exp2(s - jnp.max(s, axis=-1, keepdims=True))
                ps.append(p)
                ls.append(jnp.sum(p, axis=-1, keepdims=True))
            inv0 = 1.0 / ls[0]
            a = ps[0] - (lam * ls[0] * (1.0 / ls[1])) * ps[1]
            vs = slice(h * DA_V_DIM, (h + 1) * DA_V_DIM)
            o = jnp.dot(a.astype(BF16), vv[:, vs], preferred_element_type=F32) * inv0
            o = o * lax.rsqrt(jnp.mean(o * o, axis=-1, keepdims=True) + EPS) * gain_ref[...]
            o_ref[0, :, vs] = (o * (1.0 - lam_init)).astype(o_ref.dtype)

    t = pl.program_id(2)

    @pl.when(t == 0)
    def _():
        attend(k_ref[0, 0:ctx_len], v_ref[0, 0:ctx_len])

    @pl.when(t != 0)
    def _():
        attend(k_ref[0], v_ref[0])


def _diff_attn_call(qkv, lam_vecs, head_gain, lam_init, *, ctx_len=CTX_LEN):
    b, tok, _ = qkv.shape
    hw = ATT_HEADS * 2 * DA_QK_DIM
    ng = DA_HEADS // ATT_HEADS
    tq = ctx_len
    return pl.pallas_call(
        functools.partial(_diff_attn_kernel, lam_init=lam_init, ctx_len=ctx_len),
        out_shape=jax.ShapeDtypeStruct((b, tok, DA_V_W), BF16),
        grid=(b, ng, tok // tq),
        in_specs=[
            pl.BlockSpec((1, tq, hw), lambda bi, h, t: (bi, t, h)),
            pl.BlockSpec((1, tok, hw), lambda bi, h, t: (bi, 0, ng + h)),
            pl.BlockSpec((1, tok, hw), lambda bi, h, t: (bi, 0, 2 * ng + h)),
            pl.BlockSpec((4, DA_QK_DIM), lambda bi, h, t: (0, 0)),
            pl.BlockSpec((1, DA_V_DIM), lambda bi, h, t: (0, 0)),
        ],
        out_specs=pl.BlockSpec((1, tq, hw), lambda bi, h, t: (bi, t, h)),
        compiler_params=_params(("parallel", "parallel", "arbitrary")),
        name="diff_attention",
    )(qkv, qkv, qkv, lam_vecs, head_gain.reshape(1, DA_V_DIM))


def _split3(x):
    x1 = x.astype(BF16)
    r1 = x - x1.astype(F32)
    x2 = r1.astype(BF16)
    x3 = (r1 - x2.astype(F32)).astype(BF16)
    return x1, x2, x3


def _blockdiag(x2):
    first = lax.broadcasted_iota(jnp.int32, x2.shape, 1) < CHUNK
    return jnp.concatenate([jnp.where(first, x2, 0.0), jnp.where(first, 0.0, x2)], axis=0)


def _unit_lower_inverses(a2s, base):
    c = CHUNK
    nt = c // base
    assert base == 8
    lane = lax.broadcasted_iota(jnp.int32, (base, LANES), 1)
    sub = lax.broadcasted_iota(jnp.int32, (base, LANES), 0)
    row2 = lax.broadcasted_iota(jnp.int32, (c, LANES), 0)
    col2 = lax.broadcasted_iota(jnp.int32, (c, LANES), 1) % c

    in_blk = [((lane % c) // base) == r for r in range(nt)]
    dgs = []
    for a2 in a2s:
        dg = jnp.where(in_blk[0], a2[0:base], 0.0)
        for r in range(1, nt):
            dg = jnp.where(in_blk[r], a2[base * r:base * (r + 1)], dg)
        dgs.append(dg)
    ts = [jnp.where((lane % base) == sub, 1.0, 0.0).astype(F32) for _ in a2s]
    for jcol in range(base - 1):
        idx = (lane // base) * base + jcol
        for i, dg in enumerate(dgs):
            acol = jnp.take_along_axis(dg, idx, axis=1)
            ts[i] = ts[i] - acol * ts[i][jcol:jcol + 1, :]
    tbds = [_blockdiag(jnp.concatenate([jnp.where(in_blk[r], t, 0.0) for r in range(nt)], axis=0))
            for t in ts]

    size = base
    while size < c:
        off = ((row2 // (2 * size)) == (col2 // (2 * size))) & ((row2 // size) != (col2 // size))
        tbs = [tbd.astype(BF16) for tbd in tbds]
        tqs = [jnp.dot(tb, _blockdiag(jnp.where(off, a2, 0.0)).astype(BF16), preferred_element_type=F32)
               for tb, a2 in zip(tbs, a2s)]
        tbds = [tbd - jnp.dot(tq.astype(BF16), tb, preferred_element_type=F32)
                for tbd, tq, tb in zip(tbds, tqs, tbs)]
        size *= 2
    return tbds


def _gdn_scan_kernel(q_ref, k_ref, v_ref, ab_ref, alog_ref, dt_ref, o_ref,
                     s_ref, cg_ref, cb_ref, rg_ref, rb_ref, gt_ref, *, base):
    c = CHUNK
    d = pl.program_id(1)
    step = pl.program_id(2)
    rev = d == 1

    @pl.when(step == 0)
    def _():
        s_ref[...] = jnp.zeros(s_ref.shape, F32)

    ab = ab_ref[0]
    beta_all = jax.nn.sigmoid(ab)
    xs = ab + dt_ref[0]
    g_all = -jnp.exp(alog_ref[0]) * (jnp.maximum(xs, 0.0) + jnp.log1p(jnp.exp(-jnp.abs(xs))))
    ri = lax.broadcasted_iota(jnp.int32, (c, c), 0)
    ci = lax.broadcasted_iota(jnp.int32, (c, c), 1)
    sgn = 1 - 2 * d
    cum = jnp.where((ci - ri) * sgn <= 0, 1.0, 0.0).astype(BF16)
    gam_all = sum(jnp.dot(cum, part, preferred_element_type=F32) for part in _split3(g_all))
    gtot_all = jnp.sum(g_all, axis=0, keepdims=True)

    pi = lax.broadcasted_iota(jnp.int32, (GDN_QK_HEADS, LANES), 0)
    li = lax.broadcasted_iota(jnp.int32, (GDN_QK_HEADS, LANES), 1)
    zpad = jnp.zeros((c, LANES), BF16)

    def pair_rows(x, col0):
        sel_e = jnp.where(li == col0 + 2 * pi, 1.0, 0.0).astype(BF16)
        sel_o = jnp.where(li == col0 + 2 * pi + 1, 1.0, 0.0).astype(BF16)
        acc = jnp.zeros((GDN_QK_HEADS, LANES), F32)
        for part in _split3(x):
            acc = acc + lax.dot_general(sel_e, jnp.concatenate([part, zpad], axis=0), NT_DIMS,
                                        preferred_element_type=F32)
            acc = acc + lax.dot_general(sel_o, jnp.concatenate([zpad, part], axis=0), NT_DIMS,
                                        preferred_element_type=F32)
        return acc

    rg_ref[...] = pair_rows(gam_all, GDN_V_HEADS)
    rb_ref[...] = pair_rows(beta_all, 0)
    for h in range(GDN_V_HEADS):
        cg_ref[h] = jnp.broadcast_to(gam_all[:, GDN_V_HEADS + h:GDN_V_HEADS + h + 1], (c, LANES))
        cb_ref[h] = jnp.broadcast_to(beta_all[:, h:h + 1], (c, LANES))
        gt_ref[h:h + 1, :] = jnp.broadcast_to(gtot_all[:, GDN_V_HEADS + h:GDN_V_HEADS + h + 1], (1, LANES))

    row2 = lax.broadcasted_iota(jnp.int32, (c, LANES), 0)
    lane2 = lax.broadcasted_iota(jnp.int32, (c, LANES), 1)
    col2 = lane2 % c
    first2 = lane2 < c

    pairs = range(GDN_QK_HEADS)
    heads = range(GDN_V_HEADS)

    qkks = []
    for p in pairs:
        q = q_ref[0, p]
        k = k_ref[0, p]
        qkks.append(lax.dot_general(jnp.concatenate([q, k], axis=0), jnp.concatenate([k, k], axis=0),
                                    NT_DIMS, preferred_element_type=F32))
    a2s, qkm2s = [], []
    for p in pairs:
        gc2 = jnp.where(first2, cg_ref[2 * p], cg_ref[2 * p + 1])
        bc2 = jnp.where(first2, cb_ref[2 * p], cb_ref[2 * p + 1])
        gr2 = rg_ref[p:p + 1, :]
        br2 = rb_ref[p:p + 1, :]
        e2 = jnp.exp(-jnp.abs(gc2 - gr2))
        a2s.append(jnp.where(row2 > col2, jnp.where(rev, br2, bc2) * qkks[p][c:2 * c] * e2, 0.0))
        qkm2s.append(jnp.where((row2 - col2) * sgn >= 0, qkks[p][0:c] * e2, 0.0))

    tbds = _unit_lower_inverses(a2s, base)

    uws = []
    for p in pairs:
        kf = k_ref[0, p].astype(F32)
        rhs = []
        for h in (2 * p, 2 * p + 1):
            cb = cb_ref[h]
            vb = v_ref[0, h].astype(F32) * cb
            kb = kf * (cb * jnp.exp(cg_ref[h]))
            rhs.append(jnp.concatenate([vb, kb], axis=1).astype(BF16))
        tuse = jnp.where(rev, tbds[p].T, tbds[p]).astype(BF16)
        uws.append(jnp.dot(tuse, jnp.concatenate(rhs, axis=0), preferred_element_type=F32))

    ws_qs = []
    for h in heads:
        w = uws[h // 2][(h % 2) * c:(h % 2 + 1) * c, GDN_DIM:2 * GDN_DIM]
        qd = q_ref[0, h // 2].astype(F32) * jnp.exp(cg_ref[h])
        wq = jnp.concatenate([w.astype(BF16), qd.astype(BF16)], axis=0)
        ws_qs.append(jnp.dot(wq, s_ref[h].astype(BF16), preferred_element_type=F32))

    for h in heads:
        u = uws[h // 2][(h % 2) * c:(h % 2 + 1) * c, 0:GDN_DIM]
        v_new = (u - ws_qs[h][0:c]).astype(BF16)
        qkm = qkm2s[h // 2][:, (h % 2) * c:(h % 2 + 1) * c].astype(BF16)
        o = ws_qs[h][c:2 * c] + jnp.dot(qkm, v_new, preferred_element_type=F32)
        o_ref[0, 0, h] = o.astype(o_ref.dtype)
        gt = gt_ref[h:h + 1, :]
        k_dec = (k_ref[0, h // 2].astype(F32) * jnp.exp(gt - cg_ref[h])).astype(BF16)
        s_ref[h] = s_ref[h] * jnp.exp(gt) + lax.dot_general(k_dec, v_new, TN_DIMS,
                                                           preferred_element_type=F32)


def _gdn_scan_call(qk, v, ab, alog, dt, *, ctx_len=CTX_LEN, base=8):
    b, _, tok, _ = qk.shape
    nc = tok // CHUNK
    ncc = ctx_len // CHUNK

    def chunk(dd, s):
        back = jnp.where(s < ncc, ncc - 1 - s, nc - 1 - (s - ncc))
        return jnp.where(dd == 0, s, back)

    return pl.pallas_call(
        functools.partial(_gdn_scan_kernel, base=base),
        out_shape=jax.ShapeDtypeStruct((2, b, GDN_V_HEADS, tok, LANES), BF16),
        grid=(b, 2, nc),
        in_specs=[
            pl.BlockSpec((1, GDN_QK_HEADS, CHUNK, LANES), lambda bi, dd, s: (bi, 0, chunk(dd, s), 0)),
            pl.BlockSpec((1, GDN_QK_HEADS, CHUNK, LANES), lambda bi, dd, s: (bi, 1, chunk(dd, s), 0)),
            pl.BlockSpec((1, GDN_V_HEADS, CHUNK, LANES), lambda bi, dd, s: (bi, 0, chunk(dd, s), 0)),
            pl.BlockSpec((1, CHUNK, LANES), lambda bi, dd, s: (bi, chunk(dd, s), dd)),
            pl.BlockSpec((1, 1, LANES), lambda bi, dd, s: (dd, 0, 0)),
            pl.BlockSpec((1, 1, LANES), lambda bi, dd, s: (dd, 0, 0)),
        ],
        out_specs=pl.BlockSpec((1, 1, GDN_V_HEADS, CHUNK, LANES),
                               lambda bi, dd, s: (dd, bi, 0, chunk(dd, s), 0)),
        scratch_shapes=[
            pltpu.VMEM((GDN_V_HEADS, GDN_DIM, GDN_DIM), F32),
            pltpu.VMEM((GDN_V_HEADS, CHUNK, LANES), F32),
            pltpu.VMEM((GDN_V_HEADS, CHUNK, LANES), F32),
            pltpu.VMEM((GDN_QK_HEADS, LANES), F32),
            pltpu.VMEM((GDN_QK_HEADS, LANES), F32),
            pltpu.VMEM((GDN_V_HEADS, LANES), F32),
        ],
        compiler_params=_params(("parallel", "parallel", "arbitrary")),
        name="gdn_chunk_scan",
    )(qk, qk, v, ab, alog, dt)


def _final_norm_kernel(x_ref, w_ref, o_ref):
    xv = x_ref[0]
    o_ref[0] = xv * lax.rsqrt(jnp.mean(xv * xv, axis=-1, keepdims=True) + EPS) * w_ref[...]


def _final_norm_call(x, w, *, ctx_len=CTX_LEN):
    b, tok, d = x.shape
    tr = ctx_len
    return pl.pallas_call(
        _final_norm_kernel,
        out_shape=jax.ShapeDtypeStruct((b, tok - ctx_len, d), F32),
        grid=(b, (tok - ctx_len) // tr),
        in_specs=[pl.BlockSpec((1, tr, d), lambda bi, t: (bi, t + 1, 0)),
                  pl.BlockSpec((1, d), lambda bi, t: (0, 0))],
        out_specs=pl.BlockSpec((1, tr, d), lambda bi, t: (bi, t, 0)),
        compiler_params=_params(("parallel", "parallel")),
        name="final_rmsnorm",
    )(x, w.reshape(1, d))


def _rope_tables(seq, ctx_len, width):
    t = jnp.arange(seq, dtype=jnp.int32)
    rows = (t // GRID_W).astype(F32)
    cols = (t % GRID_W).astype(F32)
    n_freq = DA_QK_DIM // 4
    inv_freq = ROPE_BASE ** (-jnp.arange(n_freq, dtype=F32) / n_freq)
    ar = rows[:, None] * inv_freq
    ac = cols[:, None] * inv_freq
    cos = jnp.concatenate([jnp.cos(ar), jnp.cos(ar), jnp.cos(ac), jnp.cos(ac)], axis=1)
    sin = jnp.concatenate([-jnp.sin(ar), jnp.sin(ar), -jnp.sin(ac), jnp.sin(ac)], axis=1)
    cos = jnp.concatenate([jnp.ones((ctx_len, DA_QK_DIM), F32), cos], axis=0)
    sin = jnp.concatenate([jnp.zeros((ctx_len, DA_QK_DIM), F32), sin], axis=0)
    reps = width // DA_QK_DIM
    return jnp.tile(cos, (1, reps)), jnp.tile(sin, (1, reps))


def kernel(x, c, ctx, c_ctx, w_mod, b_mod, norm_mix, norm_ffn, da_w_qkv, da_lambda, da_head_gain, da_w_o,
           gdn_w_in, gdn_conv, gdn_a_log, gdn_dt_bias, gdn_norm_gain, gdn_w_o, ffn_w_up, ffn_conv,
           ffn_w_down, final_norm):
    bsz, seq, d = x.shape
    ctx_len = ctx.shape[1]
    tm = (ctx_len + seq) // 3
    tn_wide = 1024
    tn_ffn = 512
    sw_light = 512

    xa = jnp.concatenate([ctx, x], axis=1)
    pad_rows = 16 - (bsz + 1)
    cvec = jnp.concatenate([c, c_ctx[None, :], jnp.zeros((pad_rows, d), F32)], axis=0)
    mods_all = _mods_call(cvec, w_mod, b_mod)
    cos_t, sin_t = _rope_tables(seq, ctx_len, sw_light)

    def sel(mods, idx):
        lat = mods[:bsz, idx * d:(idx + 1) * d]
        cx = jnp.broadcast_to(mods[bsz:bsz + 1, idx * d:(idx + 1) * d], (bsz, d))
        return lat, cx

    def modv(mods, i_shift, i_scale):
        sl, sc = sel(mods, i_shift)
        cl, cc = sel(mods, i_scale)
        return jnp.stack([sl, cl, sc, cc], axis=1)

    def gatev(mods, idx):
        gl, gc = sel(mods, idx)
        return jnp.stack([gl, gc], axis=1)

    for i in range(DEPTH):
        mods = mods_all[i]
        jm = i // 2
        mv = modv(mods, 0, 1)
        if i % 2 == 0:
            lam_init = 0.8 - 0.6 * math.exp(-0.3 * i)
            qkv = _proj_call(xa, mv, norm_mix[i], [da_w_qkv[jm].astype(BF16)], [], epi="rope", tn=tn_wide,
                             tm=tm, sw=sw_light, extras=(cos_t, sin_t), q_tiles=DA_QK_W // tn_wide,
                             ctx_len=ctx_len)
            att = _diff_attn_call(qkv, da_lambda[jm], da_head_gain[jm], lam_init, ctx_len=ctx_len)
            xa = _oproj_call(att, da_w_o[jm].astype(BF16), xa, gatev(mods, 2), tm=tm, tn=512,
                             ctx_len=ctx_len)
        else:
            w_in = gdn_w_in[jm].astype(BF16)
            cw = gdn_conv[jm]
            qk = _proj_call(xa, mv, norm_mix[i], [w_in], [cw], epi="gdn_qk", tn=tn_wide, tm=tm,
                            n=2 * GDN_QK_W, col0=0, q_tiles=GDN_QK_W // tn_wide, ctx_len=ctx_len)
            vv = _proj_call(xa, mv, norm_mix[i], [w_in], [cw], epi="gdn_v", tn=tn_wide, tm=tm,
                            n=GDN_V_W, col0=2 * GDN_QK_W, ctx_len=ctx_len)
            z = _proj_call(xa, mv, norm_mix[i], [w_in], [], epi="plain", tn=tn_wide, tm=tm, sw=sw_light,
                           n=GDN_V_W, col0=GDN_QKV_W, ctx_len=ctx_len)
            w_ab = w_in[:, GDN_QKV_W + GDN_V_W:].reshape(d, 2, 2 * GDN_V_HEADS)
            w_ab = jnp.concatenate([w_ab, jnp.zeros_like(w_ab)], axis=2).reshape(d, 2 * LANES)
            ab = _proj_call(xa, mv, norm_mix[i], [w_ab], [], epi="plain", tn=2 * LANES, tm=tm,
                            out_dtype=F32, ctx_len=ctx_len)
            zeros = jnp.zeros((2, GDN_V_HEADS), F32)
            alog = jnp.concatenate([zeros, gdn_a_log[jm], zeros, zeros], axis=1).reshape(2, 1, LANES)
            dtb = jnp.concatenate([zeros, gdn_dt_bias[jm], zeros, zeros], axis=1).reshape(2, 1, LANES)
            o = _gdn_scan_call(qk, vv, ab, alog, dtb, ctx_len=ctx_len)
            y = _gdn_gate_call(o, z, gdn_norm_gain[jm], tr=ctx_len)
            xa = _oproj_call(y, gdn_w_o[jm].astype(BF16), xa, gatev(mods, 2), tm=tm, tn=512,
                             ctx_len=ctx_len)
        padc = D_FF_PAD - D_FF
        w_up = ffn_w_up[i].astype(BF16)
        w_g = jnp.pad(w_up[:, :D_FF], ((0, 0), (0, padc)))
        w_v = jnp.pad(w_up[:, D_FF:], ((0, 0), (0, padc)))
        cwf = ffn_conv[i]
        cw_g = jnp.pad(cwf[:, :D_FF], ((0, 0), (0, padc)))
        cw_v = jnp.pad(cwf[:, D_FF:], ((0, 0), (0, padc)))
        act = _proj_call(xa, modv(mods, 3, 4), norm_ffn[i], [w_g, w_v], [cw_g, cw_v], epi="ffn", tn=tn_ffn,
                         tm=tm, sw=sw_light, ctx_len=ctx_len)
        w_dn = jnp.pad(ffn_w_down[i].astype(BF16), ((0, padc), (0, 0)))
        xa = _oproj_call(act, w_dn, xa, gatev(mods, 5), tm=tm, tn=512, ctx_len=ctx_len)
    return _final_norm_call(xa, final_norm, ctx_len=ctx_len)
```

```python
import functools
import math

import jax
import jax.numpy as jnp
from jax import lax
from jax.experimental import pallas as pl
from jax.experimental.pallas import tpu as pltpu

F32 = jnp.float32
BF16 = jnp.bfloat16

D_MODEL = 2048
DEPTH = 4
CTX_LEN = 256
GRID_W = 64
EPS = 1e-6
DA_HEADS = 8
DA_QK_DIM = 128
DA_V_DIM = 256
DA_QK_W = DA_HEADS * 2 * DA_QK_DIM
DA_V_W = DA_HEADS * DA_V_DIM
ROPE_BASE = 10000.0
GDN_QK_HEADS = 16
GDN_V_HEADS = 32
GDN_DIM = 128
GDN_QK_W = GDN_QK_HEADS * GDN_DIM
GDN_V_W = GDN_V_HEADS * GDN_DIM
GDN_QKV_W = 2 * GDN_QK_W + GDN_V_W
GDN_CONV = 5
CHUNK = 64
D_FF = 5504
FFN_CONV = 3

LANES = 128
HALO = 16
D_FF_PAD = 5632
VMEM_LIMIT = 56 * 1024 * 1024
LOG2E = math.log2(math.e)

NT_DIMS = (((1,), (1,)), ((), ()))
TN_DIMS = (((0,), (0,)), ((), ()))


def _silu(x):
    return x * jax.nn.sigmoid(x)


def _params(sem):
    return pltpu.CompilerParams(dimension_semantics=sem, vmem_limit_bytes=VMEM_LIMIT)


def _mods_kernel(c_ref, w_ref, b_ref, o_ref):
    s = _silu(c_ref[...]).astype(BF16)
    w = w_ref[0].astype(BF16)
    o_ref[0] = jnp.dot(s, w, preferred_element_type=F32) + b_ref[0]


def _mods_call(cvec, w_mod, b_mod):
    depth, d, n = w_mod.shape
    tn = 1024
    return pl.pallas_call(
        _mods_kernel,
        out_shape=jax.ShapeDtypeStruct((depth, cvec.shape[0], n), F32),
        grid=(depth, n // tn),
        in_specs=[
            pl.BlockSpec((cvec.shape[0], d), lambda i, j: (0, 0)),
            pl.BlockSpec((1, d, tn), lambda i, j: (i, 0, j)),
            pl.BlockSpec((1, 1, tn), lambda i, j: (i, 0, j)),
        ],
        out_specs=pl.BlockSpec((1, cvec.shape[0], tn), lambda i, j: (i, 0, j)),
        compiler_params=_params(("parallel", "parallel")),
        name="adaln_mods",
    )(cvec, w_mod, b_mod.reshape(depth, 1, n))


MXU_N = 256
EPI_ROWS = 128
NORM_ROWS = 16


def _proj_kernel(*refs, n_w, taps, epi, tm, tn, sw, ctx_len, tok, q_tiles):
    halo = taps > 0
    ns = tn // sw
    it = iter(refs)
    x_ref = next(it)
    xp_ref = next(it) if halo else None
    xn_ref = next(it) if halo else None
    modv_ref = next(it)
    nw_ref = next(it)
    w_refs = [next(it) for _ in range(n_w)]
    cw_refs = [next(it) for _ in range(n_w)] if halo else []
    ex_refs = [next(it) for _ in range(2)] if epi == "rope" else []
    out_ref = next(it)
    h_ref = next(it)
    u_refs = [[next(it) for _ in range(ns)] for _ in range(n_w)]

    t = pl.program_id(1)
    j = pl.program_id(2)
    off = HALO if halo else 0
    p = taps // 2

    @pl.when(j == 0)
    def _():
        nw = nw_ref[...]
        gain_l = nw * (1.0 + modv_ref[0, 1:2])
        gain_c = nw * (1.0 + modv_ref[0, 3:4])
        shift_l = modv_ref[0, 0:1]
        shift_c = modv_ref[0, 2:3]

        def norm_rows(src_ref, n_rows, dst0, g0, keep):
            def body(i, carry):
                r = pl.multiple_of(i * NORM_ROWS, NORM_ROWS)
                xv = src_ref[0, pl.ds(r, NORM_ROWS), :]
                inv = lax.rsqrt(jnp.mean(xv * xv, axis=-1, keepdims=True) + EPS)
                is_ctx = (g0 + r + lax.broadcasted_iota(jnp.int32, (NORM_ROWS, 1), 0)) < ctx_len
                hv = (xv * inv) * jnp.where(is_ctx, gain_c, gain_l) + jnp.where(is_ctx, shift_c, shift_l)
                if keep is not None:
                    hv = jnp.where(keep, hv, 0.0)
                h_ref[pl.ds(dst0 + r, NORM_ROWS), :] = hv.astype(BF16)
                return carry
            lax.fori_loop(0, n_rows // NORM_ROWS, body, 0, unroll=2 if n_rows > NORM_ROWS else 1)

        norm_rows(x_ref, tm, off, t * tm, None)
        if halo:
            keep_prev = (t * tm != 0) & (t * tm != ctx_len)
            keep_next = (t * tm + tm != tok) & (t * tm + tm != ctx_len)
            norm_rows(xp_ref, HALO, 0, t * tm - HALO, keep_prev)
            norm_rows(xn_ref, HALO, off + tm, t * tm + tm, keep_next)

    for s in range(ns):
        for w, u_s in zip(w_refs, u_refs):
            u_s[s][...] = jnp.dot(h_ref[...], w[:, s * sw:(s + 1) * sw], preferred_element_type=F32)

    def conv_rows(u_ref, cw_ref, s, r0, nrows, edge):
        if not halo:
            return u_ref[r0:r0 + nrows, :]
        acc = None
        for jj in range(taps):
            dlt = jj - p
            sh = u_ref[off + r0 + dlt:off + r0 + dlt + nrows, :]
            if edge is not None and dlt != 0:
                ri = r0 + lax.broadcasted_iota(jnp.int32, (nrows, 1), 0)
                same = ((ri >= edge) & (ri + dlt >= edge)) | ((ri < edge) & (ri + dlt < edge))
                sh = jnp.where(same, sh, 0.0)
            term = sh * cw_ref[jj:jj + 1, s * sw:(s + 1) * sw]
            acc = term if acc is None else acc + term
        return acc

    def finish(s, r0, nrows, edge=None):
        vals = [conv_rows(u_s[s], cw, s, r0, nrows, edge)
                for u_s, cw in zip(u_refs, cw_refs if halo else [None] * n_w)]
        rs = slice(r0, r0 + nrows)
        cs = slice(s * sw, (s + 1) * sw)
        hpt = sw // LANES
        if epi == "ffn":
            out_ref[0, rs, cs] = (_silu(vals[0]) * vals[1]).astype(out_ref.dtype)
        elif epi == "plain":
            out_ref[0, rs, cs] = vals[0].astype(out_ref.dtype)
        elif epi == "gdn_v":
            a = _silu(vals[0])
            for hh in range(hpt):
                out_ref[0, s * hpt + hh, rs] = a[:, hh * LANES:(hh + 1) * LANES].astype(out_ref.dtype)
        elif epi == "gdn_qk":
            a = _silu(vals[0])
            qs = jnp.where(j < q_tiles, GDN_DIM ** -0.5, 1.0).astype(F32)
            for hh in range(hpt):
                xs = a[:, hh * LANES:(hh + 1) * LANES]
                nrm = xs * lax.rsqrt(jnp.sum(xs * xs, axis=-1, keepdims=True) + EPS)
                out_ref[0, s * hpt + hh, rs] = (nrm * qs).astype(out_ref.dtype)
        elif epi == "rope":
            u = vals[0]
            cos = jnp.where(j < 2 * q_tiles, ex_refs[0][rs, :], 1.0)
            sin = jnp.where(j < 2 * q_tiles, ex_refs[1][rs, :], 0.0)
            lane = lax.broadcasted_iota(jnp.int32, (1, sw), 1)
            swapped = jnp.where((lane % 64) < 32, pltpu.roll(u, sw - 32, 1), pltpu.roll(u, 32, 1))
            sc = jnp.where(j < q_tiles, DA_QK_DIM ** -0.5 * LOG2E, 1.0).astype(F32)
            out_ref[0, rs, cs] = ((u * cos + swapped * sin) * sc).astype(out_ref.dtype)
        else:
            raise ValueError(epi)

    for s in range(ns):
        for r0 in range(0, tm, EPI_ROWS):
            finish(s, r0, EPI_ROWS)

    if halo and ctx_len % tm != 0:
        edge = ctx_len % tm

        @pl.when(t == ctx_len // tm)
        def _():
            for s in range(ns):
                finish(s, edge - HALO, 2 * HALO, edge)


def _proj_call(x, modv, nw, ws, cws, *, epi, tn, tm, sw=MXU_N, n=None, col0=0, out_dtype=BF16, extras=(),
               q_tiles=0, ctx_len=CTX_LEN):
    b, tok, d = x.shape
    n = ws[0].shape[1] if n is None else n
    j0 = col0 // tn
    taps = cws[0].shape[0] if cws else 0
    halo = taps > 0
    nt = tok // tm
    hb = tm // HALO
    nhb = tok // HALO

    in_specs = [pl.BlockSpec((1, tm, d), lambda bi, t, j: (bi, t, 0))]
    args = [x]
    if halo:
        in_specs += [
            pl.BlockSpec((1, HALO, d), lambda bi, t, j: (bi, jnp.maximum(t * hb - 1, 0), 0)),
            pl.BlockSpec((1, HALO, d), lambda bi, t, j: (bi, jnp.minimum((t + 1) * hb, nhb - 1), 0)),
        ]
        args += [x, x]
    in_specs += [pl.BlockSpec((1, 4, d), lambda bi, t, j: (bi, 0, 0)),
                 pl.BlockSpec((1, d), lambda bi, t, j: (0, 0))]
    args += [modv, nw.reshape(1, d)]
    for w in ws:
        in_specs.append(pl.BlockSpec((d, tn), lambda bi, t, j: (0, j0 + j)))
        args.append(w)
    for cw in cws:
        in_specs.append(pl.BlockSpec((taps, tn), lambda bi, t, j: (0, j0 + j)))
        args.append(cw)
    for e in extras:
        in_specs.append(pl.BlockSpec((tm, sw), lambda bi, t, j: (t, 0)))
        args.append(e)

    if epi in ("gdn_v", "gdn_qk"):
        out_shape = jax.ShapeDtypeStruct((b, n // LANES, tok, LANES), out_dtype)
        out_spec = pl.BlockSpec((1, tn // LANES, tm, LANES), lambda bi, t, j: (bi, j, t, 0))
    else:
        out_shape = jax.ShapeDtypeStruct((b, tok, n), out_dtype)
        out_spec = pl.BlockSpec((1, tm, tn), lambda bi, t, j: (bi, t, j))

    kern = functools.partial(_proj_kernel, n_w=len(ws), taps=taps, epi=epi, tm=tm, tn=tn, sw=sw,
                             ctx_len=ctx_len, tok=tok, q_tiles=q_tiles)
    return pl.pallas_call(
        kern,
        out_shape=out_shape,
        grid=(b, nt, n // tn),
        in_specs=in_specs,
        out_specs=out_spec,
        scratch_shapes=([pltpu.VMEM((tm + (2 * HALO if halo else 0), d), BF16)]
                        + [pltpu.VMEM((tm + (2 * HALO if halo else 0), sw), F32)
                           for _ in range(len(ws) * (tn // sw))]),
        compiler_params=_params(("parallel", "parallel", "arbitrary")),
        name="norm_proj_" + epi,
    )(*args)


def _row_gate(g_ref, t, tm, ctx_len):
    g = t * tm + lax.broadcasted_iota(jnp.int32, (tm, 1), 0)
    return jnp.where(g < ctx_len, g_ref[0, 1:2], g_ref[0, 0:1])


def _oproj_kernel(y_ref, w_ref, x_ref, g_ref, o_ref, *, tm, ctx_len):
    acc = jnp.dot(y_ref[0], w_ref[...], preferred_element_type=F32)
    o_ref[0] = x_ref[0] + _row_gate(g_ref, pl.program_id(1), tm, ctx_len) * acc


def _oproj_call(y, w, x, gate, *, tm, tn, ctx_len=CTX_LEN):
    b, tok, d = x.shape
    k = y.shape[-1]
    return pl.pallas_call(
        functools.partial(_oproj_kernel, tm=tm, ctx_len=ctx_len),
        out_shape=jax.ShapeDtypeStruct(x.shape, F32),
        grid=(b, tok // tm, d // tn),
        in_specs=[
            pl.BlockSpec((1, tm, k), lambda bi, t, j: (bi, t, 0)),
            pl.BlockSpec((k, tn), lambda bi, t, j: (0, j)),
            pl.BlockSpec((1, tm, tn), lambda bi, t, j: (bi, t, j)),
            pl.BlockSpec((1, 2, tn), lambda bi, t, j: (bi, 0, j)),
        ],
        out_specs=pl.BlockSpec((1, tm, tn), lambda bi, t, j: (bi, t, j)),
        compiler_params=_params(("parallel", "parallel", "arbitrary")),
        name="out_proj_residual",
    )(y, w, x, gate)


def _gdn_gate_kernel(o_ref, z_ref, gain_ref, y_ref):
    gain = gain_ref[...]
    for h in range(GDN_V_HEADS):
        o = o_ref[0, 0, h].astype(F32) + o_ref[1, 0, h].astype(F32)
        y = o * lax.rsqrt(jnp.mean(o * o, axis=-1, keepdims=True) + EPS) * gain
        z = z_ref[0, :, h * LANES:(h + 1) * LANES].astype(F32)
        y_ref[0, :, h * LANES:(h + 1) * LANES] = (y * _silu(z)).astype(BF16)


def _gdn_gate_call(o, z, gain, *, tr):
    b, tok, k = z.shape
    return pl.pallas_call(
        _gdn_gate_kernel,
        out_shape=jax.ShapeDtypeStruct(z.shape, BF16),
        grid=(b, tok // tr),
        in_specs=[
            pl.BlockSpec((2, 1, GDN_V_HEADS, tr, LANES), lambda bi, t: (0, bi, 0, t, 0)),
            pl.BlockSpec((1, tr, k), lambda bi, t: (bi, t, 0)),
            pl.BlockSpec((1, LANES), lambda bi, t: (0, 0)),
        ],
        out_specs=pl.BlockSpec((1, tr, k), lambda bi, t: (bi, t, 0)),
        compiler_params=_params(("parallel", "parallel")),
        name="gdn_gated_norm",
    )(o, z, gain.reshape(1, LANES))


ATT_HEADS = 2


def _diff_attn_kernel(q_ref, k_ref, v_ref, lam_ref, gain_ref, o_ref, *, lam_init, ctx_len):
    lv = lam_ref[...]
    lam = (jnp.exp(jnp.sum(lv[0:1] * lv[1:2], axis=-1, keepdims=True))
           - jnp.exp(jnp.sum(lv[2:3] * lv[3:4], axis=-1, keepdims=True)) + lam_init)

    def attend(kk, vv):
        q = q_ref[0]
        scores = []
        for hc in range(2 * ATT_HEADS):
            cs = slice(hc * DA_QK_DIM, (hc + 1) * DA_QK_DIM)
            scores.append(lax.dot_general(q[:, cs], kk[:, cs], NT_DIMS, preferred_element_type=F32))
        for h in range(ATT_HEADS):
            ps, ls = [], []
            for c in range(2):
                s = scores[2 * h + c]
                p = jnp.exp2(s - jnp.max(s, axis=-1, keepdims=True))
                ps.append(p)
                ls.append(jnp.sum(p, axis=-1, keepdims=True))
            inv0 = 1.0 / ls[0]
            a = ps[0] - (lam * ls[0] * (1.0 / ls[1])) * ps[1]
            vs = slice(h * DA_V_DIM, (h + 1) * DA_V_DIM)
            o = jnp.dot(a.astype(BF16), vv[:, vs], preferred_element_type=F32) * inv0
            o = o * lax.rsqrt(jnp.mean(o * o, axis=-1, keepdims=True) + EPS) * gain_ref[...]
            o_ref[0, :, vs] = (o * (1.0 - lam_init)).astype(o_ref.dtype)

    t = pl.program_id(2)

    @pl.when(t == 0)
    def _():
        attend(k_ref[0, 0:ctx_len], v_ref[0, 0:ctx_len])

    @pl.when(t != 0)
    def _():
        attend(k_ref[0], v_ref[0])


def _diff_attn_call(qkv, lam_vecs, head_gain, lam_init, *, ctx_len=CTX_LEN):
    b, tok, _ = qkv.shape
    hw = ATT_HEADS * 2 * DA_QK_DIM
    ng = DA_HEADS // ATT_HEADS
    tq = ctx_len
    return pl.pallas_call(
        functools.partial(_diff_attn_kernel, lam_init=lam_init, ctx_len=ctx_len),
        out_shape=jax.ShapeDtypeStruct((b, tok, DA_V_W), BF16),
        grid=(b, ng, tok // tq),
        in_specs=[
            pl.BlockSpec((1, tq, hw), lambda bi, h, t: (bi, t, h)),
            pl.BlockSpec((1, tok, hw), lambda bi, h, t: (bi, 0, ng + h)),
            pl.BlockSpec((1, tok, hw), lambda bi, h, t: (bi, 0, 2 * ng + h)),
            pl.BlockSpec((4, DA_QK_DIM), lambda bi, h, t: (0, 0)),
            pl.BlockSpec((1, DA_V_DIM), lambda bi, h, t: (0, 0)),
        ],
        out_specs=pl.BlockSpec((1, tq, hw), lambda bi, h, t: (bi, t, h)),
        compiler_params=_params(("parallel", "parallel", "arbitrary")),
        name="diff_attention",
    )(qkv, qkv, qkv, lam_vecs, head_gain.reshape(1, DA_V_DIM))


def _split3(x):
    x1 = x.astype(BF16)
    r1 = x - x1.astype(F32)
    x2 = r1.astype(BF16)
    x3 = (r1 - x2.astype(F32)).astype(BF16)
    return x1, x2, x3


def _blockdiag(x2):
    first = lax.broadcasted_iota(jnp.int32, x2.shape, 1) < CHUNK
    return jnp.concatenate([jnp.where(first, x2, 0.0), jnp.where(first, 0.0, x2)], axis=0)


def _unit_lower_inverses(a2s, base):
    c = CHUNK
    nt = c // base
    assert base == 8
    lane = lax.broadcasted_iota(jnp.int32, (base, LANES), 1)
    sub = lax.broadcasted_iota(jnp.int32, (base, LANES), 0)
    row2 = lax.broadcasted_iota(jnp.int32, (c, LANES), 0)
    col2 = lax.broadcasted_iota(jnp.int32, (c, LANES), 1) % c

    in_blk = [((lane % c) // base) == r for r in range(nt)]
    dgs = []
    for a2 in a2s:
        dg = jnp.where(in_blk[0], a2[0:base], 0.0)
        for r in range(1, nt):
            dg = jnp.where(in_blk[r], a2[base * r:base * (r + 1)], dg)
        dgs.append(dg)
    ts = [jnp.where((lane % base) == sub, 1.0, 0.0).astype(F32) for _ in a2s]
    for jcol in range(base - 1):
        idx = (lane // base) * base + jcol
        for i, dg in enumerate(dgs):
            acol = jnp.take_along_axis(dg, idx, axis=1)
            ts[i] = ts[i] - acol * ts[i][jcol:jcol + 1, :]
    tbds = [_blockdiag(jnp.concatenate([jnp.where(in_blk[r], t, 0.0) for r in range(nt)], axis=0))
            for t in ts]

    size = base
    while size < c:
        off = ((row2 // (2 * size)) == (col2 // (2 * size))) & ((row2 // size) != (col2 // size))
        tbs = [tbd.astype(BF16) for tbd in tbds]
        tqs = [jnp.dot(tb, _blockdiag(jnp.where(off, a2, 0.0)).astype(BF16), preferred_element_type=F32)
               for tb, a2 in zip(tbs, a2s)]
        tbds = [tbd - jnp.dot(tq.astype(BF16), tb, preferred_element_type=F32)
                for tbd, tq, tb in zip(tbds, tqs, tbs)]
        size *= 2
    return tbds


def _gdn_scan_kernel(q_ref, k_ref, v_ref, ab_ref, alog_ref, dt_ref, o_ref,
                     s_ref, cg_ref, cb_ref, rg_ref, rb_ref, gt_ref, *, base):
    c = CHUNK
    d = pl.program_id(1)
    step = pl.program_id(2)
    rev = d == 1

    @pl.when(step == 0)
    def _():
        s_ref[...] = jnp.zeros(s_ref.shape, F32)

    ab = ab_ref[0]
    beta_all = jax.nn.sigmoid(ab)
    xs = ab + dt_ref[0]
    g_all = -jnp.exp(alog_ref[0]) * (jnp.maximum(xs, 0.0) + jnp.log1p(jnp.exp(-jnp.abs(xs))))
    ri = lax.broadcasted_iota(jnp.int32, (c, c), 0)
    ci = lax.broadcasted_iota(jnp.int32, (c, c), 1)
    sgn = 1 - 2 * d
    cum = jnp.where((ci - ri) * sgn <= 0, 1.0, 0.0).astype(BF16)
    gam_all = sum(jnp.dot(cum, part, preferred_element_type=F32) for part in _split3(g_all))
    gtot_all = jnp.sum(g_all, axis=0, keepdims=True)

    pi = lax.broadcasted_iota(jnp.int32, (GDN_QK_HEADS, LANES), 0)
    li = lax.broadcasted_iota(jnp.int32, (GDN_QK_HEADS, LANES), 1)
    zpad = jnp.zeros((c, LANES), BF16)

    def pair_rows(x, col0):
        sel_e = jnp.where(li == col0 + 2 * pi, 1.0, 0.0).astype(BF16)
        sel_o = jnp.where(li == col0 + 2 * pi + 1, 1.0, 0.0).astype(BF16)
        acc = jnp.zeros((GDN_QK_HEADS, LANES), F32)
        for part in _split3(x):
            acc = acc + lax.dot_general(sel_e, jnp.concatenate([part, zpad], axis=0), NT_DIMS,
                                        preferred_element_type=F32)
            acc = acc + lax.dot_general(sel_o, jnp.concatenate([zpad, part], axis=0), NT_DIMS,
                                        preferred_element_type=F32)
        return acc

    rg_ref[...] = pair_rows(gam_all, GDN_V_HEADS)
    rb_ref[...] = pair_rows(beta_all, 0)
    for h in range(GDN_V_HEADS):
        cg_ref[h] = jnp.broadcast_to(gam_all[:, GDN_V_HEADS + h:GDN_V_HEADS + h + 1], (c, LANES))
        cb_ref[h] = jnp.broadcast_to(beta_all[:, h:h + 1], (c, LANES))
        gt_ref[h:h + 1, :] = jnp.broadcast_to(gtot_all[:, GDN_V_HEADS + h:GDN_V_HEADS + h + 1], (1, LANES))

    row2 = lax.broadcasted_iota(jnp.int32, (c, LANES), 0)
    lane2 = lax.broadcasted_iota(jnp.int32, (c, LANES), 1)
    col2 = lane2 % c
    first2 = lane2 < c

    pairs = range(GDN_QK_HEADS)
    heads = range(GDN_V_HEADS)

    qkks = []
    for p in pairs:
        q = q_ref[0, p]
        k = k_ref[0, p]
        qkks.append(lax.dot_general(jnp.concatenate([q, k], axis=0), jnp.concatenate([k, k], axis=0),
                                    NT_DIMS, preferred_element_type=F32))
    a2s, qkm2s = [], []
    for p in pairs:
        gc2 = jnp.where(first2, cg_ref[2 * p], cg_ref[2 * p + 1])
        bc2 = jnp.where(first2, cb_ref[2 * p], cb_ref[2 * p + 1])
        gr2 = rg_ref[p:p + 1, :]
        br2 = rb_ref[p:p + 1, :]
        e2 = jnp.exp(-jnp.abs(gc2 - gr2))
        a2s.append(jnp.where(row2 > col2, jnp.where(rev, br2, bc2) * qkks[p][c:2 * c] * e2, 0.0))
        qkm2s.append(jnp.where((row2 - col2) * sgn >= 0, qkks[p][0:c] * e2, 0.0))

    tbds = _unit_lower_inverses(a2s, base)

    uws = []
    for p in pairs:
        kf = k_ref[0, p].astype(F32)
        rhs = []
        for h in (2 * p, 2 * p + 1):
            cb = cb_ref[h]
            vb = v_ref[0, h].astype(F32) * cb
            kb = kf * (cb * jnp.exp(cg_ref[h]))
            rhs.append(jnp.concatenate([vb, kb], axis=1).astype(BF16))
        tuse = jnp.where(rev, tbds[p].T, tbds[p]).astype(BF16)
        uws.append(jnp.dot(tuse, jnp.concatenate(rhs, axis=0), preferred_element_type=F32))

    ws_qs = []
    for h in heads:
        w = uws[h // 2][(h % 2) * c:(h % 2 + 1) * c, GDN_DIM:2 * GDN_DIM]
        qd = q_ref[0, h // 2].astype(F32) * jnp.exp(cg_ref[h])
        wq = jnp.concatenate([w.astype(BF16), qd.astype(BF16)], axis=0)
        ws_qs.append(jnp.dot(wq, s_ref[h].astype(BF16), preferred_element_type=F32))

    for h in heads:
        u = uws[h // 2][(h % 2) * c:(h % 2 + 1) * c, 0:GDN_DIM]
        v_new = (u - ws_qs[h][0:c]).astype(BF16)
        qkm = qkm2s[h // 2][:, (h % 2) * c:(h % 2 + 1) * c].astype(BF16)
        o = ws_qs[h][c:2 * c] + jnp.dot(qkm, v_new, preferred_element_type=F32)
        o_ref[0, 0, h] = o.astype(o_ref.dtype)
        gt = gt_ref[h:h + 1, :]
        k_dec = (k_ref[0, h // 2].astype(F32) * jnp.exp(gt - cg_ref[h])).astype(BF16)
        s_ref[h] = s_ref[h] * jnp.exp(gt) + lax.dot_general(k_dec, v_new, TN_DIMS,
                                                           preferred_element_type=F32)


def _gdn_scan_call(qk, v, ab, alog, dt, *, ctx_len=CTX_LEN, base=8):
    b, _, tok, _ = qk.shape
    nc = tok // CHUNK
    ncc = ctx_len // CHUNK

    def chunk(dd, s):
        back = jnp.where(s < ncc, ncc - 1 - s, nc - 1 - (s - ncc))
        return jnp.where(dd == 0, s, back)

    return pl.pallas_call(
        functools.partial(_gdn_scan_kernel, base=base),
        out_shape=jax.ShapeDtypeStruct((2, b, GDN_V_HEADS, tok, LANES), BF16),
        grid=(b, 2, nc),
        in_specs=[
            pl.BlockSpec((1, GDN_QK_HEADS, CHUNK, LANES), lambda bi, dd, s: (bi, 0, chunk(dd, s), 0)),
            pl.BlockSpec((1, GDN_QK_HEADS, CHUNK, LANES), lambda bi, dd, s: (bi, 1, chunk(dd, s), 0)),
            pl.BlockSpec((1, GDN_V_HEADS, CHUNK, LANES), lambda bi, dd, s: (bi, 0, chunk(dd, s), 0)),
            pl.BlockSpec((1, CHUNK, LANES), lambda bi, dd, s: (bi, chunk(dd, s), dd)),
            pl.BlockSpec((1, 1, LANES), lambda bi, dd, s: (dd, 0, 0)),
            pl.BlockSpec((1, 1, LANES), lambda bi, dd, s: (dd, 0, 0)),
        ],
        out_specs=pl.BlockSpec((1, 1, GDN_V_HEADS, CHUNK, LANES),
                               lambda bi, dd, s: (dd, bi, 0, chunk(dd, s), 0)),
        scratch_shapes=[
            pltpu.VMEM((GDN_V_HEADS, GDN_DIM, GDN_DIM), F32),
            pltpu.VMEM((GDN_V_HEADS, CHUNK, LANES), F32),
            pltpu.VMEM((GDN_V_HEADS, CHUNK, LANES), F32),
            pltpu.VMEM((GDN_QK_HEADS, LANES), F32),
            pltpu.VMEM((GDN_QK_HEADS, LANES), F32),
            pltpu.VMEM((GDN_V_HEADS, LANES), F32),
        ],
        compiler_params=_params(("parallel", "parallel", "arbitrary")),
        name="gdn_chunk_scan",
    )(qk, qk, v, ab, alog, dt)


def _final_norm_kernel(x_ref, w_ref, o_ref):
    xv = x_ref[0]
    o_ref[0] = xv * lax.rsqrt(jnp.mean(xv * xv, axis=-1, keepdims=True) + EPS) * w_ref[...]


def _final_norm_call(x, w, *, ctx_len=CTX_LEN):
    b, tok, d = x.shape
    tr = ctx_len
    return pl.pallas_call(
        _final_norm_kernel,
        out_shape=jax.ShapeDtypeStruct((b, tok - ctx_len, d), F32),
        grid=(b, (tok - ctx_len) // tr),
        in_specs=[pl.BlockSpec((1, tr, d), lambda bi, t: (bi, t + 1, 0)),
                  pl.BlockSpec((1, d), lambda bi, t: (0, 0))],
        out_specs=pl.BlockSpec((1, tr, d), lambda bi, t: (bi, t, 0)),
        compiler_params=_params(("parallel", "parallel")),
        name="final_rmsnorm",
    )(x, w.reshape(1, d))


def _rope_tables(seq, ctx_len, width):
    t = jnp.arange(seq, dtype=jnp.int32)
    rows = (t // GRID_W).astype(F32)
    cols = (t % GRID_W).astype(F32)
    n_freq = DA_QK_DIM // 4
    inv_freq = ROPE_BASE ** (-jnp.arange(n_freq, dtype=F32) / n_freq)
    ar = rows[:, None] * inv_freq
    ac = cols[:, None] * inv_freq
    cos = jnp.concatenate([jnp.cos(ar), jnp.cos(ar), jnp.cos(ac), jnp.cos(ac)], axis=1)
    sin = jnp.concatenate([-jnp.sin(ar), jnp.sin(ar), -jnp.sin(ac), jnp.sin(ac)], axis=1)
    cos = jnp.concatenate([jnp.ones((ctx_len, DA_QK_DIM), F32), cos], axis=0)
    sin = jnp.concatenate([jnp.zeros((ctx_len, DA_QK_DIM), F32), sin], axis=0)
    reps = width // DA_QK_DIM
    return jnp.tile(cos, (1, reps)), jnp.tile(sin, (1, reps))


def kernel(x, c, ctx, c_ctx, w_mod, b_mod, norm_mix, norm_ffn, da_w_qkv, da_lambda, da_head_gain, da_w_o,
           gdn_w_in, gdn_conv, gdn_a_log, gdn_dt_bias, gdn_norm_gain, gdn_w_o, ffn_w_up, ffn_conv,
           ffn_w_down, final_norm):
    bsz, seq, d = x.shape
    ctx_len = ctx.shape[1]
    tm = (ctx_len + seq) // 3
    tm_out = (ctx_len + seq) // 2
    tn_wide = 1024
    tn_ffn = 512
    sw_conv = 512

    xa = jnp.concatenate([ctx, x], axis=1)
    pad_rows = 16 - (bsz + 1)
    cvec = jnp.concatenate([c, c_ctx[None, :], jnp.zeros((pad_rows, d), F32)], axis=0)
    mods_all = _mods_call(cvec, w_mod, b_mod)
    cos_t, sin_t = _rope_tables(seq, ctx_len, MXU_N)

    def sel(mods, idx):
        lat = mods[:bsz, idx * d:(idx + 1) * d]
        cx = jnp.broadcast_to(mods[bsz:bsz + 1, idx * d:(idx + 1) * d], (bsz, d))
        return lat, cx

    def modv(mods, i_shift, i_scale):
        sl, sc = sel(mods, i_shift)
        cl, cc = sel(mods, i_scale)
        return jnp.stack([sl, cl, sc, cc], axis=1)

    def gatev(mods, idx):
        gl, gc = sel(mods, idx)
        return jnp.stack([gl, gc], axis=1)

    for i in range(DEPTH):
        mods = mods_all[i]
        jm = i // 2
        mv = modv(mods, 0, 1)
        if i % 2 == 0:
            lam_init = 0.8 - 0.6 * math.exp(-0.3 * i)
            qkv = _proj_call(xa, mv, norm_mix[i], [da_w_qkv[jm].astype(BF16)], [], epi="rope", tn=tn_wide,
                             tm=tm, extras=(cos_t, sin_t), q_tiles=DA_QK_W // tn_wide, ctx_len=ctx_len)
            att = _diff_attn_call(qkv, da_lambda[jm], da_head_gain[jm], lam_init, ctx_len=ctx_len)
            xa = _oproj_call(att, da_w_o[jm].astype(BF16), xa, gatev(mods, 2), tm=tm_out, tn=512,
                             ctx_len=ctx_len)
        else:
            w_in = gdn_w_in[jm].astype(BF16)
            cw = gdn_conv[jm]
            qk = _proj_call(xa, mv, norm_mix[i], [w_in], [cw], epi="gdn_qk", tn=tn_wide, tm=tm, sw=sw_conv,
                            n=2 * GDN_QK_W, col0=0, q_tiles=GDN_QK_W // tn_wide, ctx_len=ctx_len)
            vv = _proj_call(xa, mv, norm_mix[i], [w_in], [cw], epi="gdn_v", tn=tn_wide, tm=tm, sw=sw_conv,
                            n=GDN_V_W, col0=2 * GDN_QK_W, ctx_len=ctx_len)
            z = _proj_call(xa, mv, norm_mix[i], [w_in], [], epi="plain", tn=tn_wide, tm=tm,
                           n=GDN_V_W, col0=GDN_QKV_W, ctx_len=ctx_len)
            w_ab = w_in[:, GDN_QKV_W + GDN_V_W:].reshape(d, 2, 2 * GDN_V_HEADS)
            w_ab = jnp.concatenate([w_ab, jnp.zeros_like(w_ab)], axis=2).reshape(d, 2 * LANES)
            ab = _proj_call(xa, mv, norm_mix[i], [w_ab], [], epi="plain", tn=2 * LANES, tm=tm,
                            out_dtype=F32, ctx_len=ctx_len)
            zeros = jnp.zeros((2, GDN_V_HEADS), F32)
            alog = jnp.concatenate([zeros, gdn_a_log[jm], zeros, zeros], axis=1).reshape(2, 1, LANES)
            dtb = jnp.concatenate([zeros, gdn_dt_bias[jm], zeros, zeros], axis=1).reshape(2, 1, LANES)
            o = _gdn_scan_call(qk, vv, ab, alog, dtb, ctx_len=ctx_len)
            y = _gdn_gate_call(o, z, gdn_norm_gain[jm], tr=ctx_len)
            xa = _oproj_call(y, gdn_w_o[jm].astype(BF16), xa, gatev(mods, 2), tm=tm_out, tn=512,
                             ctx_len=ctx_len)
        padc = D_FF_PAD - D_FF
        w_up = ffn_w_up[i].astype(BF16)
        w_g = jnp.pad(w_up[:, :D_FF], ((0, 0), (0, padc)))
        w_v = jnp.pad(w_up[:, D_FF:], ((0, 0), (0, padc)))
        cwf = ffn_conv[i]
        cw_g = jnp.pad(cwf[:, :D_FF], ((0, 0), (0, padc)))
        cw_v = jnp.pad(cwf[:, D_FF:], ((0, 0), (0, padc)))
        act = _proj_call(xa, modv(mods, 3, 4), norm_ffn[i], [w_g, w_v], [cw_g, cw_v], epi="ffn", tn=tn_ffn,
                         tm=tm, sw=sw_conv, ctx_len=ctx_len)
        w_dn = jnp.pad(ffn_w_down[i].astype(BF16), ((0, padc), (0, 0)))
        xa = _oproj_call(act, w_dn, xa, gatev(mods, 5), tm=tm_out, tn=512, ctx_len=ctx_len)
    return _final_norm_call(xa, final_norm, ctx_len=ctx_len)
```

```python
import functools
import math

import jax
import jax.numpy as jnp
from jax import lax
from jax.experimental import pallas as pl
from jax.experimental.pallas import tpu as pltpu

F32 = jnp.float32
BF16 = jnp.bfloat16

D_MODEL = 2048
DEPTH = 4
CTX_LEN = 256
GRID_W = 64
EPS = 1e-6
DA_HEADS = 8
DA_QK_DIM = 128
DA_V_DIM = 256
DA_QK_W = DA_HEADS * 2 * DA_QK_DIM
DA_V_W = DA_HEADS * DA_V_DIM
ROPE_BASE = 10000.0
GDN_QK_HEADS = 16
GDN_V_HEADS = 32
GDN_DIM = 128
GDN_QK_W = GDN_QK_HEADS * GDN_DIM
GDN_V_W = GDN_V_HEADS * GDN_DIM
GDN_QKV_W = 2 * GDN_QK_W + GDN_V_W
GDN_CONV = 5
CHUNK = 64
D_FF = 5504
FFN_CONV = 3

LANES = 128
HALO = 16
D_FF_PAD = 5632
VMEM_LIMIT = 56 * 1024 * 1024
LOG2E = math.log2(math.e)

NT_DIMS = (((1,), (1,)), ((), ()))
TN_DIMS = (((0,), (0,)), ((), ()))


def _silu(x):
    return x * jax.nn.sigmoid(x)


def _params(sem):
    return pltpu.CompilerParams(dimension_semantics=sem, vmem_limit_bytes=VMEM_LIMIT)


def _mods_kernel(c_ref, w_ref, b_ref, o_ref):
    s = _silu(c_ref[...]).astype(BF16)
    w = w_ref[0].astype(BF16)
    o_ref[0] = jnp.dot(s, w, preferred_element_type=F32) + b_ref[0]


def _mods_call(cvec, w_mod, b_mod):
    depth, d, n = w_mod.shape
    tn = 1024
    return pl.pallas_call(
        _mods_kernel,
        out_shape=jax.ShapeDtypeStruct((depth, cvec.shape[0], n), F32),
        grid=(depth, n // tn),
        in_specs=[
            pl.BlockSpec((cvec.shape[0], d), lambda i, j: (0, 0)),
            pl.BlockSpec((1, d, tn), lambda i, j: (i, 0, j)),
            pl.BlockSpec((1, 1, tn), lambda i, j: (i, 0, j)),
        ],
        out_specs=pl.BlockSpec((1, cvec.shape[0], tn), lambda i, j: (i, 0, j)),
        compiler_params=_params(("parallel", "parallel")),
        name="adaln_mods",
    )(cvec, w_mod, b_mod.reshape(depth, 1, n))


MXU_N = 256
EPI_ROWS = 128
NORM_ROWS = 16


def _proj_kernel(*refs, n_w, taps, epi, tm, tn, sw, ctx_len, tok, q_tiles):
    halo = taps > 0
    ns = tn // sw
    it = iter(refs)
    x_ref = next(it)
    xp_ref = next(it) if halo else None
    xn_ref = next(it) if halo else None
    modv_ref = next(it)
    nw_ref = next(it)
    w_refs = [next(it) for _ in range(n_w)]
    cw_refs = [next(it) for _ in range(n_w)] if halo else []
    ex_refs = [next(it) for _ in range(2)] if epi == "rope" else []
    out_ref = next(it)
    h_ref = next(it)
    u_refs = [[next(it) for _ in range(ns)] for _ in range(n_w)]

    t = pl.program_id(1)
    j = pl.program_id(2)
    off = HALO if halo else 0
    p = taps // 2

    @pl.when(j == 0)
    def _():
        nw = nw_ref[...]
        gain_l = nw * (1.0 + modv_ref[0, 1:2])
        gain_c = nw * (1.0 + modv_ref[0, 3:4])
        shift_l = modv_ref[0, 0:1]
        shift_c = modv_ref[0, 2:3]

        def norm_rows(src_ref, n_rows, dst0, g0, keep):
            def body(i, carry):
                r = pl.multiple_of(i * NORM_ROWS, NORM_ROWS)
                xv = src_ref[0, pl.ds(r, NORM_ROWS), :]
                inv = lax.rsqrt(jnp.mean(xv * xv, axis=-1, keepdims=True) + EPS)
                is_ctx = (g0 + r + lax.broadcasted_iota(jnp.int32, (NORM_ROWS, 1), 0)) < ctx_len
                hv = (xv * inv) * jnp.where(is_ctx, gain_c, gain_l) + jnp.where(is_ctx, shift_c, shift_l)
                if keep is not None:
                    hv = jnp.where(keep, hv, 0.0)
                h_ref[pl.ds(dst0 + r, NORM_ROWS), :] = hv.astype(BF16)
                return carry
            lax.fori_loop(0, n_rows // NORM_ROWS, body, 0, unroll=2 if n_rows > NORM_ROWS else 1)

        norm_rows(x_ref, tm, off, t * tm, None)
        if halo:
            keep_prev = (t * tm != 0) & (t * tm != ctx_len)
            keep_next = (t * tm + tm != tok) & (t * tm + tm != ctx_len)
            norm_rows(xp_ref, HALO, 0, t * tm - HALO, keep_prev)
            norm_rows(xn_ref, HALO, off + tm, t * tm + tm, keep_next)

    for s in range(ns):
        for w, u_s in zip(w_refs, u_refs):
            u_s[s][...] = jnp.dot(h_ref[...], w[:, s * sw:(s + 1) * sw], preferred_element_type=F32)

    def conv_rows(u_ref, cw_ref, s, r0, nrows, edge):
        if not halo:
            return u_ref[r0:r0 + nrows, :]
        acc = None
        for jj in range(taps):
            dlt = jj - p
            sh = u_ref[off + r0 + dlt:off + r0 + dlt + nrows, :]
            if edge is not None and dlt != 0:
                ri = r0 + lax.broadcasted_iota(jnp.int32, (nrows, 1), 0)
                same = ((ri >= edge) & (ri + dlt >= edge)) | ((ri < edge) & (ri + dlt < edge))
                sh = jnp.where(same, sh, 0.0)
            term = sh * cw_ref[jj:jj + 1, s * sw:(s + 1) * sw]
            acc = term if acc is None else acc + term
        return acc

    def finish(s, r0, nrows, edge=None):
        vals = [conv_rows(u_s[s], cw, s, r0, nrows, edge)
                for u_s, cw in zip(u_refs, cw_refs if halo else [None] * n_w)]
        rs = slice(r0, r0 + nrows)
        cs = slice(s * sw, (s + 1) * sw)
        hpt = sw // LANES
        if epi == "ffn":
            out_ref[0, rs, cs] = (_silu(vals[0]) * vals[1]).astype(out_ref.dtype)
        elif epi == "plain":
            out_ref[0, rs, cs] = vals[0].astype(out_ref.dtype)
        elif epi == "gdn_v":
            a = _silu(vals[0])
            for hh in range(hpt):
                out_ref[0, s * hpt + hh, rs] = a[:, hh * LANES:(hh + 1) * LANES].astype(out_ref.dtype)
        elif epi == "gdn_qk":
            a = _silu(vals[0])
            qs = jnp.where(j < q_tiles, GDN_DIM ** -0.5, 1.0).astype(F32)
            for hh in range(hpt):
                xs = a[:, hh * LANES:(hh + 1) * LANES]
                nrm = xs * lax.rsqrt(jnp.sum(xs * xs, axis=-1, keepdims=True) + EPS)
                out_ref[0, s * hpt + hh, rs] = (nrm * qs).astype(out_ref.dtype)
        elif epi == "rope":
            u = vals[0]
            cos = jnp.where(j < 2 * q_tiles, ex_refs[0][rs, :], 1.0)
            sin = jnp.where(j < 2 * q_tiles, ex_refs[1][rs, :], 0.0)
            lane = lax.broadcasted_iota(jnp.int32, (1, sw), 1)
            swapped = jnp.where((lane % 64) < 32, pltpu.roll(u, sw - 32, 1), pltpu.roll(u, 32, 1))
            sc = jnp.where(j < q_tiles, DA_QK_DIM ** -0.5 * LOG2E, 1.0).astype(F32)
            out_ref[0, rs, cs] = ((u * cos + swapped * sin) * sc).astype(out_ref.dtype)
        else:
            raise ValueError(epi)

    for s in range(ns):
        for r0 in range(0, tm, EPI_ROWS):
            finish(s, r0, EPI_ROWS)

    if halo and ctx_len % tm != 0:
        edge = ctx_len % tm

        @pl.when(t == ctx_len // tm)
        def _():
            for s in range(ns):
                finish(s, edge - HALO, 2 * HALO, edge)


def _proj_call(x, modv, nw, ws, cws, *, epi, tn, tm, sw=MXU_N, n=None, col0=0, out_dtype=BF16, extras=(),
               q_tiles=0, ctx_len=CTX_LEN):
    b, tok, d = x.shape
    n = ws[0].shape[1] if n is None else n
    j0 = col0 // tn
    taps = cws[0].shape[0] if cws else 0
    halo = taps > 0
    nt = tok // tm
    hb = tm // HALO
    nhb = tok // HALO

    in_specs = [pl.BlockSpec((1, tm, d), lambda bi, t, j: (bi, t, 0))]
    args = [x]
    if halo:
        in_specs += [
            pl.BlockSpec((1, HALO, d), lambda bi, t, j: (bi, jnp.maximum(t * hb - 1, 0), 0)),
            pl.BlockSpec((1, HALO, d), lambda bi, t, j: (bi, jnp.minimum((t + 1) * hb, nhb - 1), 0)),
        ]
        args += [x, x]
    in_specs += [pl.BlockSpec((1, 4, d), lambda bi, t, j: (bi, 0, 0)),
                 pl.BlockSpec((1, d), lambda bi, t, j: (0, 0))]
    args += [modv, nw.reshape(1, d)]
    for w in ws:
        in_specs.append(pl.BlockSpec((d, tn), lambda bi, t, j: (0, j0 + j)))
        args.append(w)
    for cw in cws:
        in_specs.append(pl.BlockSpec((taps, tn), lambda bi, t, j: (0, j0 + j)))
        args.append(cw)
    for e in extras:
        in_specs.append(pl.BlockSpec((tm, sw), lambda bi, t, j: (t, 0)))
        args.append(e)

    if epi in ("gdn_v", "gdn_qk"):
        out_shape = jax.ShapeDtypeStruct((b, n // LANES, tok, LANES), out_dtype)
        out_spec = pl.BlockSpec((1, tn // LANES, tm, LANES), lambda bi, t, j: (bi, j, t, 0))
    else:
        out_shape = jax.ShapeDtypeStruct((b, tok, n), out_dtype)
        out_spec = pl.BlockSpec((1, tm, tn), lambda bi, t, j: (bi, t, j))

    kern = functools.partial(_proj_kernel, n_w=len(ws), taps=taps, epi=epi, tm=tm, tn=tn, sw=sw,
                             ctx_len=ctx_len, tok=tok, q_tiles=q_tiles)
    return pl.pallas_call(
        kern,
        out_shape=out_shape,
        grid=(b, nt, n // tn),
        in_specs=in_specs,
        out_specs=out_spec,
        scratch_shapes=([pltpu.VMEM((tm + (2 * HALO if halo else 0), d), BF16)]
                        + [pltpu.VMEM((tm + (2 * HALO if halo else 0), sw), F32)
                           for _ in range(len(ws) * (tn // sw))]),
        compiler_params=_params(("parallel", "parallel", "arbitrary")),
        name="norm_proj_" + epi,
    )(*args)


def _row_gate(g_ref, t, tm, ctx_len):
    g = t * tm + lax.broadcasted_iota(jnp.int32, (tm, 1), 0)
    return jnp.where(g < ctx_len, g_ref[0, 1:2], g_ref[0, 0:1])


def _oproj_kernel(y_ref, w_ref, x_ref, g_ref, o_ref, *, tm, ctx_len):
    acc = jnp.dot(y_ref[0], w_ref[...], preferred_element_type=F32)
    o_ref[0] = x_ref[0] + _row_gate(g_ref, pl.program_id(1), tm, ctx_len) * acc


def _oproj_call(y, w, x, gate, *, tm, tn, ctx_len=CTX_LEN):
    b, tok, d = x.shape
    k = y.shape[-1]
    return pl.pallas_call(
        functools.partial(_oproj_kernel, tm=tm, ctx_len=ctx_len),
        out_shape=jax.ShapeDtypeStruct(x.shape, F32),
        grid=(b, tok // tm, d // tn),
        in_specs=[
            pl.BlockSpec((1, tm, k), lambda bi, t, j: (bi, t, 0)),
            pl.BlockSpec((k, tn), lambda bi, t, j: (0, j)),
            pl.BlockSpec((1, tm, tn), lambda bi, t, j: (bi, t, j)),
            pl.BlockSpec((1, 2, tn), lambda bi, t, j: (bi, 0, j)),
        ],
        out_specs=pl.BlockSpec((1, tm, tn), lambda bi, t, j: (bi, t, j)),
        compiler_params=_params(("parallel", "parallel", "arbitrary")),
        name="out_proj_residual",
    )(y, w, x, gate)


def _gdn_gate_kernel(o_ref, z_ref, gain_ref, y_ref):
    gain = gain_ref[...]
    for h in range(GDN_V_HEADS):
        o = o_ref[0, 0, h].astype(F32) + o_ref[1, 0, h].astype(F32)
        y = o * lax.rsqrt(jnp.mean(o * o, axis=-1, keepdims=True) + EPS) * gain
        z = z_ref[0, :, h * LANES:(h + 1) * LANES].astype(F32)
        y_ref[0, :, h * LANES:(h + 1) * LANES] = (y * _silu(z)).astype(BF16)


def _gdn_gate_call(o, z, gain, *, tr):
    b, tok, k = z.shape
    return pl.pallas_call(
        _gdn_gate_kernel,
        out_shape=jax.ShapeDtypeStruct(z.shape, BF16),
        grid=(b, tok // tr),
        in_specs=[
            pl.BlockSpec((2, 1, GDN_V_HEADS, tr, LANES), lambda bi, t: (0, bi, 0, t, 0)),
            pl.BlockSpec((1, tr, k), lambda bi, t: (bi, t, 0)),
            pl.BlockSpec((1, LANES), lambda bi, t: (0, 0)),
        ],
        out_specs=pl.BlockSpec((1, tr, k), lambda bi, t: (bi, t, 0)),
        compiler_params=_params(("parallel", "parallel")),
        name="gdn_gated_norm",
    )(o, z, gain.reshape(1, LANES))


ATT_HEADS = 2


def _diff_attn_kernel(q_ref, k_ref, v_ref, lam_ref, gain_ref, o_ref, *, lam_init, ctx_len):
    lv = lam_ref[...]
    lam = (jnp.exp(jnp.sum(lv[0:1] * lv[1:2], axis=-1, keepdims=True))
           - jnp.exp(jnp.sum(lv[2:3] * lv[3:4], axis=-1, keepdims=True)) + lam_init)

    def attend(kk, vv):
        q = q_ref[0]
        scores = []
        for hc in range(2 * ATT_HEADS):
            cs = slice(hc * DA_QK_DIM, (hc + 1) * DA_QK_DIM)
            scores.append(lax.dot_general(q[:, cs], kk[:, cs], NT_DIMS, preferred_element_type=F32))
        for h in range(ATT_HEADS):
            ps, ls = [], []
            for c in range(2):
                s = scores[2 * h + c]
                p = jnp.exp2(s - jnp.max(s, axis=-1, keepdims=True))
                ps.append(p)
                ls.append(jnp.sum(p, axis=-1, keepdims=True))
            inv0 = 1.0 / ls[0]
            a = ps[0] - (lam * ls[0] * (1.0 / ls[1])) * ps[1]
            vs = slice(h * DA_V_DIM, (h + 1) * DA_V_DIM)
            o = jnp.dot(a.astype(BF16), vv[:, vs], preferred_element_type=F32) * inv0
            o = o * lax.rsqrt(jnp.mean(o * o, axis=-1, keepdims=True) + EPS) * gain_ref[...]
            o_ref[0, :, vs] = (o * (1.0 - lam_init)).astype(o_ref.dtype)

    t = pl.program_id(2)

    @pl.when(t == 0)
    def _():
        attend(k_ref[0, 0:ctx_len], v_ref[0, 0:ctx_len])

    @pl.when(t != 0)
    def _():
        attend(k_ref[0], v_ref[0])


def _diff_attn_call(qkv, lam_vecs, head_gain, lam_init, *, ctx_len=CTX_LEN):
    b, tok, _ = qkv.shape
    hw = ATT_HEADS * 2 * DA_QK_DIM
    ng = DA_HEADS // ATT_HEADS
    tq = ctx_len
    return pl.pallas_call(
        functools.partial(_diff_attn_kernel, lam_init=lam_init, ctx_len=ctx_len),
        out_shape=jax.ShapeDtypeStruct((b, tok, DA_V_W), BF16),
        grid=(b, ng, tok // tq),
        in_specs=[
            pl.BlockSpec((1, tq, hw), lambda bi, h, t: (bi, t, h)),
            pl.BlockSpec((1, tok, hw), lambda bi, h, t: (bi, 0, ng + h)),
            pl.BlockSpec((1, tok, hw), lambda bi, h, t: (bi, 0, 2 * ng + h)),
            pl.BlockSpec((4, DA_QK_DIM), lambda bi, h, t: (0, 0)),
            pl.BlockSpec((1, DA_V_DIM), lambda bi, h, t: (0, 0)),
        ],
        out_specs=pl.BlockSpec((1, tq, hw), lambda bi, h, t: (bi, t, h)),
        compiler_params=_params(("parallel", "parallel", "arbitrary")),
        name="diff_attention",
    )(qkv, qkv, qkv, lam_vecs, head_gain.reshape(1, DA_V_DIM))


def _split3(x):
    x1 = x.astype(BF16)
    r1 = x - x1.astype(F32)
    x2 = r1.astype(BF16)
    x3 = (r1 - x2.astype(F32)).astype(BF16)
    return x1, x2, x3


def _blockdiag(x2):
    first = lax.broadcasted_iota(jnp.int32, x2.shape, 1) < CHUNK
    return jnp.concatenate([jnp.where(first, x2, 0.0), jnp.where(first, 0.0, x2)], axis=0)


def _unit_lower_inverses(a2s, base):
    c = CHUNK
    nt = c // base
    assert base == 8
    lane = lax.broadcasted_iota(jnp.int32, (base, LANES), 1)
    sub = lax.broadcasted_iota(jnp.int32, (base, LANES), 0)
    row2 = lax.broadcasted_iota(jnp.int32, (c, LANES), 0)
    col2 = lax.broadcasted_iota(jnp.int32, (c, LANES), 1) % c

    in_blk = [((lane % c) // base) == r for r in range(nt)]
    dgs = []
    for a2 in a2s:
        dg = jnp.where(in_blk[0], a2[0:base], 0.0)
        for r in range(1, nt):
            dg = jnp.where(in_blk[r], a2[base * r:base * (r + 1)], dg)
        dgs.append(dg)
    ts = [jnp.where((lane % base) == sub, 1.0, 0.0).astype(F32) for _ in a2s]
    for jcol in range(base - 1):
        idx = (lane // base) * base + jcol
        for i, dg in enumerate(dgs):
            acol = jnp.take_along_axis(dg, idx, axis=1)
            ts[i] = ts[i] - acol * ts[i][jcol:jcol + 1, :]
    tbds = [_blockdiag(jnp.concatenate([jnp.where(in_blk[r], t, 0.0) for r in range(nt)], axis=0))
            for t in ts]

    size = base
    while size < c:
        off = ((row2 // (2 * size)) == (col2 // (2 * size))) & ((row2 // size) != (col2 // size))
        tbs = [tbd.astype(BF16) for tbd in tbds]
        tqs = [jnp.dot(tb, _blockdiag(jnp.where(off, a2, 0.0)).astype(BF16), preferred_element_type=F32)
               for tb, a2 in zip(tbs, a2s)]
        tbds = [tbd - jnp.dot(tq.astype(BF16), tb, preferred_element_type=F32)
                for tbd, tq, tb in zip(tbds, tqs, tbs)]
        size *= 2
    return tbds


def _gdn_scan_kernel(q_ref, k_ref, v_ref, ab_ref, alog_ref, dt_ref, o_ref,
                     s_ref, cg_ref, cb_ref, rg_ref, rb_ref, gt_ref, *, base):
    c = CHUNK
    d = pl.program_id(1)
    step = pl.program_id(2)
    rev = d == 1

    @pl.when(step == 0)
    def _():
        s_ref[...] = jnp.zeros(s_ref.shape, F32)

    ab = ab_ref[0]
    beta_all = jax.nn.sigmoid(ab)
    xs = ab + dt_ref[0]
    g_all = -jnp.exp(alog_ref[0]) * (jnp.maximum(xs, 0.0) + jnp.log1p(jnp.exp(-jnp.abs(xs))))
    ri = lax.broadcasted_iota(jnp.int32, (c, c), 0)
    ci = lax.broadcasted_iota(jnp.int32, (c, c), 1)
    sgn = 1 - 2 * d
    cum = jnp.where((ci - ri) * sgn <= 0, 1.0, 0.0).astype(BF16)
    gam_all = sum(jnp.dot(cum, part, preferred_element_type=F32) for part in _split3(g_all))
    gtot_all = jnp.sum(g_all, axis=0, keepdims=True)

    pi = lax.broadcasted_iota(jnp.int32, (GDN_QK_HEADS, LANES), 0)
    li = lax.broadcasted_iota(jnp.int32, (GDN_QK_HEADS, LANES), 1)
    zpad = jnp.zeros((c, LANES), BF16)

    def pair_rows(x, col0):
        sel_e = jnp.where(li == col0 + 2 * pi, 1.0, 0.0).astype(BF16)
        sel_o = jnp.where(li == col0 + 2 * pi + 1, 1.0, 0.0).astype(BF16)
        acc = jnp.zeros((GDN_QK_HEADS, LANES), F32)
        for part in _split3(x):
            acc = acc + lax.dot_general(sel_e, jnp.concatenate([part, zpad], axis=0), NT_DIMS,
                                        preferred_element_type=F32)
            acc = acc + lax.dot_general(sel_o, jnp.concatenate([zpad, part], axis=0), NT_DIMS,
                                        preferred_element_type=F32)
        return acc

    rg_ref[...] = pair_rows(gam_all, GDN_V_HEADS)
    rb_ref[...] = pair_rows(beta_all, 0)
    for h in range(GDN_V_HEADS):
        cg_ref[h] = jnp.broadcast_to(gam_all[:, GDN_V_HEADS + h:GDN_V_HEADS + h + 1], (c, LANES))
        cb_ref[h] = jnp.broadcast_to(beta_all[:, h:h + 1], (c, LANES))
        gt_ref[h:h + 1, :] = jnp.broadcast_to(gtot_all[:, GDN_V_HEADS + h:GDN_V_HEADS + h + 1], (1, LANES))

    row2 = lax.broadcasted_iota(jnp.int32, (c, LANES), 0)
    lane2 = lax.broadcasted_iota(jnp.int32, (c, LANES), 1)
    col2 = lane2 % c
    first2 = lane2 < c

    pairs = range(GDN_QK_HEADS)
    heads = range(GDN_V_HEADS)

    qkks = []
    for p in pairs:
        q = q_ref[0, p]
        k = k_ref[0, p]
        qkks.append(lax.dot_general(jnp.concatenate([q, k], axis=0), jnp.concatenate([k, k], axis=0),
                                    NT_DIMS, preferred_element_type=F32))
    a2s, qkm2s = [], []
    for p in pairs:
        gc2 = jnp.where(first2, cg_ref[2 * p], cg_ref[2 * p + 1])
        bc2 = jnp.where(first2, cb_ref[2 * p], cb_ref[2 * p + 1])
        gr2 = rg_ref[p:p + 1, :]
        br2 = rb_ref[p:p + 1, :]
        e2 = jnp.exp(-jnp.abs(gc2 - gr2))
        a2s.append(jnp.where(row2 > col2, jnp.where(rev, br2, bc2) * qkks[p][c:2 * c] * e2, 0.0))
        qkm2s.append(jnp.where((row2 - col2) * sgn >= 0, qkks[p][0:c] * e2, 0.0))

    tbds = _unit_lower_inverses(a2s, base)

    uws = []
    for p in pairs:
        kf = k_ref[0, p].astype(F32)
        rhs = []
        for h in (2 * p, 2 * p + 1):
            cb = cb_ref[h]
            vb = v_ref[0, h].astype(F32) * cb
            kb = kf * (cb * jnp.exp(cg_ref[h]))
            rhs.append(jnp.concatenate([vb, kb], axis=1).astype(BF16))
        tuse = jnp.where(rev, tbds[p].T, tbds[p]).astype(BF16)
        uws.append(jnp.dot(tuse, jnp.concatenate(rhs, axis=0), preferred_element_type=F32))

    ws_qs = []
    for h in heads:
        w = uws[h // 2][(h % 2) * c:(h % 2 + 1) * c, GDN_DIM:2 * GDN_DIM]
        qd = q_ref[0, h // 2].astype(F32) * jnp.exp(cg_ref[h])
        wq = jnp.concatenate([w.astype(BF16), qd.astype(BF16)], axis=0)
        ws_qs.append(jnp.dot(wq, s_ref[h].astype(BF16), preferred_element_type=F32))

    for h in heads:
        u = uws[h // 2][(h % 2) * c:(h % 2 + 1) * c, 0:GDN_DIM]
        v_new = (u - ws_qs[h][0:c]).astype(BF16)
        qkm = qkm2s[h // 2][:, (h % 2) * c:(h % 2 + 1) * c].astype(BF16)
        o = ws_qs[h][c:2 * c] + jnp.dot(qkm, v_new, preferred_element_type=F32)
        o_ref[0, 0, h] = o.astype(o_ref.dtype)
        gt = gt_ref[h:h + 1, :]
        k_dec = (k_ref[0, h // 2].astype(F32) * jnp.exp(gt - cg_ref[h])).astype(BF16)
        s_ref[h] = s_ref[h] * jnp.exp(gt) + lax.dot_general(k_dec, v_new, TN_DIMS,
                                                           preferred_element_type=F32)


def _gdn_scan_call(qk, v, ab, alog, dt, *, ctx_len=CTX_LEN, base=8):
    b, _, tok, _ = qk.shape
    nc = tok // CHUNK
    ncc = ctx_len // CHUNK

    def chunk(dd, s):
        back = jnp.where(s < ncc, ncc - 1 - s, nc - 1 - (s - ncc))
        return jnp.where(dd == 0, s, back)

    return pl.pallas_call(
        functools.partial(_gdn_scan_kernel, base=base),
        out_shape=jax.ShapeDtypeStruct((2, b, GDN_V_HEADS, tok, LANES), BF16),
        grid=(b, 2, nc),
        in_specs=[
            pl.BlockSpec((1, GDN_QK_HEADS, CHUNK, LANES), lambda bi, dd, s: (bi, 0, chunk(dd, s), 0)),
            pl.BlockSpec((1, GDN_QK_HEADS, CHUNK, LANES), lambda bi, dd, s: (bi, 1, chunk(dd, s), 0)),
            pl.BlockSpec((1, GDN_V_HEADS, CHUNK, LANES), lambda bi, dd, s: (bi, 0, chunk(dd, s), 0)),
            pl.BlockSpec((1, CHUNK, LANES), lambda bi, dd, s: (bi, chunk(dd, s), dd)),
            pl.BlockSpec((1, 1, LANES), lambda bi, dd, s: (dd, 0, 0)),
            pl.BlockSpec((1, 1, LANES), lambda bi, dd, s: (dd, 0, 0)),
        ],
        out_specs=pl.BlockSpec((1, 1, GDN_V_HEADS, CHUNK, LANES),
                               lambda bi, dd, s: (dd, bi, 0, chunk(dd, s), 0)),
        scratch_shapes=[
            pltpu.VMEM((GDN_V_HEADS, GDN_DIM, GDN_DIM), F32),
            pltpu.VMEM((GDN_V_HEADS, CHUNK, LANES), F32),
            pltpu.VMEM((GDN_V_HEADS, CHUNK, LANES), F32),
            pltpu.VMEM((GDN_QK_HEADS, LANES), F32),
            pltpu.VMEM((GDN_QK_HEADS, LANES), F32),
            pltpu.VMEM((GDN_V_HEADS, LANES), F32),
        ],
        compiler_params=_params(("parallel", "parallel", "arbitrary")),
        name="gdn_chunk_scan",
    )(qk, qk, v, ab, alog, dt)


def _final_norm_kernel(x_ref, w_ref, o_ref):
    xv = x_ref[0]
    o_ref[0] = xv * lax.rsqrt(jnp.mean(xv * xv, axis=-1, keepdims=True) + EPS) * w_ref[...]


def _final_norm_call(x, w, *, ctx_len=CTX_LEN):
    b, tok, d = x.shape
    tr = ctx_len
    return pl.pallas_call(
        _final_norm_kernel,
        out_shape=jax.ShapeDtypeStruct((b, tok - ctx_len, d), F32),
        grid=(b, (tok - ctx_len) // tr),
        in_specs=[pl.BlockSpec((1, tr, d), lambda bi, t: (bi, t + 1, 0)),
                  pl.BlockSpec((1, d), lambda bi, t: (0, 0))],
        out_specs=pl.BlockSpec((1, tr, d), lambda bi, t: (bi, t, 0)),
        compiler_params=_params(("parallel", "parallel")),
        name="final_rmsnorm",
    )(x, w.reshape(1, d))


def _rope_tables(seq, ctx_len, width):
    t = jnp.arange(seq, dtype=jnp.int32)
    rows = (t // GRID_W).astype(F32)
    cols = (t % GRID_W).astype(F32)
    n_freq = DA_QK_DIM // 4
    inv_freq = ROPE_BASE ** (-jnp.arange(n_freq, dtype=F32) / n_freq)
    ar = rows[:, None] * inv_freq
    ac = cols[:, None] * inv_freq
    cos = jnp.concatenate([jnp.cos(ar), jnp.cos(ar), jnp.cos(ac), jnp.cos(ac)], axis=1)
    sin = jnp.concatenate([-jnp.sin(ar), jnp.sin(ar), -jnp.sin(ac), jnp.sin(ac)], axis=1)
    cos = jnp.concatenate([jnp.ones((ctx_len, DA_QK_DIM), F32), cos], axis=0)
    sin = jnp.concatenate([jnp.zeros((ctx_len, DA_QK_DIM), F32), sin], axis=0)
    reps = width // DA_QK_DIM
    return jnp.tile(cos, (1, reps)), jnp.tile(sin, (1, reps))


def kernel(x, c, ctx, c_ctx, w_mod, b_mod, norm_mix, norm_ffn, da_w_qkv, da_lambda, da_head_gain, da_w_o,
           gdn_w_in, gdn_conv, gdn_a_log, gdn_dt_bias, gdn_norm_gain, gdn_w_o, ffn_w_up, ffn_conv,
           ffn_w_down, final_norm):
    bsz, seq, d = x.shape
    ctx_len = ctx.shape[1]
    tm = (ctx_len + seq) // 2
    tm_out = tm
    tn_wide = 1024
    tn_ffn = 512
    sw_conv = 512

    xa = jnp.concatenate([ctx, x], axis=1)
    pad_rows = 16 - (bsz + 1)
    cvec = jnp.concatenate([c, c_ctx[None, :], jnp.zeros((pad_rows, d), F32)], axis=0)
    mods_all = _mods_call(cvec, w_mod, b_mod)
    cos_t, sin_t = _rope_tables(seq, ctx_len, MXU_N)

    def sel(mods, idx):
        lat = mods[:bsz, idx * d:(idx + 1) * d]
        cx = jnp.broadcast_to(mods[bsz:bsz + 1, idx * d:(idx + 1) * d], (bsz, d))
        return lat, cx

    def modv(mods, i_shift, i_scale):
        sl, sc = sel(mods, i_shift)
        cl, cc = sel(mods, i_scale)
        return jnp.stack([sl, cl, sc, cc], axis=1)

    def gatev(mods, idx):
        gl, gc = sel(mods, idx)
        return jnp.stack([gl, gc], axis=1)

    for i in range(DEPTH):
        mods = mods_all[i]
        jm = i // 2
        mv = modv(mods, 0, 1)
        if i % 2 == 0:
            lam_init = 0.8 - 0.6 * math.exp(-0.3 * i)
            qkv = _proj_call(xa, mv, norm_mix[i], [da_w_qkv[jm].astype(BF16)], [], epi="rope", tn=tn_wide,
                             tm=tm, extras=(cos_t, sin_t), q_tiles=DA_QK_W // tn_wide, ctx_len=ctx_len)
            att = _diff_attn_call(qkv, da_lambda[jm], da_head_gain[jm], lam_init, ctx_len=ctx_len)
            xa = _oproj_call(att, da_w_o[jm].astype(BF16), xa, gatev(mods, 2), tm=tm_out, tn=512,
                             ctx_len=ctx_len)
        else:
            w_in = gdn_w_in[jm].astype(BF16)
            cw = gdn_conv[jm]
            qk = _proj_call(xa, mv, norm_mix[i], [w_in], [cw], epi="gdn_qk", tn=tn_wide, tm=tm, sw=sw_conv,
                            n=2 * GDN_QK_W, col0=0, q_tiles=GDN_QK_W // tn_wide, ctx_len=ctx_len)
            vv = _proj_call(xa, mv, norm_mix[i], [w_in], [cw], epi="gdn_v", tn=tn_wide, tm=tm, sw=sw_conv,
                            n=GDN_V_W, col0=2 * GDN_QK_W, ctx_len=ctx_len)
            z = _proj_call(xa, mv, norm_mix[i], [w_in], [], epi="plain", tn=tn_wide, tm=tm,
                           n=GDN_V_W, col0=GDN_QKV_W, ctx_len=ctx_len)
            w_ab = w_in[:, GDN_QKV_W + GDN_V_W:].reshape(d, 2, 2 * GDN_V_HEADS)
            w_ab = jnp.concatenate([w_ab, jnp.zeros_like(w_ab)], axis=2).reshape(d, 2 * LANES)
            ab = _proj_call(xa, mv, norm_mix[i], [w_ab], [], epi="plain", tn=2 * LANES, tm=tm,
                            out_dtype=F32, ctx_len=ctx_len)
            zeros = jnp.zeros((2, GDN_V_HEADS), F32)
            alog = jnp.concatenate([zeros, gdn_a_log[jm], zeros, zeros], axis=1).reshape(2, 1, LANES)
            dtb = jnp.concatenate([zeros, gdn_dt_bias[jm], zeros, zeros], axis=1).reshape(2, 1, LANES)
            o = _gdn_scan_call(qk, vv, ab, alog, dtb, ctx_len=ctx_len)
            y = _gdn_gate_call(o, z, gdn_norm_gain[jm], tr=ctx_len)
            xa = _oproj_call(y, gdn_w_o[jm].astype(BF16), xa, gatev(mods, 2), tm=tm_out, tn=512,
                             ctx_len=ctx_len)
        padc = D_FF_PAD - D_FF
        w_up = ffn_w_up[i].astype(BF16)
        w_g = jnp.pad(w_up[:, :D_FF], ((0, 0), (0, padc)))
        w_v = jnp.pad(w_up[:, D_FF:], ((0, 0), (0, padc)))
        cwf = ffn_conv[i]
        cw_g = jnp.pad(cwf[:, :D_FF], ((0, 0), (0, padc)))
        cw_v = jnp.pad(cwf[:, D_FF:], ((0, 0), (0, padc)))
        act = _proj_call(xa, modv(mods, 3, 4), norm_ffn[i], [w_g, w_v], [cw_g, cw_v], epi="ffn", tn=tn_ffn,
                         tm=tm, sw=sw_conv, ctx_len=ctx_len)
        w_dn = jnp.pad(ffn_w_down[i].astype(BF16), ((0, padc), (0, 0)))
        xa = _oproj_call(act, w_dn, xa, gatev(mods, 5), tm=tm_out, tn=512, ctx_len=ctx_len)
    return _final_norm_call(xa, final_norm, ctx_len=ctx_len)
```

```python
import functools
import math

import jax
import jax.numpy as jnp
from jax import lax
from jax.experimental import pallas as pl
from jax.experimental.pallas import tpu as pltpu

F32 = jnp.float32
BF16 = jnp.bfloat16

D_MODEL = 2048
DEPTH = 4
CTX_LEN = 256
GRID_W = 64
EPS = 1e-6
DA_HEADS = 8
DA_QK_DIM = 128
DA_V_DIM = 256
DA_QK_W = DA_HEADS * 2 * DA_QK_DIM
DA_V_W = DA_HEADS * DA_V_DIM
ROPE_BASE = 10000.0
GDN_QK_HEADS = 16
GDN_V_HEADS = 32
GDN_DIM = 128
GDN_QK_W = GDN_QK_HEADS * GDN_DIM
GDN_V_W = GDN_V_HEADS * GDN_DIM
GDN_QKV_W = 2 * GDN_QK_W + GDN_V_W
GDN_CONV = 5
CHUNK = 64
D_FF = 5504
FFN_CONV = 3

LANES = 128
HALO = 16
D_FF_PAD = 5632
VMEM_LIMIT = 56 * 1024 * 1024
LOG2E = math.log2(math.e)

NT_DIMS = (((1,), (1,)), ((), ()))
TN_DIMS = (((0,), (0,)), ((), ()))


def _silu(x):
    return x * jax.nn.sigmoid(x)


def _params(sem):
    return pltpu.CompilerParams(dimension_semantics=sem, vmem_limit_bytes=VMEM_LIMIT)


def _mods_kernel(c_ref, w_ref, b_ref, o_ref):
    s = _silu(c_ref[...]).astype(BF16)
    w = w_ref[0].astype(BF16)
    o_ref[0] = jnp.dot(s, w, preferred_element_type=F32) + b_ref[0]


def _mods_call(cvec, w_mod, b_mod):
    depth, d, n = w_mod.shape
    tn = 1024
    return pl.pallas_call(
        _mods_kernel,
        out_shape=jax.ShapeDtypeStruct((depth, cvec.shape[0], n), F32),
        grid=(depth, n // tn),
        in_specs=[
            pl.BlockSpec((cvec.shape[0], d), lambda i, j: (0, 0)),
            pl.BlockSpec((1, d, tn), lambda i, j: (i, 0, j)),
            pl.BlockSpec((1, 1, tn), lambda i, j: (i, 0, j)),
        ],
        out_specs=pl.BlockSpec((1, cvec.shape[0], tn), lambda i, j: (i, 0, j)),
        compiler_params=_params(("parallel", "parallel")),
        name="adaln_mods",
    )(cvec, w_mod, b_mod.reshape(depth, 1, n))


MXU_N = 256
EPI_ROWS = 128
NORM_ROWS = 16


def _proj_kernel(*refs, n_w, taps, epi, tm, tn, sw, ctx_len, tok, q_tiles):
    halo = taps > 0
    ns = tn // sw
    it = iter(refs)
    x_ref = next(it)
    xp_ref = next(it) if halo else None
    xn_ref = next(it) if halo else None
    modv_ref = next(it)
    nw_ref = next(it)
    w_refs = [next(it) for _ in range(n_w)]
    cw_refs = [next(it) for _ in range(n_w)] if halo else []
    ex_refs = [next(it) for _ in range(2)] if epi == "rope" else []
    out_ref = next(it)
    h_ref = next(it)
    u_refs = [[next(it) for _ in range(ns)] for _ in range(n_w)]

    t = pl.program_id(1)
    j = pl.program_id(2)
    off = HALO if halo else 0
    p = taps // 2

    @pl.when(j == 0)
    def _():
        nw = nw_ref[...]
        gain_l = nw * (1.0 + modv_ref[0, 1:2])
        gain_c = nw * (1.0 + modv_ref[0, 3:4])
        shift_l = modv_ref[0, 0:1]
        shift_c = modv_ref[0, 2:3]

        def norm_rows(src_ref, n_rows, dst0, g0, keep):
            def body(i, carry):
                r = pl.multiple_of(i * NORM_ROWS, NORM_ROWS)
                xv = src_ref[0, pl.ds(r, NORM_ROWS), :]
                inv = lax.rsqrt(jnp.mean(xv * xv, axis=-1, keepdims=True) + EPS)
                is_ctx = (g0 + r + lax.broadcasted_iota(jnp.int32, (NORM_ROWS, 1), 0)) < ctx_len
                hv = (xv * inv) * jnp.where(is_ctx, gain_c, gain_l) + jnp.where(is_ctx, shift_c, shift_l)
                if keep is not None:
                    hv = jnp.where(keep, hv, 0.0)
                h_ref[pl.ds(dst0 + r, NORM_ROWS), :] = hv.astype(BF16)
                return carry
            lax.fori_loop(0, n_rows // NORM_ROWS, body, 0, unroll=2 if n_rows > NORM_ROWS else 1)

        norm_rows(x_ref, tm, off, t * tm, None)
        if halo:
            keep_prev = (t * tm != 0) & (t * tm != ctx_len)
            keep_next = (t * tm + tm != tok) & (t * tm + tm != ctx_len)
            norm_rows(xp_ref, HALO, 0, t * tm - HALO, keep_prev)
            norm_rows(xn_ref, HALO, off + tm, t * tm + tm, keep_next)

    for s in range(ns):
        for w, u_s in zip(w_refs, u_refs):
            u_s[s][...] = jnp.dot(h_ref[...], w[:, s * sw:(s + 1) * sw], preferred_element_type=F32)

    def conv_rows(u_ref, cw_ref, s, r0, nrows, edge):
        if not halo:
            return u_ref[r0:r0 + nrows, :]
        acc = None
        for jj in range(taps):
            dlt = jj - p
            sh = u_ref[off + r0 + dlt:off + r0 + dlt + nrows, :]
            if edge is not None and dlt != 0:
                ri = r0 + lax.broadcasted_iota(jnp.int32, (nrows, 1), 0)
                same = ((ri >= edge) & (ri + dlt >= edge)) | ((ri < edge) & (ri + dlt < edge))
                sh = jnp.where(same, sh, 0.0)
            term = sh * cw_ref[jj:jj + 1, s * sw:(s + 1) * sw]
            acc = term if acc is None else acc + term
        return acc

    def finish(s, r0, nrows, edge=None):
        vals = [conv_rows(u_s[s], cw, s, r0, nrows, edge)
                for u_s, cw in zip(u_refs, cw_refs if halo else [None] * n_w)]
        rs = slice(r0, r0 + nrows)
        cs = slice(s * sw, (s + 1) * sw)
        hpt = sw // LANES
        if epi == "ffn":
            out_ref[0, rs, cs] = (_silu(vals[0]) * vals[1]).astype(out_ref.dtype)
        elif epi == "plain":
            out_ref[0, rs, cs] = vals[0].astype(out_ref.dtype)
        elif epi == "gdn_v":
            a = _silu(vals[0])
            for hh in range(hpt):
                out_ref[0, s * hpt + hh, rs] = a[:, hh * LANES:(hh + 1) * LANES].astype(out_ref.dtype)
        elif epi == "gdn_qk":
            a = _silu(vals[0])
            qs = jnp.where(j < q_tiles, GDN_DIM ** -0.5, 1.0).astype(F32)
            for hh in range(hpt):
                xs = a[:, hh * LANES:(hh + 1) * LANES]
                nrm = xs * lax.rsqrt(jnp.sum(xs * xs, axis=-1, keepdims=True) + EPS)
                out_ref[0, s * hpt + hh, rs] = (nrm * qs).astype(out_ref.dtype)
        elif epi == "rope":
            u = vals[0]
            cos = jnp.where(j < 2 * q_tiles, ex_refs[0][rs, :], 1.0)
            sin = jnp.where(j < 2 * q_tiles, ex_refs[1][rs, :], 0.0)
            lane = lax.broadcasted_iota(jnp.int32, (1, sw), 1)
            swapped = jnp.where((lane % 64) < 32, pltpu.roll(u, sw - 32, 1), pltpu.roll(u, 32, 1))
            sc = jnp.where(j < q_tiles, DA_QK_DIM ** -0.5 * LOG2E, 1.0).astype(F32)
            out_ref[0, rs, cs] = ((u * cos + swapped * sin) * sc).astype(out_ref.dtype)
        else:
            raise ValueError(epi)

    for s in range(ns):
        for r0 in range(0, tm, EPI_ROWS):
            finish(s, r0, EPI_ROWS)

    if halo and ctx_len % tm != 0:
        edge = ctx_len % tm

        @pl.when(t == ctx_len // tm)
        def _():
            for s in range(ns):
                finish(s, edge - HALO, 2 * HALO, edge)


def _proj_call(x, modv, nw, ws, cws, *, epi, tn, tm, sw=MXU_N, n=None, col0=0, out_dtype=BF16, extras=(),
               q_tiles=0, ctx_len=CTX_LEN):
    b, tok, d = x.shape
    n = ws[0].shape[1] if n is None else n
    j0 = col0 // tn
    taps = cws[0].shape[0] if cws else 0
    halo = taps > 0
    nt = tok // tm
    hb = tm // HALO
    nhb = tok // HALO

    in_specs = [pl.BlockSpec((1, tm, d), lambda bi, t, j: (bi, t, 0))]
    args = [x]
    if halo:
        in_specs += [
            pl.BlockSpec((1, HALO, d), lambda bi, t, j: (bi, jnp.maximum(t * hb - 1, 0), 0)),
            pl.BlockSpec((1, HALO, d), lambda bi, t, j: (bi, jnp.minimum((t + 1) * hb, nhb - 1), 0)),
        ]
        args += [x, x]
    in_specs += [pl.BlockSpec((1, 4, d), lambda bi, t, j: (bi, 0, 0)),
                 pl.BlockSpec((1, d), lambda bi, t, j: (0, 0))]
    args += [modv, nw.reshape(1, d)]
    for w in ws:
        in_specs.append(pl.BlockSpec((d, tn), lambda bi, t, j: (0, j0 + j)))
        args.append(w)
    for cw in cws:
        in_specs.append(pl.BlockSpec((taps, tn), lambda bi, t, j: (0, j0 + j)))
        args.append(cw)
    for e in extras:
        in_specs.append(pl.BlockSpec((tm, sw), lambda bi, t, j: (t, 0)))
        args.append(e)

    if epi in ("gdn_v", "gdn_qk"):
        out_shape = jax.ShapeDtypeStruct((b, n // LANES, tok, LANES), out_dtype)
        out_spec = pl.BlockSpec((1, tn // LANES, tm, LANES), lambda bi, t, j: (bi, j, t, 0))
    else:
        out_shape = jax.ShapeDtypeStruct((b, tok, n), out_dtype)
        out_spec = pl.BlockSpec((1, tm, tn), lambda bi, t, j: (bi, t, j))

    kern = functools.partial(_proj_kernel, n_w=len(ws), taps=taps, epi=epi, tm=tm, tn=tn, sw=sw,
                             ctx_len=ctx_len, tok=tok, q_tiles=q_tiles)
    return pl.pallas_call(
        kern,
        out_shape=out_shape,
        grid=(b, nt, n // tn),
        in_specs=in_specs,
        out_specs=out_spec,
        scratch_shapes=([pltpu.VMEM((tm + (2 * HALO if halo else 0), d), BF16)]
                        + [pltpu.VMEM((tm + (2 * HALO if halo else 0), sw), F32)
                           for _ in range(len(ws) * (tn // sw))]),
        compiler_params=_params(("parallel", "parallel", "arbitrary")),
        name="norm_proj_" + epi,
    )(*args)


def _row_gate(g_ref, t, tm, ctx_len):
    g = t * tm + lax.broadcasted_iota(jnp.int32, (tm, 1), 0)
    return jnp.where(g < ctx_len, g_ref[0, 1:2], g_ref[0, 0:1])


def _oproj_kernel(y_ref, w_ref, x_ref, g_ref, o_ref, *, tm, ctx_len):
    acc = jnp.dot(y_ref[0], w_ref[...], preferred_element_type=F32)
    o_ref[0] = x_ref[0] + _row_gate(g_ref, pl.program_id(1), tm, ctx_len) * acc


def _oproj_call(y, w, x, gate, *, tm, tn, ctx_len=CTX_LEN):
    b, tok, d = x.shape
    k = y.shape[-1]
    return pl.pallas_call(
        functools.partial(_oproj_kernel, tm=tm, ctx_len=ctx_len),
        out_shape=jax.ShapeDtypeStruct(x.shape, F32),
        grid=(b, tok // tm, d // tn),
        in_specs=[
            pl.BlockSpec((1, tm, k), lambda bi, t, j: (bi, t, 0)),
            pl.BlockSpec((k, tn), lambda bi, t, j: (0, j)),
            pl.BlockSpec((1, tm, tn), lambda bi, t, j: (bi, t, j)),
            pl.BlockSpec((1, 2, tn), lambda bi, t, j: (bi, 0, j)),
        ],
        out_specs=pl.BlockSpec((1, tm, tn), lambda bi, t, j: (bi, t, j)),
        compiler_params=_params(("parallel", "parallel", "arbitrary")),
        name="out_proj_residual",
    )(y, w, x, gate)


def _gdn_gate_kernel(of_ref, ob_ref, z_ref, gain_ref, y_ref):
    gain = gain_ref[...]
    for h in range(GDN_V_HEADS):
        o = of_ref[0, h].astype(F32) + ob_ref[0, h].astype(F32)
        y = o * lax.rsqrt(jnp.mean(o * o, axis=-1, keepdims=True) + EPS) * gain
        z = z_ref[0, :, h * LANES:(h + 1) * LANES].astype(F32)
        y_ref[0, :, h * LANES:(h + 1) * LANES] = (y * _silu(z)).astype(BF16)


def _gdn_gate_call(o_fwd, o_bwd, z, gain, *, tr):
    b, tok, k = z.shape
    return pl.pallas_call(
        _gdn_gate_kernel,
        out_shape=jax.ShapeDtypeStruct(z.shape, BF16),
        grid=(b, tok // tr),
        in_specs=[
            pl.BlockSpec((1, GDN_V_HEADS, tr, LANES), lambda bi, t: (bi, 0, t, 0)),
            pl.BlockSpec((1, GDN_V_HEADS, tr, LANES), lambda bi, t: (bi, 0, t, 0)),
            pl.BlockSpec((1, tr, k), lambda bi, t: (bi, t, 0)),
            pl.BlockSpec((1, LANES), lambda bi, t: (0, 0)),
        ],
        out_specs=pl.BlockSpec((1, tr, k), lambda bi, t: (bi, t, 0)),
        compiler_params=_params(("parallel", "parallel")),
        name="gdn_gated_norm",
    )(o_fwd, o_bwd, z, gain.reshape(1, LANES))


ATT_HEADS = 2


def _diff_attn_kernel(q_ref, k_ref, v_ref, lam_ref, gain_ref, o_ref, *, lam_init, ctx_len):
    lv = lam_ref[...]
    lam = (jnp.exp(jnp.sum(lv[0:1] * lv[1:2], axis=-1, keepdims=True))
           - jnp.exp(jnp.sum(lv[2:3] * lv[3:4], axis=-1, keepdims=True)) + lam_init)

    def attend(kk, vv):
        q = q_ref[0]
        scores = []
        for hc in range(2 * ATT_HEADS):
            cs = slice(hc * DA_QK_DIM, (hc + 1) * DA_QK_DIM)
            scores.append(lax.dot_general(q[:, cs], kk[:, cs], NT_DIMS, preferred_element_type=F32))
        for h in range(ATT_HEADS):
            ps, ls = [], []
            for c in range(2):
                s = scores[2 * h + c]
                p = jnp.exp2(s - jnp.max(s, axis=-1, keepdims=True))
                ps.append(p)
                ls.append(jnp.sum(p, axis=-1, keepdims=True))
            inv0 = 1.0 / ls[0]
            a = ps[0] - (lam * ls[0] * (1.0 / ls[1])) * ps[1]
            vs = slice(h * DA_V_DIM, (h + 1) * DA_V_DIM)
            o = jnp.dot(a.astype(BF16), vv[:, vs], preferred_element_type=F32) * inv0
            o = o * lax.rsqrt(jnp.mean(o * o, axis=-1, keepdims=True) + EPS) * gain_ref[...]
            o_ref[0, :, vs] = (o * (1.0 - lam_init)).astype(o_ref.dtype)

    t = pl.program_id(2)

    @pl.when(t == 0)
    def _():
        attend(k_ref[0, 0:ctx_len], v_ref[0, 0:ctx_len])

    @pl.when(t != 0)
    def _():
        attend(k_ref[0], v_ref[0])


def _diff_attn_call(qkv, lam_vecs, head_gain, lam_init, *, ctx_len=CTX_LEN):
    b, tok, _ = qkv.shape
    hw = ATT_HEADS * 2 * DA_QK_DIM
    ng = DA_HEADS // ATT_HEADS
    tq = ctx_len
    return pl.pallas_call(
        functools.partial(_diff_attn_kernel, lam_init=lam_init, ctx_len=ctx_len),
        out_shape=jax.ShapeDtypeStruct((b, tok, DA_V_W), BF16),
        grid=(b, ng, tok // tq),
        in_specs=[
            pl.BlockSpec((1, tq, hw), lambda bi, h, t: (bi, t, h)),
            pl.BlockSpec((1, tok, hw), lambda bi, h, t: (bi, 0, ng + h)),
            pl.BlockSpec((1, tok, hw), lambda bi, h, t: (bi, 0, 2 * ng + h)),
            pl.BlockSpec((4, DA_QK_DIM), lambda bi, h, t: (0, 0)),
            pl.BlockSpec((1, DA_V_DIM), lambda bi, h, t: (0, 0)),
        ],
        out_specs=pl.BlockSpec((1, tq, hw), lambda bi, h, t: (bi, t, h)),
        compiler_params=_params(("parallel", "parallel", "arbitrary")),
        name="diff_attention",
    )(qkv, qkv, qkv, lam_vecs, head_gain.reshape(1, DA_V_DIM))


def _split3(x):
    x1 = x.astype(BF16)
    r1 = x - x1.astype(F32)
    x2 = r1.astype(BF16)
    x3 = (r1 - x2.astype(F32)).astype(BF16)
    return x1, x2, x3


def _blockdiag(x2):
    first = lax.broadcasted_iota(jnp.int32, x2.shape, 1) < CHUNK
    return jnp.concatenate([jnp.where(first, x2, 0.0), jnp.where(first, 0.0, x2)], axis=0)


def _unit_lower_inverses(a2s, base):
    c = CHUNK
    nt = c // base
    assert base == 8
    lane = lax.broadcasted_iota(jnp.int32, (base, LANES), 1)
    sub = lax.broadcasted_iota(jnp.int32, (base, LANES), 0)
    row2 = lax.broadcasted_iota(jnp.int32, (c, LANES), 0)
    col2 = lax.broadcasted_iota(jnp.int32, (c, LANES), 1) % c

    in_blk = [((lane % c) // base) == r for r in range(nt)]
    dgs = []
    for a2 in a2s:
        dg = jnp.where(in_blk[0], a2[0:base], 0.0)
        for r in range(1, nt):
            dg = jnp.where(in_blk[r], a2[base * r:base * (r + 1)], dg)
        dgs.append(dg)
    ts = [jnp.where((lane % base) == sub, 1.0, 0.0).astype(F32) for _ in a2s]
    for jcol in range(base - 1):
        idx = (lane // base) * base + jcol
        for i, dg in enumerate(dgs):
            acol = jnp.take_along_axis(dg, idx, axis=1)
            ts[i] = ts[i] - acol * ts[i][jcol:jcol + 1, :]
    tbds = [_blockdiag(jnp.concatenate([jnp.where(in_blk[r], t, 0.0) for r in range(nt)], axis=0))
            for t in ts]

    size = base
    while size < c:
        off = ((row2 // (2 * size)) == (col2 // (2 * size))) & ((row2 // size) != (col2 // size))
        tbs = [tbd.astype(BF16) for tbd in tbds]
        tqs = [jnp.dot(tb, _blockdiag(jnp.where(off, a2, 0.0)).astype(BF16), preferred_element_type=F32)
               for tb, a2 in zip(tbs, a2s)]
        tbds = [tbd - jnp.dot(tq.astype(BF16), tb, preferred_element_type=F32)
                for tbd, tq, tb in zip(tbds, tqs, tbs)]
        size *= 2
    return tbds


def _gdn_scan_kernel(qf_ref, kf_ref, vf_ref, abf_ref, qb_ref, kb_ref, vb_ref, abb_ref, alog_ref, dt_ref,
                     of_ref, ob_ref, s_ref, cg_ref, cb_ref, rg_ref, rb_ref, gt_ref, *, base):
    c = CHUNK
    nh, npair = GDN_V_HEADS, GDN_QK_HEADS
    dirs = ((qf_ref, kf_ref, vf_ref, abf_ref, of_ref), (qb_ref, kb_ref, vb_ref, abb_ref, ob_ref))

    @pl.when(pl.program_id(1) == 0)
    def _():
        s_ref[...] = jnp.zeros(s_ref.shape, F32)

    ri = lax.broadcasted_iota(jnp.int32, (c, c), 0)
    ci = lax.broadcasted_iota(jnp.int32, (c, c), 1)
    pi = lax.broadcasted_iota(jnp.int32, (npair, LANES), 0)
    li = lax.broadcasted_iota(jnp.int32, (npair, LANES), 1)
    zpad = jnp.zeros((c, LANES), BF16)

    def pair_rows(x, col0):
        sel_e = jnp.where(li == col0 + 2 * pi, 1.0, 0.0).astype(BF16)
        sel_o = jnp.where(li == col0 + 2 * pi + 1, 1.0, 0.0).astype(BF16)
        acc = jnp.zeros((npair, LANES), F32)
        for part in _split3(x):
            acc = acc + lax.dot_general(sel_e, jnp.concatenate([part, zpad], axis=0), NT_DIMS,
                                        preferred_element_type=F32)
            acc = acc + lax.dot_general(sel_o, jnp.concatenate([zpad, part], axis=0), NT_DIMS,
                                        preferred_element_type=F32)
        return acc

    for d, (_, _, _, ab_ref, _) in enumerate(dirs):
        ab = ab_ref[0]
        beta_all = jax.nn.sigmoid(ab)
        xs = ab + dt_ref[d]
        g_all = -jnp.exp(alog_ref[d]) * (jnp.maximum(xs, 0.0) + jnp.log1p(jnp.exp(-jnp.abs(xs))))
        cum = jnp.where((ci >= ri) if d else (ci <= ri), 1.0, 0.0).astype(BF16)
        gam_all = sum(jnp.dot(cum, part, preferred_element_type=F32) for part in _split3(g_all))
        gtot_all = jnp.sum(g_all, axis=0, keepdims=True)
        rg_ref[d * npair:(d + 1) * npair, :] = pair_rows(gam_all, nh)
        rb_ref[d * npair:(d + 1) * npair, :] = pair_rows(beta_all, 0)
        for h in range(nh):
            cg_ref[d * nh + h] = jnp.broadcast_to(gam_all[:, nh + h:nh + h + 1], (c, LANES))
            cb_ref[d * nh + h] = jnp.broadcast_to(beta_all[:, h:h + 1], (c, LANES))
            gt_ref[d * nh + h:d * nh + h + 1, :] = jnp.broadcast_to(gtot_all[:, nh + h:nh + h + 1],
                                                                    (1, LANES))

    row2 = lax.broadcasted_iota(jnp.int32, (c, LANES), 0)
    lane2 = lax.broadcasted_iota(jnp.int32, (c, LANES), 1)
    col2 = lane2 % c
    first2 = lane2 < c

    items = [(d, p) for d in range(2) for p in range(npair)]

    qkks = {}
    for d, p in items:
        q = dirs[d][0][0, p]
        k = dirs[d][1][0, p]
        qkks[d, p] = lax.dot_general(jnp.concatenate([q, k], axis=0), jnp.concatenate([k, k], axis=0),
                                     NT_DIMS, preferred_element_type=F32)
    a2s, qkm2s = [], {}
    for d, p in items:
        h0 = d * nh + 2 * p
        gc2 = jnp.where(first2, cg_ref[h0], cg_ref[h0 + 1])
        bc2 = jnp.where(first2, cb_ref[h0], cb_ref[h0 + 1])
        gr2 = rg_ref[d * npair + p:d * npair + p + 1, :]
        br2 = rb_ref[d * npair + p:d * npair + p + 1, :]
        e2 = jnp.exp(-jnp.abs(gc2 - gr2))
        a2s.append(jnp.where(row2 > col2, (br2 if d else bc2) * qkks[d, p][c:2 * c] * e2, 0.0))
        qkm2s[d, p] = jnp.where((row2 <= col2) if d else (row2 >= col2), qkks[d, p][0:c] * e2, 0.0)

    tbds = dict(zip(items, _unit_lower_inverses(a2s, base)))

    uws = {}
    for d, p in items:
        kf = dirs[d][1][0, p].astype(F32)
        rhs = []
        for hh in (2 * p, 2 * p + 1):
            cb = cb_ref[d * nh + hh]
            vb = dirs[d][2][0, hh].astype(F32) * cb
            kb = kf * (cb * jnp.exp(cg_ref[d * nh + hh]))
            rhs.append(jnp.concatenate([vb, kb], axis=1).astype(BF16))
        tuse = (tbds[d, p].T if d else tbds[d, p]).astype(BF16)
        uws[d, p] = jnp.dot(tuse, jnp.concatenate(rhs, axis=0), preferred_element_type=F32)

    heads = [(d, hh) for d in range(2) for hh in range(nh)]
    ws_qs = {}
    for d, hh in heads:
        w = uws[d, hh // 2][(hh % 2) * c:(hh % 2 + 1) * c, GDN_DIM:2 * GDN_DIM]
        qd = dirs[d][0][0, hh // 2].astype(F32) * jnp.exp(cg_ref[d * nh + hh])
        wq = jnp.concatenate([w.astype(BF16), qd.astype(BF16)], axis=0)
        ws_qs[d, hh] = jnp.dot(wq, s_ref[d * nh + hh].astype(BF16), preferred_element_type=F32)

    for d, hh in heads:
        h = d * nh + hh
        u = uws[d, hh // 2][(hh % 2) * c:(hh % 2 + 1) * c, 0:GDN_DIM]
        v_new = (u - ws_qs[d, hh][0:c]).astype(BF16)
        qkm = qkm2s[d, hh // 2][:, (hh % 2) * c:(hh % 2 + 1) * c].astype(BF16)
        o = ws_qs[d, hh][c:2 * c] + jnp.dot(qkm, v_new, preferred_element_type=F32)
        dirs[d][4][0, hh] = o.astype(of_ref.dtype)
        gt = gt_ref[h:h + 1, :]
        k_dec = (dirs[d][1][0, hh // 2].astype(F32) * jnp.exp(gt - cg_ref[h])).astype(BF16)
        s_ref[h] = s_ref[h] * jnp.exp(gt) + lax.dot_general(k_dec, v_new, TN_DIMS,
                                                           preferred_element_type=F32)


def _gdn_scan_call(qk, v, ab, alog, dt, *, ctx_len=CTX_LEN, base=8):
    b, _, tok, _ = qk.shape
    nc = tok // CHUNK
    ncc = ctx_len // CHUNK

    def back(s):
        return jnp.where(s < ncc, ncc - 1 - s, nc - 1 - (s - ncc))

    qk_blk = (1, GDN_QK_HEADS, CHUNK, LANES)
    v_blk = (1, GDN_V_HEADS, CHUNK, LANES)
    o_shape = jax.ShapeDtypeStruct((b, GDN_V_HEADS, tok, LANES), BF16)
    return pl.pallas_call(
        functools.partial(_gdn_scan_kernel, base=base),
        out_shape=(o_shape, o_shape),
        grid=(b, nc),
        in_specs=[
            pl.BlockSpec(qk_blk, lambda bi, s: (bi, 0, s, 0)),
            pl.BlockSpec(qk_blk, lambda bi, s: (bi, 1, s, 0)),
            pl.BlockSpec(v_blk, lambda bi, s: (bi, 0, s, 0)),
            pl.BlockSpec((1, CHUNK, LANES), lambda bi, s: (bi, s, 0)),
            pl.BlockSpec(qk_blk, lambda bi, s: (bi, 0, back(s), 0)),
            pl.BlockSpec(qk_blk, lambda bi, s: (bi, 1, back(s), 0)),
            pl.BlockSpec(v_blk, lambda bi, s: (bi, 0, back(s), 0)),
            pl.BlockSpec((1, CHUNK, LANES), lambda bi, s: (bi, back(s), 1)),
            pl.BlockSpec((2, 1, LANES), lambda bi, s: (0, 0, 0)),
            pl.BlockSpec((2, 1, LANES), lambda bi, s: (0, 0, 0)),
        ],
        out_specs=(pl.BlockSpec(v_blk, lambda bi, s: (bi, 0, s, 0)),
                   pl.BlockSpec(v_blk, lambda bi, s: (bi, 0, back(s), 0))),
        scratch_shapes=[
            pltpu.VMEM((2 * GDN_V_HEADS, GDN_DIM, GDN_DIM), F32),
            pltpu.VMEM((2 * GDN_V_HEADS, CHUNK, LANES), F32),
            pltpu.VMEM((2 * GDN_V_HEADS, CHUNK, LANES), F32),
            pltpu.VMEM((2 * GDN_QK_HEADS, LANES), F32),
            pltpu.VMEM((2 * GDN_QK_HEADS, LANES), F32),
            pltpu.VMEM((2 * GDN_V_HEADS, LANES), F32),
        ],
        compiler_params=_params(("parallel", "arbitrary")),
        name="gdn_chunk_scan",
    )(qk, qk, v, ab, qk, qk, v, ab, alog, dt)


def _final_norm_kernel(x_ref, w_ref, o_ref):
    xv = x_ref[0]
    o_ref[0] = xv * lax.rsqrt(jnp.mean(xv * xv, axis=-1, keepdims=True) + EPS) * w_ref[...]


def _final_norm_call(x, w, *, ctx_len=CTX_LEN):
    b, tok, d = x.shape
    tr = ctx_len
    return pl.pallas_call(
        _final_norm_kernel,
        out_shape=jax.ShapeDtypeStruct((b, tok - ctx_len, d), F32),
        grid=(b, (tok - ctx_len) // tr),
        in_specs=[pl.BlockSpec((1, tr, d), lambda bi, t: (bi, t + 1, 0)),
                  pl.BlockSpec((1, d), lambda bi, t: (0, 0))],
        out_specs=pl.BlockSpec((1, tr, d), lambda bi, t: (bi, t, 0)),
        compiler_params=_params(("parallel", "parallel")),
        name="final_rmsnorm",
    )(x, w.reshape(1, d))


def _rope_tables(seq, ctx_len, width):
    t = jnp.arange(seq, dtype=jnp.int32)
    rows = (t // GRID_W).astype(F32)
    cols = (t % GRID_W).astype(F32)
    n_freq = DA_QK_DIM // 4
    inv_freq = ROPE_BASE ** (-jnp.arange(n_freq, dtype=F32) / n_freq)
    ar = rows[:, None] * inv_freq
    ac = cols[:, None] * inv_freq
    cos = jnp.concatenate([jnp.cos(ar), jnp.cos(ar), jnp.cos(ac), jnp.cos(ac)], axis=1)
    sin = jnp.concatenate([-jnp.sin(ar), jnp.sin(ar), -jnp.sin(ac), jnp.sin(ac)], axis=1)
    cos = jnp.concatenate([jnp.ones((ctx_len, DA_QK_DIM), F32), cos], axis=0)
    sin = jnp.concatenate([jnp.zeros((ctx_len, DA_QK_DIM), F32), sin], axis=0)
    reps = width // DA_QK_DIM
    return jnp.tile(cos, (1, reps)), jnp.tile(sin, (1, reps))


def kernel(x, c, ctx, c_ctx, w_mod, b_mod, norm_mix, norm_ffn, da_w_qkv, da_lambda, da_head_gain, da_w_o,
           gdn_w_in, gdn_conv, gdn_a_log, gdn_dt_bias, gdn_norm_gain, gdn_w_o, ffn_w_up, ffn_conv,
           ffn_w_down, final_norm):
    bsz, seq, d = x.shape
    ctx_len = ctx.shape[1]
    tm = (ctx_len + seq) // 3
    tm_out = (ctx_len + seq) // 2
    tn_wide = 1024
    tn_ffn = 512
    sw_conv = 512

    xa = jnp.concatenate([ctx, x], axis=1)
    pad_rows = 16 - (bsz + 1)
    cvec = jnp.concatenate([c, c_ctx[None, :], jnp.zeros((pad_rows, d), F32)], axis=0)
    mods_all = _mods_call(cvec, w_mod, b_mod)
    cos_t, sin_t = _rope_tables(seq, ctx_len, MXU_N)

    def sel(mods, idx):
        lat = mods[:bsz, idx * d:(idx + 1) * d]
        cx = jnp.broadcast_to(mods[bsz:bsz + 1, idx * d:(idx + 1) * d], (bsz, d))
        return lat, cx

    def modv(mods, i_shift, i_scale):
        sl, sc = sel(mods, i_shift)
        cl, cc = sel(mods, i_scale)
        return jnp.stack([sl, cl, sc, cc], axis=1)

    def gatev(mods, idx):
        gl, gc = sel(mods, idx)
        return jnp.stack([gl, gc], axis=1)

    for i in range(DEPTH):
        mods = mods_all[i]
        jm = i // 2
        mv = modv(mods, 0, 1)
        if i % 2 == 0:
            lam_init = 0.8 - 0.6 * math.exp(-0.3 * i)
            qkv = _proj_call(xa, mv, norm_mix[i], [da_w_qkv[jm].astype(BF16)], [], epi="rope", tn=tn_wide,
                             tm=tm_out, extras=(cos_t, sin_t), q_tiles=DA_QK_W // tn_wide, ctx_len=ctx_len)
            att = _diff_attn_call(qkv, da_lambda[jm], da_head_gain[jm], lam_init, ctx_len=ctx_len)
            xa = _oproj_call(att, da_w_o[jm].astype(BF16), xa, gatev(mods, 2), tm=tm_out, tn=512,
                             ctx_len=ctx_len)
        else:
            w_in = gdn_w_in[jm].astype(BF16)
            cw = gdn_conv[jm]
            qk = _proj_call(xa, mv, norm_mix[i], [w_in], [cw], epi="gdn_qk", tn=tn_wide, tm=tm, sw=sw_conv,
                            n=2 * GDN_QK_W, col0=0, q_tiles=GDN_QK_W // tn_wide, ctx_len=ctx_len)
            vv = _proj_call(xa, mv, norm_mix[i], [w_in], [cw], epi="gdn_v", tn=tn_wide, tm=tm, sw=sw_conv,
                            n=GDN_V_W, col0=2 * GDN_QK_W, ctx_len=ctx_len)
            z = _proj_call(xa, mv, norm_mix[i], [w_in], [], epi="plain", tn=tn_wide, tm=tm_out,
                           n=GDN_V_W, col0=GDN_QKV_W, ctx_len=ctx_len)
            w_ab = w_in[:, GDN_QKV_W + GDN_V_W:].reshape(d, 2, 2 * GDN_V_HEADS)
            w_ab = jnp.concatenate([w_ab, jnp.zeros_like(w_ab)], axis=2).reshape(d, 2 * LANES)
            ab = _proj_call(xa, mv, norm_mix[i], [w_ab], [], epi="plain", tn=2 * LANES, tm=tm,
                            out_dtype=F32, ctx_len=ctx_len)
            zeros = jnp.zeros((2, GDN_V_HEADS), F32)
            alog = jnp.concatenate([zeros, gdn_a_log[jm], zeros, zeros], axis=1).reshape(2, 1, LANES)
            dtb = jnp.concatenate([zeros, gdn_dt_bias[jm], zeros, zeros], axis=1).reshape(2, 1, LANES)
            o_fwd, o_bwd = _gdn_scan_call(qk, vv, ab, alog, dtb, ctx_len=ctx_len)
            y = _gdn_gate_call(o_fwd, o_bwd, z, gdn_norm_gain[jm], tr=ctx_len)
            xa = _oproj_call(y, gdn_w_o[jm].astype(BF16), xa, gatev(mods, 2), tm=tm_out, tn=512,
                             ctx_len=ctx_len)
        padc = D_FF_PAD - D_FF
        w_up = ffn_w_up[i].astype(BF16)
        w_g = jnp.pad(w_up[:, :D_FF], ((0, 0), (0, padc)))
        w_v = jnp.pad(w_up[:, D_FF:], ((0, 0), (0, padc)))
        cwf = ffn_conv[i]
        cw_g = jnp.pad(cwf[:, :D_FF], ((0, 0), (0, padc)))
        cw_v = jnp.pad(cwf[:, D_FF:], ((0, 0), (0, padc)))
        act = _proj_call(xa, modv(mods, 3, 4), norm_ffn[i], [w_g, w_v], [cw_g, cw_v], epi="ffn", tn=tn_ffn,
                         tm=tm, sw=sw_conv, ctx_len=ctx_len)
        w_dn = jnp.pad(ffn_w_down[i].astype(BF16), ((0, padc), (0, 0)))
        xa = _oproj_call(act, w_dn, xa, gatev(mods, 5), tm=tm_out, tn=512, ctx_len=ctx_len)
    return _final_norm_call(xa, final_norm, ctx_len=ctx_len)
```

```python
import functools
import math

import jax
import jax.numpy as jnp
from jax import lax
from jax.experimental import pallas as pl
from jax.experimental.pallas import tpu as pltpu

F32 = jnp.float32
BF16 = jnp.bfloat16

D_MODEL = 2048
DEPTH = 4
CTX_LEN = 256
GRID_W = 64
EPS = 1e-6
DA_HEADS = 8
DA_QK_DIM = 128
DA_V_DIM = 256
DA_QK_W = DA_HEADS * 2 * DA_QK_DIM
DA_V_W = DA_HEADS * DA_V_DIM
ROPE_BASE = 10000.0
GDN_QK_HEADS = 16
GDN_V_HEADS = 32
GDN_DIM = 128
GDN_QK_W = GDN_QK_HEADS * GDN_DIM
GDN_V_W = GDN_V_HEADS * GDN_DIM
GDN_QKV_W = 2 * GDN_QK_W + GDN_V_W
GDN_CONV = 5
CHUNK = 64
D_FF = 5504
FFN_CONV = 3

LANES = 128
HALO = 16
D_FF_PAD = 5632
VMEM_LIMIT = 56 * 1024 * 1024
LOG2E = math.log2(math.e)

NT_DIMS = (((1,), (1,)), ((), ()))
TN_DIMS = (((0,), (0,)), ((), ()))


def _silu(x):
    return x * jax.nn.sigmoid(x)


def _params(sem):
    return pltpu.CompilerParams(dimension_semantics=sem, vmem_limit_bytes=VMEM_LIMIT)


def _mods_kernel(c_ref, w_ref, b_ref, o_ref):
    s = _silu(c_ref[...]).astype(BF16)
    w = w_ref[0].astype(BF16)
    o_ref[0] = jnp.dot(s, w, preferred_element_type=F32) + b_ref[0]


def _mods_call(cvec, w_mod, b_mod):
    depth, d, n = w_mod.shape
    tn = 1024
    return pl.pallas_call(
        _mods_kernel,
        out_shape=jax.ShapeDtypeStruct((depth, cvec.shape[0], n), F32),
        grid=(depth, n // tn),
        in_specs=[
            pl.BlockSpec((cvec.shape[0], d), lambda i, j: (0, 0)),
            pl.BlockSpec((1, d, tn), lambda i, j: (i, 0, j)),
            pl.BlockSpec((1, 1, tn), lambda i, j: (i, 0, j)),
        ],
        out_specs=pl.BlockSpec((1, cvec.shape[0], tn), lambda i, j: (i, 0, j)),
        compiler_params=_params(("parallel", "parallel")),
        name="adaln_mods",
    )(cvec, w_mod, b_mod.reshape(depth, 1, n))


MXU_N = 256
EPI_ROWS = 128
NORM_ROWS = 16


def _proj_kernel(*refs, n_w, taps, epi, tm, tn, sw, ctx_len, tok, q_tiles):
    halo = taps > 0
    ns = tn // sw
    it = iter(refs)
    x_ref = next(it)
    xp_ref = next(it) if halo else None
    xn_ref = next(it) if halo else None
    modv_ref = next(it)
    nw_ref = next(it)
    w_refs = [next(it) for _ in range(n_w)]
    cw_refs = [next(it) for _ in range(n_w)] if halo else []
    ex_refs = [next(it) for _ in range(2)] if epi == "rope" else []
    out_ref = next(it)
    h_ref = next(it)
    u_refs = [[next(it) for _ in range(ns)] for _ in range(n_w)]

    t = pl.program_id(1)
    j = pl.program_id(2)
    off = HALO if halo else 0
    p = taps // 2

    @pl.when(j == 0)
    def _():
        nw = nw_ref[...]
        gain_l = nw * (1.0 + modv_ref[0, 1:2])
        gain_c = nw * (1.0 + modv_ref[0, 3:4])
        shift_l = modv_ref[0, 0:1]
        shift_c = modv_ref[0, 2:3]

        def norm_rows(src_ref, n_rows, dst0, g0, keep):
            def body(i, carry):
                r = pl.multiple_of(i * NORM_ROWS, NORM_ROWS)
                xv = src_ref[0, pl.ds(r, NORM_ROWS), :]
                inv = lax.rsqrt(jnp.mean(xv * xv, axis=-1, keepdims=True) + EPS)
                is_ctx = (g0 + r + lax.broadcasted_iota(jnp.int32, (NORM_ROWS, 1), 0)) < ctx_len
                hv = (xv * inv) * jnp.where(is_ctx, gain_c, gain_l) + jnp.where(is_ctx, shift_c, shift_l)
                if keep is not None:
                    hv = jnp.where(keep, hv, 0.0)
                h_ref[pl.ds(dst0 + r, NORM_ROWS), :] = hv.astype(BF16)
                return carry
            lax.fori_loop(0, n_rows // NORM_ROWS, body, 0, unroll=8 if n_rows > NORM_ROWS else 1)

        norm_rows(x_ref, tm, off, t * tm, None)
        if halo:
            keep_prev = (t * tm != 0) & (t * tm != ctx_len)
            keep_next = (t * tm + tm != tok) & (t * tm + tm != ctx_len)
            norm_rows(xp_ref, HALO, 0, t * tm - HALO, keep_prev)
            norm_rows(xn_ref, HALO, off + tm, t * tm + tm, keep_next)

    for s in range(ns):
        for w, u_s in zip(w_refs, u_refs):
            u_s[s][...] = jnp.dot(h_ref[...], w[:, s * sw:(s + 1) * sw], preferred_element_type=F32)

    def conv_rows(u_ref, cw_ref, s, r0, nrows, edge):
        if not halo:
            return u_ref[r0:r0 + nrows, :]
        acc = None
        for jj in range(taps):
            dlt = jj - p
            sh = u_ref[off + r0 + dlt:off + r0 + dlt + nrows, :]
            if edge is not None and dlt != 0:
                ri = r0 + lax.broadcasted_iota(jnp.int32, (nrows, 1), 0)
                same = ((ri >= edge) & (ri + dlt >= edge)) | ((ri < edge) & (ri + dlt < edge))
                sh = jnp.where(same, sh, 0.0)
            term = sh * cw_ref[jj:jj + 1, s * sw:(s + 1) * sw]
            acc = term if acc is None else acc + term
        return acc

    def finish(s, r0, nrows, edge=None):
        vals = [conv_rows(u_s[s], cw, s, r0, nrows, edge)
                for u_s, cw in zip(u_refs, cw_refs if halo else [None] * n_w)]
        rs = slice(r0, r0 + nrows)
        cs = slice(s * sw, (s + 1) * sw)
        hpt = sw // LANES
        if epi == "ffn":
            out_ref[0, rs, cs] = (_silu(vals[0]) * vals[1]).astype(out_ref.dtype)
        elif epi == "plain":
            out_ref[0, rs, cs] = vals[0].astype(out_ref.dtype)
        elif epi == "gdn_v":
            a = _silu(vals[0])
            for hh in range(hpt):
                out_ref[0, s * hpt + hh, rs] = a[:, hh * LANES:(hh + 1) * LANES].astype(out_ref.dtype)
        elif epi == "gdn_qk":
            a = _silu(vals[0])
            qs = jnp.where(j < q_tiles, GDN_DIM ** -0.5, 1.0).astype(F32)
            for hh in range(hpt):
                xs = a[:, hh * LANES:(hh + 1) * LANES]
                nrm = xs * lax.rsqrt(jnp.sum(xs * xs, axis=-1, keepdims=True) + EPS)
                out_ref[0, s * hpt + hh, rs] = (nrm * qs).astype(out_ref.dtype)
        elif epi == "rope":
            u = vals[0]
            cos = jnp.where(j < 2 * q_tiles, ex_refs[0][rs, :], 1.0)
            sin = jnp.where(j < 2 * q_tiles, ex_refs[1][rs, :], 0.0)
            lane = lax.broadcasted_iota(jnp.int32, (1, sw), 1)
            swapped = jnp.where((lane % 64) < 32, pltpu.roll(u, sw - 32, 1), pltpu.roll(u, 32, 1))
            sc = jnp.where(j < q_tiles, DA_QK_DIM ** -0.5 * LOG2E, 1.0).astype(F32)
            out_ref[0, rs, cs] = ((u * cos + swapped * sin) * sc).astype(out_ref.dtype)
        else:
            raise ValueError(epi)

    for s in range(ns):
        for r0 in range(0, tm, EPI_ROWS):
            finish(s, r0, EPI_ROWS)

    if halo and ctx_len % tm != 0:
        edge = ctx_len % tm

        @pl.when(t == ctx_len // tm)
        def _():
            for s in range(ns):
                finish(s, edge - HALO, 2 * HALO, edge)


def _proj_call(x, modv, nw, ws, cws, *, epi, tn, tm, sw=MXU_N, n=None, col0=0, out_dtype=BF16, extras=(),
               q_tiles=0, ctx_len=CTX_LEN):
    b, tok, d = x.shape
    n = ws[0].shape[1] if n is None else n
    j0 = col0 // tn
    taps = cws[0].shape[0] if cws else 0
    halo = taps > 0
    nt = tok // tm
    hb = tm // HALO
    nhb = tok // HALO

    in_specs = [pl.BlockSpec((1, tm, d), lambda bi, t, j: (bi, t, 0))]
    args = [x]
    if halo:
        in_specs += [
            pl.BlockSpec((1, HALO, d), lambda bi, t, j: (bi, jnp.maximum(t * hb - 1, 0), 0)),
            pl.BlockSpec((1, HALO, d), lambda bi, t, j: (bi, jnp.minimum((t + 1) * hb, nhb - 1), 0)),
        ]
        args += [x, x]
    in_specs += [pl.BlockSpec((1, 4, d), lambda bi, t, j: (bi, 0, 0)),
                 pl.BlockSpec((1, d), lambda bi, t, j: (0, 0))]
    args += [modv, nw.reshape(1, d)]
    for w in ws:
        in_specs.append(pl.BlockSpec((d, tn), lambda bi, t, j: (0, j0 + j)))
        args.append(w)
    for cw in cws:
        in_specs.append(pl.BlockSpec((taps, tn), lambda bi, t, j: (0, j0 + j)))
        args.append(cw)
    for e in extras:
        in_specs.append(pl.BlockSpec((tm, sw), lambda bi, t, j: (t, 0)))
        args.append(e)

    if epi in ("gdn_v", "gdn_qk"):
        out_shape = jax.ShapeDtypeStruct((b, n // LANES, tok, LANES), out_dtype)
        out_spec = pl.BlockSpec((1, tn // LANES, tm, LANES), lambda bi, t, j: (bi, j, t, 0))
    else:
        out_shape = jax.ShapeDtypeStruct((b, tok, n), out_dtype)
        out_spec = pl.BlockSpec((1, tm, tn), lambda bi, t, j: (bi, t, j))

    kern = functools.partial(_proj_kernel, n_w=len(ws), taps=taps, epi=epi, tm=tm, tn=tn, sw=sw,
                             ctx_len=ctx_len, tok=tok, q_tiles=q_tiles)
    return pl.pallas_call(
        kern,
        out_shape=out_shape,
        grid=(b, nt, n // tn),
        in_specs=in_specs,
        out_specs=out_spec,
        scratch_shapes=([pltpu.VMEM((tm + (2 * HALO if halo else 0), d), BF16)]
                        + [pltpu.VMEM((tm + (2 * HALO if halo else 0), sw), F32)
                           for _ in range(len(ws) * (tn // sw))]),
        compiler_params=_params(("parallel", "parallel", "arbitrary")),
        name="norm_proj_" + epi,
    )(*args)


def _row_gate(g_ref, t, tm, ctx_len):
    g = t * tm + lax.broadcasted_iota(jnp.int32, (tm, 1), 0)
    return jnp.where(g < ctx_len, g_ref[0, 1:2], g_ref[0, 0:1])


def _oproj_kernel(y_ref, w_ref, x_ref, g_ref, o_ref, *, tm, ctx_len):
    acc = jnp.dot(y_ref[0], w_ref[...], preferred_element_type=F32)
    o_ref[0] = x_ref[0] + _row_gate(g_ref, pl.program_id(1), tm, ctx_len) * acc


def _oproj_call(y, w, x, gate, *, tm, tn, ctx_len=CTX_LEN):
    b, tok, d = x.shape
    k = y.shape[-1]
    return pl.pallas_call(
        functools.partial(_oproj_kernel, tm=tm, ctx_len=ctx_len),
        out_shape=jax.ShapeDtypeStruct(x.shape, F32),
        grid=(b, tok // tm, d // tn),
        in_specs=[
            pl.BlockSpec((1, tm, k), lambda bi, t, j: (bi, t, 0)),
            pl.BlockSpec((k, tn), lambda bi, t, j: (0, j)),
            pl.BlockSpec((1, tm, tn), lambda bi, t, j: (bi, t, j)),
            pl.BlockSpec((1, 2, tn), lambda bi, t, j: (bi, 0, j)),
        ],
        out_specs=pl.BlockSpec((1, tm, tn), lambda bi, t, j: (bi, t, j)),
        compiler_params=_params(("parallel", "parallel", "arbitrary")),
        name="out_proj_residual",
    )(y, w, x, gate)


def _gdn_gate_kernel(of_ref, ob_ref, z_ref, gain_ref, y_ref):
    gain = gain_ref[...]
    for h in range(GDN_V_HEADS):
        o = of_ref[0, h].astype(F32) + ob_ref[0, h].astype(F32)
        y = o * lax.rsqrt(jnp.mean(o * o, axis=-1, keepdims=True) + EPS) * gain
        z = z_ref[0, :, h * LANES:(h + 1) * LANES].astype(F32)
        y_ref[0, :, h * LANES:(h + 1) * LANES] = (y * _silu(z)).astype(BF16)


def _gdn_gate_call(o_fwd, o_bwd, z, gain, *, tr):
    b, tok, k = z.shape
    return pl.pallas_call(
        _gdn_gate_kernel,
        out_shape=jax.ShapeDtypeStruct(z.shape, BF16),
        grid=(b, tok // tr),
        in_specs=[
            pl.BlockSpec((1, GDN_V_HEADS, tr, LANES), lambda bi, t: (bi, 0, t, 0)),
            pl.BlockSpec((1, GDN_V_HEADS, tr, LANES), lambda bi, t: (bi, 0, t, 0)),
            pl.BlockSpec((1, tr, k), lambda bi, t: (bi, t, 0)),
            pl.BlockSpec((1, LANES), lambda bi, t: (0, 0)),
        ],
        out_specs=pl.BlockSpec((1, tr, k), lambda bi, t: (bi, t, 0)),
        compiler_params=_params(("parallel", "parallel")),
        name="gdn_gated_norm",
    )(o_fwd, o_bwd, z, gain.reshape(1, LANES))


ATT_HEADS = 2


def _diff_attn_kernel(q_ref, k_ref, v_ref, lam_ref, gain_ref, o_ref, *, lam_init, ctx_len):
    lv = lam_ref[...]
    lam = (jnp.exp(jnp.sum(lv[0:1] * lv[1:2], axis=-1, keepdims=True))
           - jnp.exp(jnp.sum(lv[2:3] * lv[3:4], axis=-1, keepdims=True)) + lam_init)

    def attend(kk, vv):
        q = q_ref[0]
        scores = []
        for hc in range(2 * ATT_HEADS):
            cs = slice(hc * DA_QK_DIM, (hc + 1) * DA_QK_DIM)
            scores.append(lax.dot_general(q[:, cs], kk[:, cs], NT_DIMS, preferred_element_type=F32))
        for h in range(ATT_HEADS):
            ps, ls = [], []
            for c in range(2):
                s = scores[2 * h + c]
                p = jnp.exp2(s - jnp.max(s, axis=-1, keepdims=True))
                ps.append(p)
                ls.append(jnp.sum(p, axis=-1, keepdims=True))
            inv0 = 1.0 / ls[0]
            a = ps[0] - (lam * ls[0] * (1.0 / ls[1])) * ps[1]
            vs = slice(h * DA_V_DIM, (h + 1) * DA_V_DIM)
            o = jnp.dot(a.astype(BF16), vv[:, vs], preferred_element_type=F32) * inv0
            o = o * lax.rsqrt(jnp.mean(o * o, axis=-1, keepdims=True) + EPS) * gain_ref[...]
            o_ref[0, :, vs] = (o * (1.0 - lam_init)).astype(o_ref.dtype)

    t = pl.program_id(2)

    @pl.when(t == 0)
    def _():
        attend(k_ref[0, 0:ctx_len], v_ref[0, 0:ctx_len])

    @pl.when(t != 0)
    def _():
        attend(k_ref[0], v_ref[0])


def _diff_attn_call(qkv, lam_vecs, head_gain, lam_init, *, ctx_len=CTX_LEN):
    b, tok, _ = qkv.shape
    hw = ATT_HEADS * 2 * DA_QK_DIM
    ng = DA_HEADS // ATT_HEADS
    tq = ctx_len
    return pl.pallas_call(
        functools.partial(_diff_attn_kernel, lam_init=lam_init, ctx_len=ctx_len),
        out_shape=jax.ShapeDtypeStruct((b, tok, DA_V_W), BF16),
        grid=(b, ng, tok // tq),
        in_specs=[
            pl.BlockSpec((1, tq, hw), lambda bi, h, t: (bi, t, h)),
            pl.BlockSpec((1, tok, hw), lambda bi, h, t: (bi, 0, ng + h)),
            pl.BlockSpec((1, tok, hw), lambda bi, h, t: (bi, 0, 2 * ng + h)),
            pl.BlockSpec((4, DA_QK_DIM), lambda bi, h, t: (0, 0)),
            pl.BlockSpec((1, DA_V_DIM), lambda bi, h, t: (0, 0)),
        ],
        out_specs=pl.BlockSpec((1, tq, hw), lambda bi, h, t: (bi, t, h)),
        compiler_params=_params(("parallel", "parallel", "arbitrary")),
        name="diff_attention",
    )(qkv, qkv, qkv, lam_vecs, head_gain.reshape(1, DA_V_DIM))


def _split3(x):
    x1 = x.astype(BF16)
    r1 = x - x1.astype(F32)
    x2 = r1.astype(BF16)
    x3 = (r1 - x2.astype(F32)).astype(BF16)
    return x1, x2, x3


def _blockdiag(x2):
    first = lax.broadcasted_iota(jnp.int32, x2.shape, 1) < CHUNK
    return jnp.concatenate([jnp.where(first, x2, 0.0), jnp.where(first, 0.0, x2)], axis=0)


def _unit_lower_inverses(a2s, base):
    c = CHUNK
    nt = c // base
    assert base == 8
    lane = lax.broadcasted_iota(jnp.int32, (base, LANES), 1)
    sub = lax.broadcasted_iota(jnp.int32, (base, LANES), 0)
    row2 = lax.broadcasted_iota(jnp.int32, (c, LANES), 0)
    col2 = lax.broadcasted_iota(jnp.int32, (c, LANES), 1) % c

    in_blk = [((lane % c) // base) == r for r in range(nt)]
    dgs = []
    for a2 in a2s:
        dg = jnp.where(in_blk[0], a2[0:base], 0.0)
        for r in range(1, nt):
            dg = jnp.where(in_blk[r], a2[base * r:base * (r + 1)], dg)
        dgs.append(dg)
    ts = [jnp.where((lane % base) == sub, 1.0, 0.0).astype(F32) for _ in a2s]
    for jcol in range(base - 1):
        idx = (lane // base) * base + jcol
        for i, dg in enumerate(dgs):
            acol = jnp.take_along_axis(dg, idx, axis=1)
            ts[i] = ts[i] - acol * ts[i][jcol:jcol + 1, :]
    tbds = [_blockdiag(jnp.concatenate([jnp.where(in_blk[r], t, 0.0) for r in range(nt)], axis=0))
            for t in ts]

    size = base
    while size < c:
        off = ((row2 // (2 * size)) == (col2 // (2 * size))) & ((row2 // size) != (col2 // size))
        tbs = [tbd.astype(BF16) for tbd in tbds]
        tqs = [jnp.dot(tb, _blockdiag(jnp.where(off, a2, 0.0)).astype(BF16), preferred_element_type=F32)
               for tb, a2 in zip(tbs, a2s)]
        tbds = [tbd - jnp.dot(tq.astype(BF16), tb, preferred_element_type=F32)
                for tbd, tq, tb in zip(tbds, tqs, tbs)]
        size *= 2
    return tbds


def _gdn_scan_kernel(qf_ref, kf_ref, vf_ref, abf_ref, qb_ref, kb_ref, vb_ref, abb_ref, alog_ref, dt_ref,
                     of_ref, ob_ref, s_ref, cg_ref, cb_ref, rg_ref, rb_ref, gt_ref, *, base):
    c = CHUNK
    nh, npair = GDN_V_HEADS, GDN_QK_HEADS
    dirs = ((qf_ref, kf_ref, vf_ref, abf_ref, of_ref), (qb_ref, kb_ref, vb_ref, abb_ref, ob_ref))

    @pl.when(pl.program_id(1) == 0)
    def _():
        s_ref[...] = jnp.zeros(s_ref.shape, F32)

    ri = lax.broadcasted_iota(jnp.int32, (c, c), 0)
    ci = lax.broadcasted_iota(jnp.int32, (c, c), 1)
    pi = lax.broadcasted_iota(jnp.int32, (npair, LANES), 0)
    li = lax.broadcasted_iota(jnp.int32, (npair, LANES), 1)
    zpad = jnp.zeros((c, LANES), BF16)

    def pair_rows(x, col0):
        sel_e = jnp.where(li == col0 + 2 * pi, 1.0, 0.0).astype(BF16)
        sel_o = jnp.where(li == col0 + 2 * pi + 1, 1.0, 0.0).astype(BF16)
        acc = jnp.zeros((npair, LANES), F32)
        for part in _split3(x):
            acc = acc + lax.dot_general(sel_e, jnp.concatenate([part, zpad], axis=0), NT_DIMS,
                                        preferred_element_type=F32)
            acc = acc + lax.dot_general(sel_o, jnp.concatenate([zpad, part], axis=0), NT_DIMS,
                                        preferred_element_type=F32)
        return acc

    for d, (_, _, _, ab_ref, _) in enumerate(dirs):
        ab = ab_ref[0]
        beta_all = jax.nn.sigmoid(ab)
        xs = ab + dt_ref[d]
        g_all = -jnp.exp(alog_ref[d]) * (jnp.maximum(xs, 0.0) + jnp.log1p(jnp.exp(-jnp.abs(xs))))
        cum = jnp.where((ci >= ri) if d else (ci <= ri), 1.0, 0.0).astype(BF16)
        gam_all = sum(jnp.dot(cum, part, preferred_element_type=F32) for part in _split3(g_all))
        gtot_all = jnp.sum(g_all, axis=0, keepdims=True)
        rg_ref[d * npair:(d + 1) * npair, :] = pair_rows(gam_all, nh)
        rb_ref[d * npair:(d + 1) * npair, :] = pair_rows(beta_all, 0)
        for h in range(nh):
            cg_ref[d * nh + h] = jnp.broadcast_to(gam_all[:, nh + h:nh + h + 1], (c, LANES))
            cb_ref[d * nh + h] = jnp.broadcast_to(beta_all[:, h:h + 1], (c, LANES))
            gt_ref[d * nh + h:d * nh + h + 1, :] = jnp.broadcast_to(gtot_all[:, nh + h:nh + h + 1],
                                                                    (1, LANES))

    row2 = lax.broadcasted_iota(jnp.int32, (c, LANES), 0)
    lane2 = lax.broadcasted_iota(jnp.int32, (c, LANES), 1)
    col2 = lane2 % c
    first2 = lane2 < c

    items = [(d, p) for d in range(2) for p in range(npair)]

    qkks = {}
    for d, p in items:
        q = dirs[d][0][0, p]
        k = dirs[d][1][0, p]
        qkks[d, p] = lax.dot_general(jnp.concatenate([q, k], axis=0), jnp.concatenate([k, k], axis=0),
                                     NT_DIMS, preferred_element_type=F32)
    a2s, qkm2s = [], {}
    for d, p in items:
        h0 = d * nh + 2 * p
        gc2 = jnp.where(first2, cg_ref[h0], cg_ref[h0 + 1])
        bc2 = jnp.where(first2, cb_ref[h0], cb_ref[h0 + 1])
        gr2 = rg_ref[d * npair + p:d * npair + p + 1, :]
        br2 = rb_ref[d * npair + p:d * npair + p + 1, :]
        e2 = jnp.exp(-jnp.abs(gc2 - gr2))
        a2s.append(jnp.where(row2 > col2, (br2 if d else bc2) * qkks[d, p][c:2 * c] * e2, 0.0))
        qkm2s[d, p] = jnp.where((row2 <= col2) if d else (row2 >= col2), qkks[d, p][0:c] * e2, 0.0)

    tbds = dict(zip(items, _unit_lower_inverses(a2s, base)))

    uws = {}
    for d, p in items:
        kf = dirs[d][1][0, p].astype(F32)
        rhs = []
        for hh in (2 * p, 2 * p + 1):
            cb = cb_ref[d * nh + hh]
            vb = dirs[d][2][0, hh].astype(F32) * cb
            kb = kf * (cb * jnp.exp(cg_ref[d * nh + hh]))
            rhs.append(jnp.concatenate([vb, kb], axis=1).astype(BF16))
        tuse = (tbds[d, p].T if d else tbds[d, p]).astype(BF16)
        uws[d, p] = jnp.dot(tuse, jnp.concatenate(rhs, axis=0), preferred_element_type=F32)

    heads = [(d, hh) for d in range(2) for hh in range(nh)]
    ws_qs = {}
    for d, hh in heads:
        w = uws[d, hh // 2][(hh % 2) * c:(hh % 2 + 1) * c, GDN_DIM:2 * GDN_DIM]
        qd = dirs[d][0][0, hh // 2].astype(F32) * jnp.exp(cg_ref[d * nh + hh])
        wq = jnp.concatenate([w.astype(BF16), qd.astype(BF16)], axis=0)
        ws_qs[d, hh] = jnp.dot(wq, s_ref[d * nh + hh].astype(BF16), preferred_element_type=F32)

    for d, hh in heads:
        h = d * nh + hh
        u = uws[d, hh // 2][(hh % 2) * c:(hh % 2 + 1) * c, 0:GDN_DIM]
        v_new = (u - ws_qs[d, hh][0:c]).astype(BF16)
        qkm = qkm2s[d, hh // 2][:, (hh % 2) * c:(hh % 2 + 1) * c].astype(BF16)
        o = ws_qs[d, hh][c:2 * c] + jnp.dot(qkm, v_new, preferred_element_type=F32)
        dirs[d][4][0, hh] = o.astype(of_ref.dtype)
        gt = gt_ref[h:h + 1, :]
        k_dec = (dirs[d][1][0, hh // 2].astype(F32) * jnp.exp(gt - cg_ref[h])).astype(BF16)
        s_ref[h] = s_ref[h] * jnp.exp(gt) + lax.dot_general(k_dec, v_new, TN_DIMS,
                                                           preferred_element_type=F32)


def _gdn_scan_call(qk, v, ab, alog, dt, *, ctx_len=CTX_LEN, base=8):
    b, _, tok, _ = qk.shape
    nc = tok // CHUNK
    ncc = ctx_len // CHUNK

    def back(s):
        return jnp.where(s < ncc, ncc - 1 - s, nc - 1 - (s - ncc))

    qk_blk = (1, GDN_QK_HEADS, CHUNK, LANES)
    v_blk = (1, GDN_V_HEADS, CHUNK, LANES)
    o_shape = jax.ShapeDtypeStruct((b, GDN_V_HEADS, tok, LANES), BF16)
    return pl.pallas_call(
        functools.partial(_gdn_scan_kernel, base=base),
        out_shape=(o_shape, o_shape),
        grid=(b, nc),
        in_specs=[
            pl.BlockSpec(qk_blk, lambda bi, s: (bi, 0, s, 0)),
            pl.BlockSpec(qk_blk, lambda bi, s: (bi, 1, s, 0)),
            pl.BlockSpec(v_blk, lambda bi, s: (bi, 0, s, 0)),
            pl.BlockSpec((1, CHUNK, LANES), lambda bi, s: (bi, s, 0)),
            pl.BlockSpec(qk_blk, lambda bi, s: (bi, 0, back(s), 0)),
            pl.BlockSpec(qk_blk, lambda bi, s: (bi, 1, back(s), 0)),
            pl.BlockSpec(v_blk, lambda bi, s: (bi, 0, back(s), 0)),
            pl.BlockSpec((1, CHUNK, LANES), lambda bi, s: (bi, back(s), 1)),
            pl.BlockSpec((2, 1, LANES), lambda bi, s: (0, 0, 0)),
            pl.BlockSpec((2, 1, LANES), lambda bi, s: (0, 0, 0)),
        ],
        out_specs=(pl.BlockSpec(v_blk, lambda bi, s: (bi, 0, s, 0)),
                   pl.BlockSpec(v_blk, lambda bi, s: (bi, 0, back(s), 0))),
        scratch_shapes=[
            pltpu.VMEM((2 * GDN_V_HEADS, GDN_DIM, GDN_DIM), F32),
            pltpu.VMEM((2 * GDN_V_HEADS, CHUNK, LANES), F32),
            pltpu.VMEM((2 * GDN_V_HEADS, CHUNK, LANES), F32),
            pltpu.VMEM((2 * GDN_QK_HEADS, LANES), F32),
            pltpu.VMEM((2 * GDN_QK_HEADS, LANES), F32),
            pltpu.VMEM((2 * GDN_V_HEADS, LANES), F32),
        ],
        compiler_params=_params(("parallel", "arbitrary")),
        name="gdn_chunk_scan",
    )(qk, qk, v, ab, qk, qk, v, ab, alog, dt)


def _final_norm_kernel(x_ref, w_ref, o_ref):
    xv = x_ref[0]
    o_ref[0] = xv * lax.rsqrt(jnp.mean(xv * xv, axis=-1, keepdims=True) + EPS) * w_ref[...]


def _final_norm_call(x, w, *, ctx_len=CTX_LEN):
    b, tok, d = x.shape
    tr = ctx_len
    return pl.pallas_call(
        _final_norm_kernel,
        out_shape=jax.ShapeDtypeStruct((b, tok - ctx_len, d), F32),
        grid=(b, (tok - ctx_len) // tr),
        in_specs=[pl.BlockSpec((1, tr, d), lambda bi, t: (bi, t + 1, 0)),
                  pl.BlockSpec((1, d), lambda bi, t: (0, 0))],
        out_specs=pl.BlockSpec((1, tr, d), lambda bi, t: (bi, t, 0)),
        compiler_params=_params(("parallel", "parallel")),
        name="final_rmsnorm",
    )(x, w.reshape(1, d))


def _rope_tables(seq, ctx_len, width):
    t = jnp.arange(seq, dtype=jnp.int32)
    rows = (t // GRID_W).astype(F32)
    cols = (t % GRID_W).astype(F32)
    n_freq = DA_QK_DIM // 4
    inv_freq = ROPE_BASE ** (-jnp.arange(n_freq, dtype=F32) / n_freq)
    ar = rows[:, None] * inv_freq
    ac = cols[:, None] * inv_freq
    cos = jnp.concatenate([jnp.cos(ar), jnp.cos(ar), jnp.cos(ac), jnp.cos(ac)], axis=1)
    sin = jnp.concatenate([-jnp.sin(ar), jnp.sin(ar), -jnp.sin(ac), jnp.sin(ac)], axis=1)
    cos = jnp.concatenate([jnp.ones((ctx_len, DA_QK_DIM), F32), cos], axis=0)
    sin = jnp.concatenate([jnp.zeros((ctx_len, DA_QK_DIM), F32), sin], axis=0)
    reps = width // DA_QK_DIM
    return jnp.tile(cos, (1, reps)), jnp.tile(sin, (1, reps))


def kernel(x, c, ctx, c_ctx, w_mod, b_mod, norm_mix, norm_ffn, da_w_qkv, da_lambda, da_head_gain, da_w_o,
           gdn_w_in, gdn_conv, gdn_a_log, gdn_dt_bias, gdn_norm_gain, gdn_w_o, ffn_w_up, ffn_conv,
           ffn_w_down, final_norm):
    bsz, seq, d = x.shape
    ctx_len = ctx.shape[1]
    tm = (ctx_len + seq) // 3
    tm_out = (ctx_len + seq) // 2
    tn_wide = 1024
    tn_ffn = 512
    sw_conv = 512

    xa = jnp.concatenate([ctx, x], axis=1)
    pad_rows = 16 - (bsz + 1)
    cvec = jnp.concatenate([c, c_ctx[None, :], jnp.zeros((pad_rows, d), F32)], axis=0)
    mods_all = _mods_call(cvec, w_mod, b_mod)
    cos_t, sin_t = _rope_tables(seq, ctx_len, MXU_N)

    def sel(mods, idx):
        lat = mods[:bsz, idx * d:(idx + 1) * d]
        cx = jnp.broadcast_to(mods[bsz:bsz + 1, idx * d:(idx + 1) * d], (bsz, d))
        return lat, cx

    def modv(mods, i_shift, i_scale):
        sl, sc = sel(mods, i_shift)
        cl, cc = sel(mods, i_scale)
        return jnp.stack([sl, cl, sc, cc], axis=1)

    def gatev(mods, idx):
        gl, gc = sel(mods, idx)
        return jnp.stack([gl, gc], axis=1)

    for i in range(DEPTH):
        mods = mods_all[i]
        jm = i // 2
        mv = modv(mods, 0, 1)
        if i % 2 == 0:
            lam_init = 0.8 - 0.6 * math.exp(-0.3 * i)
            qkv = _proj_call(xa, mv, norm_mix[i], [da_w_qkv[jm].astype(BF16)], [], epi="rope", tn=tn_wide,
                             tm=tm_out, extras=(cos_t, sin_t), q_tiles=DA_QK_W // tn_wide, ctx_len=ctx_len)
            att = _diff_attn_call(qkv, da_lambda[jm], da_head_gain[jm], lam_init, ctx_len=ctx_len)
            xa = _oproj_call(att, da_w_o[jm].astype(BF16), xa, gatev(mods, 2), tm=tm_out, tn=512,
                             ctx_len=ctx_len)
        else:
            w_in = gdn_w_in[jm].astype(BF16)
            cw = gdn_conv[jm]
            qk = _proj_call(xa, mv, norm_mix[i], [w_in], [cw], epi="gdn_qk", tn=tn_wide, tm=tm, sw=sw_conv,
                            n=2 * GDN_QK_W, col0=0, q_tiles=GDN_QK_W // tn_wide, ctx_len=ctx_len)
            vv = _proj_call(xa, mv, norm_mix[i], [w_in], [cw], epi="gdn_v", tn=tn_wide, tm=tm, sw=sw_conv,
                            n=GDN_V_W, col0=2 * GDN_QK_W, ctx_len=ctx_len)
            z = _proj_call(xa, mv, norm_mix[i], [w_in], [], epi="plain", tn=tn_wide, tm=tm_out,
                           n=GDN_V_W, col0=GDN_QKV_W, ctx_len=ctx_len)
            w_ab = w_in[:, GDN_QKV_W + GDN_V_W:].reshape(d, 2, 2 * GDN_V_HEADS)
            w_ab = jnp.concatenate([w_ab, jnp.zeros_like(w_ab)], axis=2).reshape(d, 2 * LANES)
            ab = _proj_call(xa, mv, norm_mix[i], [w_ab], [], epi="plain", tn=2 * LANES, tm=tm,
                            out_dtype=F32, ctx_len=ctx_len)
            zeros = jnp.zeros((2, GDN_V_HEADS), F32)
            alog = jnp.concatenate([zeros, gdn_a_log[jm], zeros, zeros], axis=1).reshape(2, 1, LANES)
            dtb = jnp.concatenate([zeros, gdn_dt_bias[jm], zeros, zeros], axis=1).reshape(2, 1, LANES)
            o_fwd, o_bwd = _gdn_scan_call(qk, vv, ab, alog, dtb, ctx_len=ctx_len)
            y = _gdn_gate_call(o_fwd, o_bwd, z, gdn_norm_gain[jm], tr=ctx_len)
            xa = _oproj_call(y, gdn_w_o[jm].astype(BF16), xa, gatev(mods, 2), tm=tm_out, tn=512,
                             ctx_len=ctx_len)
        padc = D_FF_PAD - D_FF
        w_up = ffn_w_up[i].astype(BF16)
        w_g = jnp.pad(w_up[:, :D_FF], ((0, 0), (0, padc)))
        w_v = jnp.pad(w_up[:, D_FF:], ((0, 0), (0, padc)))
        cwf = ffn_conv[i]
        cw_g = jnp.pad(cwf[:, :D_FF], ((0, 0), (0, padc)))
        cw_v = jnp.pad(cwf[:, D_FF:], ((0, 0), (0, padc)))
        act = _proj_call(xa, modv(mods, 3, 4), norm_ffn[i], [w_g, w_v], [cw_g, cw_v], epi="ffn", tn=tn_ffn,
                         tm=tm, sw=sw_conv, ctx_len=ctx_len)
        w_dn = jnp.pad(ffn_w_down[i].astype(BF16), ((0, padc), (0, 0)))
        xa = _oproj_call(act, w_dn, xa, gatev(mods, 5), tm=tm_out, tn=512, ctx_len=ctx_len)
    return _final_norm_call(xa, final_norm, ctx_len=ctx_len)
```

```python
import functools
import math

import jax
import jax.numpy as jnp
from jax import lax
from jax.experimental import pallas as pl
from jax.experimental.pallas import tpu as pltpu

F32 = jnp.float32
BF16 = jnp.bfloat16

D_MODEL = 2048
DEPTH = 4
CTX_LEN = 256
GRID_W = 64
EPS = 1e-6
DA_HEADS = 8
DA_QK_DIM = 128
DA_V_DIM = 256
DA_QK_W = DA_HEADS * 2 * DA_QK_DIM
DA_V_W = DA_HEADS * DA_V_DIM
ROPE_BASE = 10000.0
GDN_QK_HEADS = 16
GDN_V_HEADS = 32
GDN_DIM = 128
GDN_QK_W = GDN_QK_HEADS * GDN_DIM
GDN_V_W = GDN_V_HEADS * GDN_DIM
GDN_QKV_W = 2 * GDN_QK_W + GDN_V_W
GDN_CONV = 5
CHUNK = 64
D_FF = 5504
FFN_CONV = 3

LANES = 128
HALO = 16
D_FF_PAD = 5632
VMEM_LIMIT = 56 * 1024 * 1024
LOG2E = math.log2(math.e)

NT_DIMS = (((1,), (1,)), ((), ()))
TN_DIMS = (((0,), (0,)), ((), ()))


def _silu(x):
    return x * jax.nn.sigmoid(x)


def _params(sem):
    return pltpu.CompilerParams(dimension_semantics=sem, vmem_limit_bytes=VMEM_LIMIT)


def _mods_kernel(c_ref, w_ref, b_ref, o_ref):
    s = _silu(c_ref[...]).astype(BF16)
    w = w_ref[0].astype(BF16)
    o_ref[0] = jnp.dot(s, w, preferred_element_type=F32) + b_ref[0]


def _mods_call(cvec, w_mod, b_mod):
    depth, d, n = w_mod.shape
    tn = 1024
    return pl.pallas_call(
        _mods_kernel,
        out_shape=jax.ShapeDtypeStruct((depth, cvec.shape[0], n), F32),
        grid=(depth, n // tn),
        in_specs=[
            pl.BlockSpec((cvec.shape[0], d), lambda i, j: (0, 0)),
            pl.BlockSpec((1, d, tn), lambda i, j: (i, 0, j)),
            pl.BlockSpec((1, 1, tn), lambda i, j: (i, 0, j)),
        ],
        out_specs=pl.BlockSpec((1, cvec.shape[0], tn), lambda i, j: (i, 0, j)),
        compiler_params=_params(("parallel", "parallel")),
        name="adaln_mods",
    )(cvec, w_mod, b_mod.reshape(depth, 1, n))


MXU_N = 256
EPI_ROWS = 128
NORM_ROWS = 16


def _proj_kernel(*refs, n_w, taps, epi, tm, tn, sw, ctx_len, tok, q_tiles):
    halo = taps > 0
    ns = tn // sw
    it = iter(refs)
    x_ref = next(it)
    xp_ref = next(it) if halo else None
    xn_ref = next(it) if halo else None
    modv_ref = next(it)
    nw_ref = next(it)
    w_refs = [next(it) for _ in range(n_w)]
    cw_refs = [next(it) for _ in range(n_w)] if halo else []
    ex_refs = [next(it) for _ in range(2)] if epi == "rope" else []
    out_ref = next(it)
    h_ref = next(it)
    u_refs = [[next(it) for _ in range(ns)] for _ in range(n_w)]

    t = pl.program_id(1)
    j = pl.program_id(2)
    off = HALO if halo else 0
    p = taps // 2

    @pl.when(j == 0)
    def _():
        nw = nw_ref[...]
        gain_l = nw * (1.0 + modv_ref[0, 1:2])
        gain_c = nw * (1.0 + modv_ref[0, 3:4])
        shift_l = modv_ref[0, 0:1]
        shift_c = modv_ref[0, 2:3]

        def norm_rows(src_ref, n_rows, dst0, g0, keep):
            def body(i, carry):
                r = pl.multiple_of(i * NORM_ROWS, NORM_ROWS)
                xv = src_ref[0, pl.ds(r, NORM_ROWS), :]
                inv = lax.rsqrt(jnp.mean(xv * xv, axis=-1, keepdims=True) + EPS)
                is_ctx = (g0 + r + lax.broadcasted_iota(jnp.int32, (NORM_ROWS, 1), 0)) < ctx_len
                hv = (xv * inv) * jnp.where(is_ctx, gain_c, gain_l) + jnp.where(is_ctx, shift_c, shift_l)
                if keep is not None:
                    hv = jnp.where(keep, hv, 0.0)
                h_ref[pl.ds(dst0 + r, NORM_ROWS), :] = hv.astype(BF16)
                return carry
            lax.fori_loop(0, n_rows // NORM_ROWS, body, 0, unroll=8 if n_rows > NORM_ROWS else 1)

        norm_rows(x_ref, tm, off, t * tm, None)
        if halo:
            keep_prev = (t * tm != 0) & (t * tm != ctx_len)
            keep_next = (t * tm + tm != tok) & (t * tm + tm != ctx_len)
            norm_rows(xp_ref, HALO, 0, t * tm - HALO, keep_prev)
            norm_rows(xn_ref, HALO, off + tm, t * tm + tm, keep_next)

    for s in range(ns):
        for w, u_s in zip(w_refs, u_refs):
            u_s[s][...] = jnp.dot(h_ref[...], w[:, s * sw:(s + 1) * sw], preferred_element_type=F32)

    def conv_rows(u_ref, cw_ref, s, r0, nrows, edge):
        if not halo:
            return u_ref[r0:r0 + nrows, :]
        acc = None
        for jj in range(taps):
            dlt = jj - p
            sh = u_ref[off + r0 + dlt:off + r0 + dlt + nrows, :]
            if edge is not None and dlt != 0:
                ri = r0 + lax.broadcasted_iota(jnp.int32, (nrows, 1), 0)
                same = ((ri >= edge) & (ri + dlt >= edge)) | ((ri < edge) & (ri + dlt < edge))
                sh = jnp.where(same, sh, 0.0)
            term = sh * cw_ref[jj:jj + 1, s * sw:(s + 1) * sw]
            acc = term if acc is None else acc + term
        return acc

    def finish(s, r0, nrows, edge=None):
        vals = [conv_rows(u_s[s], cw, s, r0, nrows, edge)
                for u_s, cw in zip(u_refs, cw_refs if halo else [None] * n_w)]
        rs = slice(r0, r0 + nrows)
        cs = slice(s * sw, (s + 1) * sw)
        hpt = sw // LANES
        if epi == "ffn":
            out_ref[0, rs, cs] = (_silu(vals[0]) * vals[1]).astype(out_ref.dtype)
        elif epi == "plain":
            out_ref[0, rs, cs] = vals[0].astype(out_ref.dtype)
        elif epi == "gdn_v":
            a = _silu(vals[0])
            for hh in range(hpt):
                out_ref[0, s * hpt + hh, rs] = a[:, hh * LANES:(hh + 1) * LANES].astype(out_ref.dtype)
        elif epi == "gdn_qk":
            a = _silu(vals[0])
            qs = jnp.where(j < q_tiles, GDN_DIM ** -0.5, 1.0).astype(F32)
            for hh in range(hpt):
                xs = a[:, hh * LANES:(hh + 1) * LANES]
                nrm = xs * lax.rsqrt(jnp.sum(xs * xs, axis=-1, keepdims=True) + EPS)
                out_ref[0, s * hpt + hh, rs] = (nrm * qs).astype(out_ref.dtype)
        elif epi == "rope":
            u = vals[0]
            cos = jnp.where(j < 2 * q_tiles, ex_refs[0][rs, :], 1.0)
            sin = jnp.where(j < 2 * q_tiles, ex_refs[1][rs, :], 0.0)
            lane = lax.broadcasted_iota(jnp.int32, (1, sw), 1)
            swapped = jnp.where((lane % 64) < 32, pltpu.roll(u, sw - 32, 1), pltpu.roll(u, 32, 1))
            sc = jnp.where(j < q_tiles, DA_QK_DIM ** -0.5 * LOG2E, 1.0).astype(F32)
            out_ref[0, rs, cs] = ((u * cos + swapped * sin) * sc).astype(out_ref.dtype)
        else:
            raise ValueError(epi)

    for s in range(ns):
        for r0 in range(0, tm, EPI_ROWS):
            finish(s, r0, EPI_ROWS)

    if halo and ctx_len % tm != 0:
        edge = ctx_len % tm

        @pl.when(t == ctx_len // tm)
        def _():
            for s in range(ns):
                finish(s, edge - HALO, 2 * HALO, edge)


def _proj_call(x, modv, nw, ws, cws, *, epi, tn, tm, sw=MXU_N, n=None, col0=0, out_dtype=BF16, extras=(),
               q_tiles=0, ctx_len=CTX_LEN):
    b, tok, d = x.shape
    n = ws[0].shape[1] if n is None else n
    col0s = col0 if isinstance(col0, tuple) else (col0,) * len(ws)
    j0s = [cc // tn for cc in col0s]
    taps = cws[0].shape[0] if cws else 0
    halo = taps > 0
    nt = tok // tm
    hb = tm // HALO
    nhb = tok // HALO

    in_specs = [pl.BlockSpec((1, tm, d), lambda bi, t, j: (bi, t, 0))]
    args = [x]
    if halo:
        in_specs += [
            pl.BlockSpec((1, HALO, d), lambda bi, t, j: (bi, jnp.maximum(t * hb - 1, 0), 0)),
            pl.BlockSpec((1, HALO, d), lambda bi, t, j: (bi, jnp.minimum((t + 1) * hb, nhb - 1), 0)),
        ]
        args += [x, x]
    in_specs += [pl.BlockSpec((1, 4, d), lambda bi, t, j: (bi, 0, 0)),
                 pl.BlockSpec((1, d), lambda bi, t, j: (0, 0))]
    args += [modv, nw.reshape(1, d)]
    for w, j0 in zip(ws, j0s):
        in_specs.append(pl.BlockSpec((d, tn), lambda bi, t, j, j0=j0: (0, j0 + j)))
        args.append(w)
    for cw, j0 in zip(cws, j0s):
        in_specs.append(pl.BlockSpec((taps, tn), lambda bi, t, j, j0=j0: (0, j0 + j)))
        args.append(cw)
    for e in extras:
        in_specs.append(pl.BlockSpec((tm, sw), lambda bi, t, j: (t, 0)))
        args.append(e)

    if epi in ("gdn_v", "gdn_qk"):
        out_shape = jax.ShapeDtypeStruct((b, n // LANES, tok, LANES), out_dtype)
        out_spec = pl.BlockSpec((1, tn // LANES, tm, LANES), lambda bi, t, j: (bi, j, t, 0))
    else:
        out_shape = jax.ShapeDtypeStruct((b, tok, n), out_dtype)
        out_spec = pl.BlockSpec((1, tm, tn), lambda bi, t, j: (bi, t, j))

    kern = functools.partial(_proj_kernel, n_w=len(ws), taps=taps, epi=epi, tm=tm, tn=tn, sw=sw,
                             ctx_len=ctx_len, tok=tok, q_tiles=q_tiles)
    return pl.pallas_call(
        kern,
        out_shape=out_shape,
        grid=(b, nt, n // tn),
        in_specs=in_specs,
        out_specs=out_spec,
        scratch_shapes=([pltpu.VMEM((tm + (2 * HALO if halo else 0), d), BF16)]
                        + [pltpu.VMEM((tm + (2 * HALO if halo else 0), sw), F32)
                           for _ in range(len(ws) * (tn // sw))]),
        compiler_params=_params(("parallel", "parallel", "arbitrary")),
        name="norm_proj_" + epi,
    )(*args)


def _row_gate(g_ref, t, tm, ctx_len):
    g = t * tm + lax.broadcasted_iota(jnp.int32, (tm, 1), 0)
    return jnp.where(g < ctx_len, g_ref[0, 1:2], g_ref[0, 0:1])


def _oproj_kernel(y_ref, w_ref, x_ref, g_ref, o_ref, *, tm, ctx_len):
    acc = jnp.dot(y_ref[0], w_ref[...], preferred_element_type=F32)
    o_ref[0] = x_ref[0] + _row_gate(g_ref, pl.program_id(1), tm, ctx_len) * acc


def _oproj_call(y, w, x, gate, *, tm, tn, ctx_len=CTX_LEN):
    b, tok, d = x.shape
    k = y.shape[-1]
    return pl.pallas_call(
        functools.partial(_oproj_kernel, tm=tm, ctx_len=ctx_len),
        out_shape=jax.ShapeDtypeStruct(x.shape, F32),
        grid=(b, tok // tm, d // tn),
        in_specs=[
            pl.BlockSpec((1, tm, k), lambda bi, t, j: (bi, t, 0)),
            pl.BlockSpec((k, tn), lambda bi, t, j: (0, j)),
            pl.BlockSpec((1, tm, tn), lambda bi, t, j: (bi, t, j)),
            pl.BlockSpec((1, 2, tn), lambda bi, t, j: (bi, 0, j)),
        ],
        out_specs=pl.BlockSpec((1, tm, tn), lambda bi, t, j: (bi, t, j)),
        compiler_params=_params(("parallel", "parallel", "arbitrary")),
        name="out_proj_residual",
    )(y, w, x, gate)


def _gdn_gate_kernel(of_ref, ob_ref, z_ref, gain_ref, y_ref):
    gain = gain_ref[...]
    for h in range(GDN_V_HEADS):
        o = of_ref[0, h].astype(F32) + ob_ref[0, h].astype(F32)
        y = o * lax.rsqrt(jnp.mean(o * o, axis=-1, keepdims=True) + EPS) * gain
        z = z_ref[0, :, h * LANES:(h + 1) * LANES].astype(F32)
        y_ref[0, :, h * LANES:(h + 1) * LANES] = (y * _silu(z)).astype(BF16)


def _gdn_gate_call(o_fwd, o_bwd, z, gain, *, tr):
    b, tok, k = z.shape
    return pl.pallas_call(
        _gdn_gate_kernel,
        out_shape=jax.ShapeDtypeStruct(z.shape, BF16),
        grid=(b, tok // tr),
        in_specs=[
            pl.BlockSpec((1, GDN_V_HEADS, tr, LANES), lambda bi, t: (bi, 0, t, 0)),
            pl.BlockSpec((1, GDN_V_HEADS, tr, LANES), lambda bi, t: (bi, 0, t, 0)),
            pl.BlockSpec((1, tr, k), lambda bi, t: (bi, t, 0)),
            pl.BlockSpec((1, LANES), lambda bi, t: (0, 0)),
        ],
        out_specs=pl.BlockSpec((1, tr, k), lambda bi, t: (bi, t, 0)),
        compiler_params=_params(("parallel", "parallel")),
        name="gdn_gated_norm",
    )(o_fwd, o_bwd, z, gain.reshape(1, LANES))


ATT_HEADS = 2


def _diff_attn_kernel(q_ref, k_ref, v_ref, lam_ref, gain_ref, o_ref, *, lam_init, ctx_len):
    lv = lam_ref[...]
    lam = (jnp.exp(jnp.sum(lv[0:1] * lv[1:2], axis=-1, keepdims=True))
           - jnp.exp(jnp.sum(lv[2:3] * lv[3:4], axis=-1, keepdims=True)) + lam_init)

    def attend(kk, vv):
        q = q_ref[0]
        scores = []
        for hc in range(2 * ATT_HEADS):
            cs = slice(hc * DA_QK_DIM, (hc + 1) * DA_QK_DIM)
            scores.append(lax.dot_general(q[:, cs], kk[:, cs], NT_DIMS, preferred_element_type=F32))
        for h in range(ATT_HEADS):
            ps, ls = [], []
            for c in range(2):
                s = scores[2 * h + c]
                p = jnp.exp2(s - jnp.max(s, axis=-1, keepdims=True))
                ps.append(p)
                ls.append(jnp.sum(p, axis=-1, keepdims=True))
            inv0 = 1.0 / ls[0]
            a = ps[0] - (lam * ls[0] * (1.0 / ls[1])) * ps[1]
            vs = slice(h * DA_V_DIM, (h + 1) * DA_V_DIM)
            o = jnp.dot(a.astype(BF16), vv[:, vs], preferred_element_type=F32) * inv0
            o = o * lax.rsqrt(jnp.mean(o * o, axis=-1, keepdims=True) + EPS) * gain_ref[...]
            o_ref[0, :, vs] = (o * (1.0 - lam_init)).astype(o_ref.dtype)

    t = pl.program_id(2)

    @pl.when(t == 0)
    def _():
        attend(k_ref[0, 0:ctx_len], v_ref[0, 0:ctx_len])

    @pl.when(t != 0)
    def _():
        attend(k_ref[0], v_ref[0])


def _diff_attn_call(qkv, lam_vecs, head_gain, lam_init, *, ctx_len=CTX_LEN):
    b, tok, _ = qkv.shape
    hw = ATT_HEADS * 2 * DA_QK_DIM
    ng = DA_HEADS // ATT_HEADS
    tq = ctx_len
    return pl.pallas_call(
        functools.partial(_diff_attn_kernel, lam_init=lam_init, ctx_len=ctx_len),
        out_shape=jax.ShapeDtypeStruct((b, tok, DA_V_W), BF16),
        grid=(b, ng, tok // tq),
        in_specs=[
            pl.BlockSpec((1, tq, hw), lambda bi, h, t: (bi, t, h)),
            pl.BlockSpec((1, tok, hw), lambda bi, h, t: (bi, 0, ng + h)),
            pl.BlockSpec((1, tok, hw), lambda bi, h, t: (bi, 0, 2 * ng + h)),
            pl.BlockSpec((4, DA_QK_DIM), lambda bi, h, t: (0, 0)),
            pl.BlockSpec((1, DA_V_DIM), lambda bi, h, t: (0, 0)),
        ],
        out_specs=pl.BlockSpec((1, tq, hw), lambda bi, h, t: (bi, t, h)),
        compiler_params=_params(("parallel", "parallel", "arbitrary")),
        name="diff_attention",
    )(qkv, qkv, qkv, lam_vecs, head_gain.reshape(1, DA_V_DIM))


def _split3(x):
    x1 = x.astype(BF16)
    r1 = x - x1.astype(F32)
    x2 = r1.astype(BF16)
    x3 = (r1 - x2.astype(F32)).astype(BF16)
    return x1, x2, x3


def _blockdiag(x2):
    first = lax.broadcasted_iota(jnp.int32, x2.shape, 1) < CHUNK
    return jnp.concatenate([jnp.where(first, x2, 0.0), jnp.where(first, 0.0, x2)], axis=0)


def _unit_lower_inverses(a2s, base):
    c = CHUNK
    nt = c // base
    assert base == 8
    lane = lax.broadcasted_iota(jnp.int32, (base, LANES), 1)
    sub = lax.broadcasted_iota(jnp.int32, (base, LANES), 0)
    row2 = lax.broadcasted_iota(jnp.int32, (c, LANES), 0)
    col2 = lax.broadcasted_iota(jnp.int32, (c, LANES), 1) % c

    in_blk = [((lane % c) // base) == r for r in range(nt)]
    dgs = []
    for a2 in a2s:
        dg = jnp.where(in_blk[0], a2[0:base], 0.0)
        for r in range(1, nt):
            dg = jnp.where(in_blk[r], a2[base * r:base * (r + 1)], dg)
        dgs.append(dg)
    ts = [jnp.where((lane % base) == sub, 1.0, 0.0).astype(F32) for _ in a2s]
    for jcol in range(base - 1):
        idx = (lane // base) * base + jcol
        for i, dg in enumerate(dgs):
            acol = jnp.take_along_axis(dg, idx, axis=1)
            ts[i] = ts[i] - acol * ts[i][jcol:jcol + 1, :]
    tbds = [_blockdiag(jnp.concatenate([jnp.where(in_blk[r], t, 0.0) for r in range(nt)], axis=0))
            for t in ts]

    size = base
    while size < c:
        off = ((row2 // (2 * size)) == (col2 // (2 * size))) & ((row2 // size) != (col2 // size))
        tbs = [tbd.astype(BF16) for tbd in tbds]
        tqs = [jnp.dot(tb, _blockdiag(jnp.where(off, a2, 0.0)).astype(BF16), preferred_element_type=F32)
               for tb, a2 in zip(tbs, a2s)]
        tbds = [tbd - jnp.dot(tq.astype(BF16), tb, preferred_element_type=F32)
                for tbd, tq, tb in zip(tbds, tqs, tbs)]
        size *= 2
    return tbds


def _gdn_scan_kernel(qf_ref, kf_ref, vf_ref, abf_ref, qb_ref, kb_ref, vb_ref, abb_ref, alog_ref, dt_ref,
                     of_ref, ob_ref, s_ref, cg_ref, cb_ref, rg_ref, rb_ref, gt_ref, *, base):
    c = CHUNK
    nh, npair = GDN_V_HEADS, GDN_QK_HEADS
    dirs = ((qf_ref, kf_ref, vf_ref, abf_ref, of_ref), (qb_ref, kb_ref, vb_ref, abb_ref, ob_ref))

    @pl.when(pl.program_id(1) == 0)
    def _():
        s_ref[...] = jnp.zeros(s_ref.shape, F32)

    ri = lax.broadcasted_iota(jnp.int32, (c, c), 0)
    ci = lax.broadcasted_iota(jnp.int32, (c, c), 1)
    pi = lax.broadcasted_iota(jnp.int32, (npair, LANES), 0)
    li = lax.broadcasted_iota(jnp.int32, (npair, LANES), 1)
    zpad = jnp.zeros((c, LANES), BF16)

    def pair_rows(x, col0):
        sel_e = jnp.where(li == col0 + 2 * pi, 1.0, 0.0).astype(BF16)
        sel_o = jnp.where(li == col0 + 2 * pi + 1, 1.0, 0.0).astype(BF16)
        acc = jnp.zeros((npair, LANES), F32)
        for part in _split3(x):
            acc = acc + lax.dot_general(sel_e, jnp.concatenate([part, zpad], axis=0), NT_DIMS,
                                        preferred_element_type=F32)
            acc = acc + lax.dot_general(sel_o, jnp.concatenate([zpad, part], axis=0), NT_DIMS,
                                        preferred_element_type=F32)
        return acc

    for d, (_, _, _, ab_ref, _) in enumerate(dirs):
        ab = ab_ref[0]
        beta_all = jax.nn.sigmoid(ab)
        xs = ab + dt_ref[d]
        g_all = -jnp.exp(alog_ref[d]) * (jnp.maximum(xs, 0.0) + jnp.log1p(jnp.exp(-jnp.abs(xs))))
        cum = jnp.where((ci >= ri) if d else (ci <= ri), 1.0, 0.0).astype(BF16)
        gam_all = sum(jnp.dot(cum, part, preferred_element_type=F32) for part in _split3(g_all))
        gtot_all = jnp.sum(g_all, axis=0, keepdims=True)
        rg_ref[d * npair:(d + 1) * npair, :] = pair_rows(gam_all, nh)
        rb_ref[d * npair:(d + 1) * npair, :] = pair_rows(beta_all, 0)
        for h in range(nh):
            cg_ref[d * nh + h] = jnp.broadcast_to(gam_all[:, nh + h:nh + h + 1], (c, LANES))
            cb_ref[d * nh + h] = jnp.broadcast_to(beta_all[:, h:h + 1], (c, LANES))
            gt_ref[d * nh + h:d * nh + h + 1, :] = jnp.broadcast_to(gtot_all[:, nh + h:nh + h + 1],
                                                                    (1, LANES))

    row2 = lax.broadcasted_iota(jnp.int32, (c, LANES), 0)
    lane2 = lax.broadcasted_iota(jnp.int32, (c, LANES), 1)
    col2 = lane2 % c
    first2 = lane2 < c

    items = [(d, p) for d in range(2) for p in range(npair)]

    qkks = {}
    for d, p in items:
        q = dirs[d][0][0, p]
        k = dirs[d][1][0, p]
        qkks[d, p] = lax.dot_general(jnp.concatenate([q, k], axis=0), jnp.concatenate([k, k], axis=0),
                                     NT_DIMS, preferred_element_type=F32)
    a2s, qkm2s = [], {}
    for d, p in items:
        h0 = d * nh + 2 * p
        gc2 = jnp.where(first2, cg_ref[h0], cg_ref[h0 + 1])
        bc2 = jnp.where(first2, cb_ref[h0], cb_ref[h0 + 1])
        gr2 = rg_ref[d * npair + p:d * npair + p + 1, :]
        br2 = rb_ref[d * npair + p:d * npair + p + 1, :]
        e2 = jnp.exp(-jnp.abs(gc2 - gr2))
        a2s.append(jnp.where(row2 > col2, (br2 if d else bc2) * qkks[d, p][c:2 * c] * e2, 0.0))
        qkm2s[d, p] = jnp.where((row2 <= col2) if d else (row2 >= col2), qkks[d, p][0:c] * e2, 0.0)

    tbds = dict(zip(items, _unit_lower_inverses(a2s, base)))

    uws = {}
    for d, p in items:
        kf = dirs[d][1][0, p].astype(F32)
        rhs = []
        for hh in (2 * p, 2 * p + 1):
            cb = cb_ref[d * nh + hh]
            vb = dirs[d][2][0, hh].astype(F32) * cb
            kb = kf * (cb * jnp.exp(cg_ref[d * nh + hh]))
            rhs.append(jnp.concatenate([vb, kb], axis=1).astype(BF16))
        tuse = (tbds[d, p].T if d else tbds[d, p]).astype(BF16)
        uws[d, p] = jnp.dot(tuse, jnp.concatenate(rhs, axis=0), preferred_element_type=F32)

    heads = [(d, hh) for d in range(2) for hh in range(nh)]
    ws_qs = {}
    for d, hh in heads:
        w = uws[d, hh // 2][(hh % 2) * c:(hh % 2 + 1) * c, GDN_DIM:2 * GDN_DIM]
        qd = dirs[d][0][0, hh // 2].astype(F32) * jnp.exp(cg_ref[d * nh + hh])
        wq = jnp.concatenate([w.astype(BF16), qd.astype(BF16)], axis=0)
        ws_qs[d, hh] = jnp.dot(wq, s_ref[d * nh + hh].astype(BF16), preferred_element_type=F32)

    for d, hh in heads:
        h = d * nh + hh
        u = uws[d, hh // 2][(hh % 2) * c:(hh % 2 + 1) * c, 0:GDN_DIM]
        v_new = (u - ws_qs[d, hh][0:c]).astype(BF16)
        qkm = qkm2s[d, hh // 2][:, (hh % 2) * c:(hh % 2 + 1) * c].astype(BF16)
        o = ws_qs[d, hh][c:2 * c] + jnp.dot(qkm, v_new, preferred_element_type=F32)
        dirs[d][4][0, hh] = o.astype(of_ref.dtype)
        gt = gt_ref[h:h + 1, :]
        k_dec = (dirs[d][1][0, hh // 2].astype(F32) * jnp.exp(gt - cg_ref[h])).astype(BF16)
        s_ref[h] = s_ref[h] * jnp.exp(gt) + lax.dot_general(k_dec, v_new, TN_DIMS,
                                                           preferred_element_type=F32)


def _gdn_scan_call(qk, v, ab, alog, dt, *, ctx_len=CTX_LEN, base=8):
    b, _, tok, _ = qk.shape
    nc = tok // CHUNK
    ncc = ctx_len // CHUNK

    def back(s):
        return jnp.where(s < ncc, ncc - 1 - s, nc - 1 - (s - ncc))

    qk_blk = (1, GDN_QK_HEADS, CHUNK, LANES)
    v_blk = (1, GDN_V_HEADS, CHUNK, LANES)
    o_shape = jax.ShapeDtypeStruct((b, GDN_V_HEADS, tok, LANES), BF16)
    return pl.pallas_call(
        functools.partial(_gdn_scan_kernel, base=base),
        out_shape=(o_shape, o_shape),
        grid=(b, nc),
        in_specs=[
            pl.BlockSpec(qk_blk, lambda bi, s: (bi, 0, s, 0)),
            pl.BlockSpec(qk_blk, lambda bi, s: (bi, 1, s, 0)),
            pl.BlockSpec(v_blk, lambda bi, s: (bi, 0, s, 0)),
            pl.BlockSpec((1, CHUNK, LANES), lambda bi, s: (bi, s, 0)),
            pl.BlockSpec(qk_blk, lambda bi, s: (bi, 0, back(s), 0)),
            pl.BlockSpec(qk_blk, lambda bi, s: (bi, 1, back(s), 0)),
            pl.BlockSpec(v_blk, lambda bi, s: (bi, 0, back(s), 0)),
            pl.BlockSpec((1, CHUNK, LANES), lambda bi, s: (bi, back(s), 1)),
            pl.BlockSpec((2, 1, LANES), lambda bi, s: (0, 0, 0)),
            pl.BlockSpec((2, 1, LANES), lambda bi, s: (0, 0, 0)),
        ],
        out_specs=(pl.BlockSpec(v_blk, lambda bi, s: (bi, 0, s, 0)),
                   pl.BlockSpec(v_blk, lambda bi, s: (bi, 0, back(s), 0))),
        scratch_shapes=[
            pltpu.VMEM((2 * GDN_V_HEADS, GDN_DIM, GDN_DIM), F32),
            pltpu.VMEM((2 * GDN_V_HEADS, CHUNK, LANES), F32),
            pltpu.VMEM((2 * GDN_V_HEADS, CHUNK, LANES), F32),
            pltpu.VMEM((2 * GDN_QK_HEADS, LANES), F32),
            pltpu.VMEM((2 * GDN_QK_HEADS, LANES), F32),
            pltpu.VMEM((2 * GDN_V_HEADS, LANES), F32),
        ],
        compiler_params=_params(("parallel", "arbitrary")),
        name="gdn_chunk_scan",
    )(qk, qk, v, ab, qk, qk, v, ab, alog, dt)


def _final_norm_kernel(x_ref, w_ref, o_ref):
    xv = x_ref[0]
    o_ref[0] = xv * lax.rsqrt(jnp.mean(xv * xv, axis=-1, keepdims=True) + EPS) * w_ref[...]


def _final_norm_call(x, w, *, ctx_len=CTX_LEN):
    b, tok, d = x.shape
    tr = ctx_len
    return pl.pallas_call(
        _final_norm_kernel,
        out_shape=jax.ShapeDtypeStruct((b, tok - ctx_len, d), F32),
        grid=(b, (tok - ctx_len) // tr),
        in_specs=[pl.BlockSpec((1, tr, d), lambda bi, t: (bi, t + 1, 0)),
                  pl.BlockSpec((1, d), lambda bi, t: (0, 0))],
        out_specs=pl.BlockSpec((1, tr, d), lambda bi, t: (bi, t, 0)),
        compiler_params=_params(("parallel", "parallel")),
        name="final_rmsnorm",
    )(x, w.reshape(1, d))


def _rope_tables(seq, ctx_len, width):
    t = jnp.arange(seq, dtype=jnp.int32)
    rows = (t // GRID_W).astype(F32)
    cols = (t % GRID_W).astype(F32)
    n_freq = DA_QK_DIM // 4
    inv_freq = ROPE_BASE ** (-jnp.arange(n_freq, dtype=F32) / n_freq)
    ar = rows[:, None] * inv_freq
    ac = cols[:, None] * inv_freq
    cos = jnp.concatenate([jnp.cos(ar), jnp.cos(ar), jnp.cos(ac), jnp.cos(ac)], axis=1)
    sin = jnp.concatenate([-jnp.sin(ar), jnp.sin(ar), -jnp.sin(ac), jnp.sin(ac)], axis=1)
    cos = jnp.concatenate([jnp.ones((ctx_len, DA_QK_DIM), F32), cos], axis=0)
    sin = jnp.concatenate([jnp.zeros((ctx_len, DA_QK_DIM), F32), sin], axis=0)
    reps = width // DA_QK_DIM
    return jnp.tile(cos, (1, reps)), jnp.tile(sin, (1, reps))


def kernel(x, c, ctx, c_ctx, w_mod, b_mod, norm_mix, norm_ffn, da_w_qkv, da_lambda, da_head_gain, da_w_o,
           gdn_w_in, gdn_conv, gdn_a_log, gdn_dt_bias, gdn_norm_gain, gdn_w_o, ffn_w_up, ffn_conv,
           ffn_w_down, final_norm):
    bsz, seq, d = x.shape
    ctx_len = ctx.shape[1]
    tm = (ctx_len + seq) // 3
    tm_out = (ctx_len + seq) // 2
    tn_wide = 1024
    tn_ffn = 512
    sw_conv = 512

    xa = jnp.concatenate([ctx, x], axis=1)
    pad_rows = 16 - (bsz + 1)
    cvec = jnp.concatenate([c, c_ctx[None, :], jnp.zeros((pad_rows, d), F32)], axis=0)
    mods_all = _mods_call(cvec, w_mod, b_mod)
    cos_t, sin_t = _rope_tables(seq, ctx_len, MXU_N)

    def sel(mods, idx):
        lat = mods[:bsz, idx * d:(idx + 1) * d]
        cx = jnp.broadcast_to(mods[bsz:bsz + 1, idx * d:(idx + 1) * d], (bsz, d))
        return lat, cx

    def modv(mods, i_shift, i_scale):
        sl, sc = sel(mods, i_shift)
        cl, cc = sel(mods, i_scale)
        return jnp.stack([sl, cl, sc, cc], axis=1)

    def gatev(mods, idx):
        gl, gc = sel(mods, idx)
        return jnp.stack([gl, gc], axis=1)

    for i in range(DEPTH):
        mods = mods_all[i]
        jm = i // 2
        mv = modv(mods, 0, 1)
        if i % 2 == 0:
            lam_init = 0.8 - 0.6 * math.exp(-0.3 * i)
            qkv = _proj_call(xa, mv, norm_mix[i], [da_w_qkv[jm].astype(BF16)], [], epi="rope", tn=tn_wide,
                             tm=tm_out, extras=(cos_t, sin_t), q_tiles=DA_QK_W // tn_wide, ctx_len=ctx_len)
            att = _diff_attn_call(qkv, da_lambda[jm], da_head_gain[jm], lam_init, ctx_len=ctx_len)
            xa = _oproj_call(att, da_w_o[jm].astype(BF16), xa, gatev(mods, 2), tm=tm_out, tn=512,
                             ctx_len=ctx_len)
        else:
            w_in = gdn_w_in[jm].astype(BF16)
            cw = gdn_conv[jm]
            qk = _proj_call(xa, mv, norm_mix[i], [w_in], [cw], epi="gdn_qk", tn=tn_wide, tm=tm, sw=sw_conv,
                            n=2 * GDN_QK_W, col0=0, q_tiles=GDN_QK_W // tn_wide, ctx_len=ctx_len)
            vv = _proj_call(xa, mv, norm_mix[i], [w_in], [cw], epi="gdn_v", tn=tn_wide, tm=tm, sw=sw_conv,
                            n=GDN_V_W, col0=2 * GDN_QK_W, ctx_len=ctx_len)
            z = _proj_call(xa, mv, norm_mix[i], [w_in], [], epi="plain", tn=tn_wide, tm=tm_out,
                           n=GDN_V_W, col0=GDN_QKV_W, ctx_len=ctx_len)
            w_ab = w_in[:, GDN_QKV_W + GDN_V_W:].reshape(d, 2, 2 * GDN_V_HEADS)
            w_ab = jnp.concatenate([w_ab, jnp.zeros_like(w_ab)], axis=2).reshape(d, 2 * LANES)
            ab = _proj_call(xa, mv, norm_mix[i], [w_ab], [], epi="plain", tn=2 * LANES, tm=tm,
                            out_dtype=F32, ctx_len=ctx_len)
            zeros = jnp.zeros((2, GDN_V_HEADS), F32)
            alog = jnp.concatenate([zeros, gdn_a_log[jm], zeros, zeros], axis=1).reshape(2, 1, LANES)
            dtb = jnp.concatenate([zeros, gdn_dt_bias[jm], zeros, zeros], axis=1).reshape(2, 1, LANES)
            o_fwd, o_bwd = _gdn_scan_call(qk, vv, ab, alog, dtb, ctx_len=ctx_len)
            y = _gdn_gate_call(o_fwd, o_bwd, z, gdn_norm_gain[jm], tr=ctx_len)
            xa = _oproj_call(y, gdn_w_o[jm].astype(BF16), xa, gatev(mods, 2), tm=tm_out, tn=512,
                             ctx_len=ctx_len)
        padc = D_FF_PAD - D_FF

        def halves_padded(a, dtype):
            zpad = jnp.zeros((a.shape[0], padc), dtype)
            return jnp.concatenate([a[:, :D_FF].astype(dtype), zpad, a[:, D_FF:].astype(dtype), zpad], axis=1)

        w_gv = halves_padded(ffn_w_up[i], BF16)
        cw_gv = halves_padded(ffn_conv[i], F32)
        act = _proj_call(xa, modv(mods, 3, 4), norm_ffn[i], [w_gv, w_gv], [cw_gv, cw_gv], epi="ffn",
                         tn=tn_ffn, tm=tm, sw=sw_conv, n=D_FF_PAD, col0=(0, D_FF_PAD), ctx_len=ctx_len)
        w_dn = jnp.pad(ffn_w_down[i].astype(BF16), ((0, padc), (0, 0)))
        xa = _oproj_call(act, w_dn, xa, gatev(mods, 5), tm=tm_out, tn=512, ctx_len=ctx_len)
    return _final_norm_call(xa, final_norm, ctx_len=ctx_len)
```
